```python
import jax, jax.numpy as jnp
from jax import lax
import numpy as np

D_MODEL = 2048
BATCH = 1
SEQ = 8192
DEPTH = 4

MLA_HEADS = D_MODEL // 128
Q_LORA = 512
KV_LORA = 512
NOPE_DIM = 128
ROPE_DIM = 64
V_DIM = 128
ROPE_THETA = 10000.0
Q_BLOCK = 128
MASK_VALUE = -1e30
HGRN_HEADS = D_MODEL // 128
HGRN_DK = 128
HGRN_DV = 128
HGRN_K = HGRN_HEADS * HGRN_DK
HGRN_V = HGRN_HEADS * HGRN_DV
CHUNK = 64
MIN_FORGET = 1e-30
D_FF = ((8 * D_MODEL // 3 + 255) // 256) * 256
CONV_WIDTH = 3
EPS = 1e-6
IN_SPLITS = (Q_LORA, KV_LORA, ROPE_DIM, HGRN_K, HGRN_K, HGRN_V, HGRN_V, D_MODEL, D_MODEL)
IN_WIDTH = sum(IN_SPLITS)

kernel_name = "hybrid_mla_hgrn2_convffn_trunk"


def rmsnorm(x, g):
    xf = x.astype(jnp.float32)
    y = xf * lax.rsqrt(jnp.mean(xf * xf, axis=-1, keepdims=True) + EPS)
    return (y * g.astype(jnp.float32)).astype(x.dtype)


def rope_tables(positions):
    inv_freq = ROPE_THETA ** (-jnp.arange(0, ROPE_DIM, 2, dtype=jnp.float32) / ROPE_DIM)
    ang = positions.astype(jnp.float32)[..., None] * inv_freq
    return jnp.cos(ang)[:, :, None, :], jnp.sin(ang)[:, :, None, :]


def apply_rope(x, cos, sin):
    xf = x.astype(jnp.float32)
    x1, x2 = jnp.split(xf, 2, axis=-1)
    return jnp.concatenate([x1 * cos - x2 * sin, x2 * cos + x1 * sin], axis=-1).astype(x.dtype)


def causal_block_attention(q, k, v):
    B, S, H, Dqk = q.shape
    nb = S // Q_BLOCK
    scale = Dqk ** -0.5
    qb = q.reshape(B, nb, Q_BLOCK, H, Dqk).transpose(1, 0, 2, 3, 4)
    key_pos = jnp.arange(S)

    def one_block(args):
        q_blk, blk = args
        s = jnp.einsum('bqhd,bkhd->bhqk', q_blk, k).astype(jnp.float32) * scale
        q_pos = blk * Q_BLOCK + jnp.arange(Q_BLOCK)
        s = jnp.where(key_pos[None, :] <= q_pos[:, None], s, MASK_VALUE)
        p = jax.nn.softmax(s, axis=-1).astype(v.dtype)
        return jnp.einsum('bhqk,bkhd->bqhd', p, v)

    out = lax.map(one_block, (qb, jnp.arange(nb)))
    return out.transpose(1, 0, 2, 3, 4).reshape(B, S, H, v.shape[-1])


def hgrn2_chunkwise(q, k, log_f, v):
    B, S, H, DK = q.shape
    DV = v.shape[-1]
    nc = S // CHUNK

    def to_chunks(t):
        return t.astype(jnp.float32).reshape(B, nc, CHUNK, H, t.shape[-1]).transpose(1, 0, 3, 2, 4)

    qc, kc, gc, vc = to_chunks(q), to_chunks(k), to_chunks(log_f), to_chunks(v)
    causal = jnp.tril(jnp.ones((CHUNK, CHUNK), dtype=bool))[:, :, None]

    def step(state, inp):
        q_, k_, g_, v_ = inp
        b = jnp.cumsum(g_, axis=-2)
        diff = b[..., :, None, :] - b[..., None, :, :]
        decay = jnp.where(causal, jnp.exp(jnp.where(causal, diff, 0.0)), 0.0)
        scores = jnp.einsum('bhtd,bhsd,bhtsd->bhts', q_, k_, decay)
        o_intra = jnp.einsum('bhts,bhsv->bhtv', scores, v_)
        o_inter = jnp.einsum('bhtd,bhdv->bhtv', q_ * jnp.exp(b), state)
        b_last = b[..., -1:, :]
        k_dec = k_ * jnp.exp(b_last - b)
        new_state = state * jnp.exp(b_last[..., 0, :])[..., None] + jnp.einsum('bhsd,bhsv->bhdv', k_dec, v_)
        return new_state, o_intra + o_inter

    state0 = jnp.zeros((B, H, DK, DV), jnp.float32)
    _, out = lax.scan(step, state0, (qc, kc, gc, vc))
    return out.transpose(1, 0, 3, 2, 4).reshape(B, S, H, DV)


def causal_dwconv(u, w, b):
    S = u.shape[1]
    up = jnp.pad(u, ((0, 0), (CONV_WIDTH - 1, 0), (0, 0)))
    y = b
    for j in range(CONV_WIDTH):
        y = y + w[j] * up[:, j:j + S, :]
    return y


def setup_inputs(seed: int = 0) -> dict:
    key = jax.random.key(seed)
    ks = jax.random.split(key, 24)
    f32 = jnp.float32

    def nrm(k, shape, scale):
        return jax.random.normal(k, shape, f32) * scale

    def gain(k, shape):
        return 1.0 + 0.02 * jax.random.normal(k, shape, f32)

    res_scale = (2.0 * DEPTH) ** -0.5
    x = jax.random.normal(ks[0], (BATCH, SEQ, D_MODEL), f32)
    offsets = jax.random.randint(ks[1], (BATCH, 1), 0, 4096, dtype=jnp.int32)
    positions = offsets + jnp.arange(SEQ, dtype=jnp.int32)[None, :]
    return {
        "x": x,
        "positions": positions,
        "attn_norm_g": gain(ks[2], (DEPTH, D_MODEL)),
        "w_in": nrm(ks[3], (DEPTH, D_MODEL, IN_WIDTH), D_MODEL ** -0.5),
        "q_norm_g": gain(ks[4], (DEPTH, Q_LORA)),
        "w_uq": nrm(ks[5], (DEPTH, Q_LORA, MLA_HEADS * (NOPE_DIM + ROPE_DIM)), Q_LORA ** -0.5),
        "kv_norm_g": gain(ks[6], (DEPTH, KV_LORA)),
        "w_ukv": nrm(ks[7], (DEPTH, KV_LORA, MLA_HEADS * (NOPE_DIM + V_DIM)), KV_LORA ** -0.5),
        "hgrn_lb_logits": nrm(ks[8], (DEPTH, HGRN_K), 0.5),
        "hgrn_out_norm_g": gain(ks[9], (DEPTH, HGRN_DV)),
        "w_branch_a": nrm(ks[10], (DEPTH, MLA_HEADS * V_DIM, D_MODEL), (MLA_HEADS * V_DIM) ** -0.5),
        "w_branch_b": nrm(ks[11], (DEPTH, HGRN_V, D_MODEL), HGRN_V ** -0.5),
        "w_out": nrm(ks[12], (DEPTH, D_MODEL, D_MODEL), D_MODEL ** -0.5 * res_scale),
        "ffn_norm_g": gain(ks[13], (DEPTH, D_MODEL)),
        "w_up": nrm(ks[14], (DEPTH, D_MODEL, 2 * D_FF), D_MODEL ** -0.5),
        "conv_w": nrm(ks[15], (DEPTH, CONV_WIDTH, 2 * D_FF), CONV_WIDTH ** -0.5),
        "conv_b": nrm(ks[16], (DEPTH, 2 * D_FF), 0.01),
        "w_down": nrm(ks[17], (DEPTH, D_FF, D_MODEL), D_FF ** -0.5 * res_scale),
        "final_norm_g": gain(ks[18], (D_MODEL,)),
    }


def reference(x, positions, attn_norm_g, w_in, q_norm_g, w_uq, kv_norm_g, w_ukv, hgrn_lb_logits,
              hgrn_out_norm_g, w_branch_a, w_branch_b, w_out, ffn_norm_g, w_up, conv_w, conv_b,
              w_down, final_norm_g):
    B, S, _ = x.shape
    cos, sin = rope_tables(positions)
    p = jax.nn.softmax(hgrn_lb_logits.astype(jnp.float32), axis=0)
    lower_bounds = jnp.cumsum(p, axis=0) - p[0]
    split_points = [int(s) for s in np.cumsum(IN_SPLITS)[:-1]]

    for l in range(DEPTH):
        h = rmsnorm(x, attn_norm_g[l])
        proj = h @ w_in[l]
        c_q, c_kv, k_rope, hq, hf, hi, hg, gate_a, gate_b = jnp.split(proj, split_points, axis=-1)

        q = (rmsnorm(c_q, q_norm_g[l]) @ w_uq[l]).reshape(B, S, MLA_HEADS, NOPE_DIM + ROPE_DIM)
        kv = (rmsnorm(c_kv, kv_norm_g[l]) @ w_ukv[l]).reshape(B, S, MLA_HEADS, NOPE_DIM + V_DIM)
        q_nope, q_pe = q[..., :NOPE_DIM], q[..., NOPE_DIM:]
        k_nope, v = kv[..., :NOPE_DIM], kv[..., NOPE_DIM:]
        q_pe = apply_rope(q_pe, cos, sin)
        k_pe = jnp.broadcast_to(apply_rope(k_rope[:, :, None, :], cos, sin), (B, S, MLA_HEADS, ROPE_DIM))
        attn = causal_block_attention(jnp.concatenate([q_nope, q_pe], axis=-1),
                                      jnp.concatenate([k_nope, k_pe], axis=-1), v)
        attn = attn.reshape(B, S, MLA_HEADS * V_DIM)

        lb = lower_bounds[l]
        z = hf.astype(jnp.float32)
        forget = lb + (1.0 - lb) * jax.nn.sigmoid(z)
        log_f = jnp.log(jnp.maximum(forget, MIN_FORGET))
        k_in = 1.0 - forget
        rec = hgrn2_chunkwise(jax.nn.silu(hq).reshape(B, S, HGRN_HEADS, HGRN_DK),
                              k_in.reshape(B, S, HGRN_HEADS, HGRN_DK),
                              log_f.reshape(B, S, HGRN_HEADS, HGRN_DK),
                              hi.reshape(B, S, HGRN_HEADS, HGRN_DV)).astype(x.dtype)
        rec = rmsnorm(rec, hgrn_out_norm_g[l]) * jax.nn.silu(hg.reshape(B, S, HGRN_HEADS, HGRN_DV))
        rec = rec.reshape(B, S, HGRN_V)

        merged = (jax.nn.sigmoid(gate_a) * (attn @ w_branch_a[l])
                  + jax.nn.sigmoid(gate_b) * (rec @ w_branch_b[l]))
        x = x + merged @ w_out[l]

        h = rmsnorm(x, ffn_norm_g[l])
        u = causal_dwconv(h @ w_up[l], conv_w[l], conv_b[l])
        g, up = jnp.split(u, 2, axis=-1)
        x = x + (jax.nn.silu(g) * up) @ w_down[l]

    return rmsnorm(x, final_norm_g)
```

```python
import functools

import jax
import jax.numpy as jnp
from jax import lax
from jax.experimental import pallas as pl
from jax.experimental.pallas import tpu as pltpu

F32 = jnp.float32
BF16 = jnp.bfloat16

D_MODEL = 2048
DEPTH = 4
HEADS = 16
Q_LORA = 512
KV_LORA = 512
NOPE = 128
ROPE = 64
VDIM = 128
ROPE_THETA = 10000.0
MASK_VALUE = -1e30
HG_D = 128
SUB = 16
MIN_FORGET = 1e-30
D_FF = 5632
EPS = 1e-6
LANES = 128
QK_PAD = 256
HALO = 16

VMEM_LIMIT = 52 * 1024 * 1024


def _cparams(sem):
    return pltpu.CompilerParams(dimension_semantics=sem, vmem_limit_bytes=VMEM_LIMIT)


def _rms(x, g):
    return x * lax.rsqrt(jnp.mean(x * x, axis=-1, keepdims=True) + EPS) * g


def _dot(a, b):
    return jnp.dot(a, b, preferred_element_type=F32)


def _rope_kernel(pos_ref, invf_ref, sgn_ref, cos_ref, sin_ref):
    ang = pos_ref[...].astype(F32) * invf_ref[...]
    sgn = sgn_ref[...]
    cos_ref[...] = jnp.cos(ang) * jnp.abs(sgn)
    sin_ref[...] = jnp.sin(ang) * sgn


def rope_tables(pos_col, invf, sgn, tm=1024):
    S = pos_col.shape[0]
    return pl.pallas_call(
        _rope_kernel,
        grid=(S // tm,),
        in_specs=[pl.BlockSpec((tm, 1), lambda i: (i, 0)),
                  pl.BlockSpec((1, LANES), lambda i: (0, 0)),
                  pl.BlockSpec((1, LANES), lambda i: (0, 0))],
        out_specs=[pl.BlockSpec((tm, LANES), lambda i: (i, 0))] * 2,
        out_shape=[jax.ShapeDtypeStruct((S, LANES), F32)] * 2,
        compiler_params=_cparams(("arbitrary",)),
        name="rope_tables",
    )(pos_col, invf, sgn)


def _in1_kernel(x_ref, g_ref, w_ref, qg_ref, kvg_ref, cos_ref, sin_ref, cq_ref, ckv_ref, kpe_ref):
    h = _rms(x_ref[...], g_ref[...]).astype(BF16)
    y = _dot(h, w_ref[...])
    cq_ref[...] = _rms(y[:, :Q_LORA], qg_ref[...]).astype(BF16)
    ckv_ref[...] = _rms(y[:, Q_LORA:Q_LORA + KV_LORA], kvg_ref[...]).astype(BF16)
    o = Q_LORA + KV_LORA
    kpe = y[:, o:o + LANES] * cos_ref[...] + y[:, o + LANES:o + 2 * LANES] * sin_ref[...]
    kpe_ref[...] = kpe.astype(BF16)


def in1(x, g, w1, qg, kvg, cosT, sinT, tm=512):
    S = x.shape[0]
    n1 = w1.shape[1]
    row = lambda i: (i, 0)
    fix = lambda i: (0, 0)
    return pl.pallas_call(
        _in1_kernel,
        grid=(S // tm,),
        in_specs=[pl.BlockSpec((tm, D_MODEL), row), pl.BlockSpec((1, D_MODEL), fix),
                  pl.BlockSpec((D_MODEL, n1), fix), pl.BlockSpec((1, Q_LORA), fix),
                  pl.BlockSpec((1, KV_LORA), fix), pl.BlockSpec((tm, LANES), row),
                  pl.BlockSpec((tm, LANES), row)],
        out_specs=[pl.BlockSpec((tm, Q_LORA), row), pl.BlockSpec((tm, KV_LORA), row),
                   pl.BlockSpec((tm, LANES), row)],
        out_shape=[jax.ShapeDtypeStruct((S, Q_LORA), BF16), jax.ShapeDtypeStruct((S, KV_LORA), BF16),
                   jax.ShapeDtypeStruct((S, LANES), BF16)],
        compiler_params=_cparams(("arbitrary",)),
        name="in1",
    )(x, g, w1, qg, kvg, cosT, sinT)


def _in2_kernel(layer, blocks_per_seg, x_ref, g_ref, w_ref, lbl_ref, o_ref, h_ref):
    j = pl.program_id(1)

    @pl.when(j == 0)
    def _():
        h_ref[...] = _rms(x_ref[...], g_ref[...]).astype(BF16)

    y = _dot(h_ref[...], w_ref[...])
    seg = j // blocks_per_seg

    @pl.when((seg == 0) | (seg == 3))
    def _():
        o_ref[...] = y * jax.nn.sigmoid(y)

    @pl.when(seg == 1)
    def _():
        lg = lbl_ref[...]
        e = jnp.exp(lg - jnp.max(lg, axis=0, keepdims=True))
        p = e / jnp.sum(e, axis=0, keepdims=True)
        lb = jnp.sum(p[:layer + 1], axis=0, keepdims=True) - p[0:1]
        o_ref[...] = lb + (1.0 - lb) * jax.nn.sigmoid(y)

    @pl.when(seg == 2)
    def _():
        o_ref[...] = y

    @pl.when(seg >= 4)
    def _():
        o_ref[...] = jax.nn.sigmoid(y)


def in2(layer, x, g, w2, lb_logits, tm=1024, tn=1024):
    S = x.shape[0]
    n2 = w2.shape[1]
    bps = D_MODEL // tn
    return pl.pallas_call(
        functools.partial(_in2_kernel, layer, bps),
        grid=(S // tm, n2 // tn),
        in_specs=[pl.BlockSpec((tm, D_MODEL), lambda i, j: (i, 0)),
                  pl.BlockSpec((1, D_MODEL), lambda i, j: (0, 0)),
                  pl.BlockSpec((D_MODEL, tn), lambda i, j: (0, j)),
                  pl.BlockSpec((DEPTH, tn), lambda i, j: (0, j % bps))],
        out_specs=pl.BlockSpec((tm, tn), lambda i, j: (i, j)),
        out_shape=jax.ShapeDtypeStruct((S, n2), F32),
        scratch_shapes=[pltpu.VMEM((tm, D_MODEL), BF16)],
        compiler_params=_cparams(("arbitrary", "arbitrary")),
        name="in2",
    )(x, g, w2, lb_logits)


def _qkv_kernel(scale, cq_ref, ckv_ref, kpe_ref, cos_ref, sin_ref, wq_ref, wkv_ref, q_ref, k_ref, v_ref):
    cq = cq_ref[...]
    ckv = ckv_ref[...]
    cos = cos_ref[...]
    sin = sin_ref[...]
    kpe = kpe_ref[...]
    for h in range(HEADS):
        y = _dot(cq, wq_ref[h])
        pe = y[:, NOPE:NOPE + LANES] * cos + y[:, NOPE + LANES:] * sin
        q_ref[h, :, :NOPE] = (y[:, :NOPE] * scale).astype(BF16)
        q_ref[h, :, NOPE:] = (pe * scale).astype(BF16)
        z = _dot(ckv, wkv_ref[h])
        k_ref[h, :, :NOPE] = z[:, :NOPE].astype(BF16)
        k_ref[h, :, NOPE:] = kpe
        v_ref[h] = z[:, NOPE:].astype(BF16)


def qkv(cq, ckv, kpe, cosT, sinT, wq, wkv, tm=512):
    S = cq.shape[0]
    scale = float((NOPE + ROPE) ** -0.5)
    row = lambda i: (i, 0)
    return pl.pallas_call(
        functools.partial(_qkv_kernel, scale),
        grid=(S // tm,),
        in_specs=[pl.BlockSpec((tm, Q_LORA), row), pl.BlockSpec((tm, KV_LORA), row),
                  pl.BlockSpec((tm, LANES), row), pl.BlockSpec((tm, LANES), row),
                  pl.BlockSpec((tm, LANES), row),
                  pl.BlockSpec(wq.shape, lambda i: (0, 0, 0)),
                  pl.BlockSpec(wkv.shape, lambda i: (0, 0, 0))],
        out_specs=[pl.BlockSpec((HEADS, tm, QK_PAD), lambda i: (0, i, 0)),
                   pl.BlockSpec((HEADS, tm, QK_PAD), lambda i: (0, i, 0)),
                   pl.BlockSpec((HEADS, tm, VDIM), lambda i: (0, i, 0))],
        out_shape=[jax.ShapeDtypeStruct((HEADS, S, QK_PAD), BF16),
                   jax.ShapeDtypeStruct((HEADS, S, QK_PAD), BF16),
                   jax.ShapeDtypeStruct((HEADS, S, VDIM), BF16)],
        compiler_params=_cparams(("arbitrary",)),
        name="qkv",
    )(cq, ckv, kpe, cosT, sinT, wq, wkv)


def _attn_kernel(t, q_ref, k_ref, v_ref, o_ref, m_ref, l_ref, acc_ref):
    i = pl.program_id(1)
    q = q_ref[0]
    m_ref[...] = jnp.full(m_ref.shape, -jnp.inf, F32)
    l_ref[...] = jnp.zeros(l_ref.shape, F32)
    acc_ref[...] = jnp.zeros(acc_ref.shape, F32)

    def step(j, masked):
        start = pl.multiple_of(j * t, t)
        k = k_ref[0, pl.ds(start, t), :]
        v = v_ref[0, pl.ds(start, t), :]
        s = lax.dot_general(q, k, (((1,), (1,)), ((), ())), preferred_element_type=F32)
        if masked:
            row = lax.broadcasted_iota(jnp.int32, (t, t), 0)
            col = lax.broadcasted_iota(jnp.int32, (t, t), 1)
            s = jnp.where(col <= row, s, MASK_VALUE)
        m_prev = m_ref[...]
        m_new = jnp.maximum(m_prev, jnp.max(s, axis=-1, keepdims=True))
        alpha = jnp.exp(m_prev - m_new)
        p = jnp.exp(s - m_new)
        l_ref[...] = alpha * l_ref[...] + jnp.sum(p, axis=-1, keepdims=True)
        acc_ref[...] = alpha * acc_ref[...] + _dot(p.astype(BF16), v)
        m_ref[...] = m_new

    def body(j, c):
        step(j, False)
        return c

    lax.fori_loop(0, i, body, 0)
    step(i, True)
    o_ref[...] = (acc_ref[...] / l_ref[...]).astype(BF16)


def attention(q, k, v, t=512):
    S = q.shape[1]
    return pl.pallas_call(
        functools.partial(_attn_kernel, t),
        grid=(HEADS, S // t),
        in_specs=[pl.BlockSpec((1, t, QK_PAD), lambda h, i: (h, i, 0)),
                  pl.BlockSpec((1, S, QK_PAD), lambda h, i: (h, 0, 0)),
                  pl.BlockSpec((1, S, VDIM), lambda h, i: (h, 0, 0))],
        out_specs=pl.BlockSpec((t, VDIM), lambda h, i: (i, h)),
        out_shape=jax.ShapeDtypeStruct((S, HEADS * VDIM), BF16),
        scratch_shapes=[pltpu.VMEM((t, 1), F32), pltpu.VMEM((t, 1), F32), pltpu.VMEM((t, VDIM), F32)],
        compiler_params=_cparams(("arbitrary", "arbitrary")),
        name="attention",
    )(q, k, v)


def _hgrn_kernel(nblk, q_ref, f_ref, v_ref, gate_ref, gn_ref, o_ref, st_ref):
    @pl.when(pl.program_id(1) == 0)
    def _():
        st_ref[...] = jnp.zeros(st_ref.shape, F32)

    B = LANES
    r = lax.broadcasted_iota(jnp.int32, (B, B), 0)
    c = lax.broadcasted_iota(jnp.int32, (B, B), 1)
    same = (r // SUB) == (c // SUB)
    incl = jnp.where(same & (c <= r), 1.0, 0.0).astype(F32)
    after = jnp.where(same & (c > r), 1.0, 0.0).astype(F32)
    t_idx = lax.broadcasted_iota(jnp.int32, (SUB, 1), 0)
    gn = gn_ref[...]

    def blk_body(bi, carry):
        base = pl.multiple_of(bi * B, B)
        rows = pl.ds(base, B)
        f = f_ref[rows, :]
        q = q_ref[rows, :]
        v = v_ref[rows, :]
        g = jnp.log(jnp.maximum(f, MIN_FORGET))
        b = jnp.dot(incl, g, precision=lax.Precision.HIGHEST, preferred_element_type=F32)
        suf = jnp.dot(after, g, precision=lax.Precision.HIGHEST, preferred_element_type=F32)
        eb = jnp.exp(b)
        kin = 1.0 - f
        qe = (q * eb).astype(BF16)
        kd = (kin * jnp.exp(suf)).astype(BF16)
        vb = v.astype(BF16)
        st = st_ref[...]
        outs = []
        for u in range(B // SUB):
            lo = u * SUB
            q16 = q[lo:lo + SUB]
            b16 = b[lo:lo + SUB]
            od = jnp.zeros((SUB, HG_D), F32)
            for s in range(SUB):
                rr = lo + s
                e = jnp.exp(jnp.where(t_idx >= s, b16 - b[rr:rr + 1], MASK_VALUE))
                a = jnp.sum(q16 * kin[rr:rr + 1] * e, axis=-1, keepdims=True)
                od = od + a * v[rr:rr + 1]
            o_int = lax.dot_general(qe[lo:lo + SUB], st.astype(BF16), (((1,), (1,)), ((), ())),
                                    preferred_element_type=F32)
            outs.append(od + o_int)
            upd = lax.dot_general(vb[lo:lo + SUB], kd[lo:lo + SUB], (((0,), (0,)), ((), ())),
                                  preferred_element_type=F32)
            st = st * eb[lo + SUB - 1:lo + SUB] + upd
        st_ref[...] = st
        o = jnp.concatenate(outs, axis=0)
        o = _rms(o, gn) * gate_ref[rows, :]
        o_ref[rows, :] = o.astype(BF16)
        return carry

    lax.fori_loop(0, nblk, blk_body, 0)


def hgrn(p2, gn, ts=1024):
    S = p2.shape[0]
    hb = D_MODEL // HG_D
    return pl.pallas_call(
        functools.partial(_hgrn_kernel, ts // LANES),
        grid=(HEADS, S // ts),
        in_specs=[pl.BlockSpec((ts, HG_D), lambda h, i: (i, h)),
                  pl.BlockSpec((ts, HG_D), lambda h, i: (i, hb + h)),
                  pl.BlockSpec((ts, HG_D), lambda h, i: (i, 2 * hb + h)),
                  pl.BlockSpec((ts, HG_D), lambda h, i: (i, 3 * hb + h)),
                  pl.BlockSpec((1, HG_D), lambda h, i: (0, 0))],
        out_specs=pl.BlockSpec((ts, HG_D), lambda h, i: (i, h)),
        out_shape=jax.ShapeDtypeStruct((S, D_MODEL), BF16),
        scratch_shapes=[pltpu.VMEM((HG_D, HG_D), F32)],
        compiler_params=_cparams(("arbitrary", "arbitrary")),
        name="hgrn",
    )(p2, p2, p2, p2, gn)


def _merge_kernel(nj, a_ref, r_ref, ga_ref, gb_ref, wa_ref, wb_ref, wo_ref, x_ref, o_ref, acc_ref):
    j = pl.program_id(1)
    ma = _dot(a_ref[...], wa_ref[...])
    mb = _dot(r_ref[...], wb_ref[...])
    mg = (ga_ref[...] * ma + gb_ref[...] * mb).astype(BF16)
    c = _dot(mg, wo_ref[...])

    @pl.when(j == 0)
    def _():
        acc_ref[...] = c

    @pl.when(j > 0)
    def _():
        acc_ref[...] += c

    @pl.when(j == nj - 1)
    def _():
        o_ref[...] = x_ref[...] + acc_ref[...]


def merge(attn, rec, p2, wa, wb, wo, x, tm=512, tn=256):
    S = x.shape[0]
    nj = D_MODEL // tn
    ga0 = 4 * D_MODEL // tn
    gb0 = 5 * D_MODEL // tn
    return pl.pallas_call(
        functools.partial(_merge_kernel, nj),
        grid=(S // tm, nj),
        in_specs=[pl.BlockSpec((tm, D_MODEL), lambda i, j: (i, 0)),
                  pl.BlockSpec((tm, D_MODEL), lambda i, j: (i, 0)),
                  pl.BlockSpec((tm, tn), lambda i, j: (i, ga0 + j)),
                  pl.BlockSpec((tm, tn), lambda i, j: (i, gb0 + j)),
                  pl.BlockSpec((D_MODEL, tn), lambda i, j: (0, j)),
                  pl.BlockSpec((D_MODEL, tn), lambda i, j: (0, j)),
                  pl.BlockSpec((tn, D_MODEL), lambda i, j: (j, 0)),
                  pl.BlockSpec((tm, D_MODEL), lambda i, j: (i, 0))],
        out_specs=pl.BlockSpec((tm, D_MODEL), lambda i, j: (i, 0)),
        out_shape=jax.ShapeDtypeStruct((S, D_MODEL), F32),
        scratch_shapes=[pltpu.VMEM((tm, D_MODEL), F32)],
        compiler_params=_cparams(("arbitrary", "arbitrary")),
        name="merge",
    )(attn, rec, p2, p2, wa, wb, wo, x)


def _ffn_kernel(final, nj, tm, x_ref, xh_ref, g_ref, wg_ref, wu_ref, cwg_ref, cwu_ref, cbg_ref, cbu_ref,
                wd_ref, fg_ref, o_ref, h_ref, ug_ref, uu_ref, acc_ref):
    m = pl.program_id(0)
    j = pl.program_id(1)

    @pl.when(j == 0)
    def _():
        g = g_ref[...]
        h_ref[HALO:, :] = _rms(x_ref[...], g).astype(BF16)
        hh = jnp.where(m > 0, _rms(xh_ref[...], g), 0.0)
        h_ref[:HALO, :] = hh.astype(BF16)

    h = h_ref[...]
    ug_ref[...] = _dot(h, wg_ref[...])
    uu_ref[...] = _dot(h, wu_ref[...])

    def conv(u_ref, cw_ref, cb_ref):
        y = cb_ref[...]
        for tap in range(3):
            y = y + cw_ref[tap:tap + 1, :] * u_ref[pl.ds(HALO - 2 + tap, tm), :]
        return y

    yg = conv(ug_ref, cwg_ref, cbg_ref)
    yu = conv(uu_ref, cwu_ref, cbu_ref)
    act = (yg * jax.nn.sigmoid(yg) * yu).astype(BF16)
    c = _dot(act, wd_ref[...])

    @pl.when(j == 0)
    def _():
        acc_ref[...] = c

    @pl.when(j > 0)
    def _():
        acc_ref[...] += c

    @pl.when(j == nj - 1)
    def _():
        y = x_ref[...] + acc_ref[...]
        if final:
            y = _rms(y, fg_ref[...])
        o_ref[...] = y


def ffn(final, x, g, w_up, conv_w, conv_b, w_down, fg, tm=512, tn=512):
    S = x.shape[0]
    nj = D_FF // tn
    hb = tm // HALO
    return pl.pallas_call(
        functools.partial(_ffn_kernel, final, nj, tm),
        grid=(S // tm, nj),
        in_specs=[pl.BlockSpec((tm, D_MODEL), lambda i, j: (i, 0)),
                  pl.BlockSpec((HALO, D_MODEL), lambda i, j: (jnp.maximum(i * hb - 1, 0), 0)),
                  pl.BlockSpec((1, D_MODEL), lambda i, j: (0, 0)),
                  pl.BlockSpec((D_MODEL, tn), lambda i, j: (0, j)),
                  pl.BlockSpec((D_MODEL, tn), lambda i, j: (0, nj + j)),
                  pl.BlockSpec((3, tn), lambda i, j: (0, j)),
                  pl.BlockSpec((3, tn), lambda i, j: (0, nj + j)),
                  pl.BlockSpec((1, tn), lambda i, j: (0, j)),
                  pl.BlockSpec((1, tn), lambda i, j: (0, nj + j)),
                  pl.BlockSpec((tn, D_MODEL), lambda i, j: (j, 0)),
                  pl.BlockSpec((1, D_MODEL), lambda i, j: (0, 0))],
        out_specs=pl.BlockSpec((tm, D_MODEL), lambda i, j: (i, 0)),
        out_shape=jax.ShapeDtypeStruct((S, D_MODEL), F32),
        scratch_shapes=[pltpu.VMEM((tm + HALO, D_MODEL), BF16),
                        pltpu.VMEM((tm + HALO, tn), F32),
                        pltpu.VMEM((tm + HALO, tn), F32),
                        pltpu.VMEM((tm, D_MODEL), F32)],
        compiler_params=_cparams(("arbitrary", "arbitrary")),
        name="ffn",
    )(x, x, g, w_up, w_up, conv_w, conv_w, conv_b, conv_b, w_down, fg)


def _rope_pair_cols(w):
    x1, x2 = w[..., :ROPE // 2], w[..., ROPE // 2:]
    z = jnp.zeros(w.shape[:-1] + (LANES - ROPE,), w.dtype)
    return jnp.concatenate([x1, x2, z], axis=-1), jnp.concatenate([x2, x1, z], axis=-1)


def kernel(x, positions, attn_norm_g, w_in, q_norm_g, w_uq, kv_norm_g, w_ukv, hgrn_lb_logits, hgrn_out_norm_g,
           w_branch_a, w_branch_b, w_out, ffn_norm_g, w_up, conv_w, conv_b, w_down, final_norm_g):
    B, S, D = x.shape
    assert B == 1 and D == D_MODEL
    xs = x.reshape(S, D)

    inv_freq = ROPE_THETA ** (-jnp.arange(0, ROPE, 2, dtype=F32) / ROPE)
    zeros = jnp.zeros((LANES - ROPE,), F32)
    invf = jnp.concatenate([inv_freq, inv_freq, zeros]).reshape(1, LANES)
    half = jnp.ones((ROPE // 2,), F32)
    sgn = jnp.concatenate([-half, half, zeros]).reshape(1, LANES)
    cosT, sinT = rope_tables(positions.reshape(S, 1), invf, sgn)

    n_lat = Q_LORA + KV_LORA
    lb_logits = hgrn_lb_logits.astype(F32)
    for l in range(DEPTH):
        wl = w_in[l]
        ka, kb = _rope_pair_cols(wl[:, n_lat:n_lat + ROPE])
        w1 = jnp.concatenate([wl[:, :n_lat], ka, kb], axis=-1).astype(BF16)
        w2 = wl[:, n_lat + ROPE:].astype(BF16)
        wq = w_uq[l].reshape(Q_LORA, HEADS, NOPE + ROPE)
        qa, qb = _rope_pair_cols(wq[..., NOPE:])
        wq = jnp.concatenate([wq[..., :NOPE], qa, qb], axis=-1).transpose(1, 0, 2).astype(BF16)
        wkv = w_ukv[l].reshape(KV_LORA, HEADS, NOPE + VDIM).transpose(1, 0, 2).astype(BF16)

        cq, ckv, kpe = in1(xs, attn_norm_g[l].reshape(1, D), w1, q_norm_g[l].reshape(1, Q_LORA),
                           kv_norm_g[l].reshape(1, KV_LORA), cosT, sinT)
        p2 = in2(l, xs, attn_norm_g[l].reshape(1, D), w2, lb_logits)
        q, k, v = qkv(cq, ckv, kpe, cosT, sinT, wq, wkv)
        attn = attention(q, k, v)
        rec = hgrn(p2, hgrn_out_norm_g[l].reshape(1, HG_D))
        xs = merge(attn, rec, p2, w_branch_a[l].astype(BF16), w_branch_b[l].astype(BF16),
                   w_out[l].astype(BF16), xs)
        xs = ffn(l == DEPTH - 1, xs, ffn_norm_g[l].reshape(1, D), w_up[l].astype(BF16), conv_w[l],
                 conv_b[l].reshape(1, 2 * D_FF), w_down[l].astype(BF16), final_norm_g.reshape(1, D))
    return xs.reshape(B, S, D)
```

```python
import functools

import jax
import jax.numpy as jnp
from jax import lax
from jax.experimental import pallas as pl
from jax.experimental.pallas import tpu as pltpu

F32 = jnp.float32
BF16 = jnp.bfloat16

D_MODEL = 2048
DEPTH = 4
HEADS = 16
Q_LORA = 512
KV_LORA = 512
NOPE = 128
ROPE = 64
VDIM = 128
ROPE_THETA = 10000.0
MASK_VALUE = -1e30
HG_D = 128
SUB = 16
MIN_FORGET = 1e-30
D_FF = 5632
EPS = 1e-6
LANES = 128
QK_PAD = 256
HALO = 16

VMEM_LIMIT = 52 * 1024 * 1024


def _cparams(sem):
    return pltpu.CompilerParams(dimension_semantics=sem, vmem_limit_bytes=VMEM_LIMIT)


def _rms(x, g):
    return x * lax.rsqrt(jnp.mean(x * x, axis=-1, keepdims=True) + EPS) * g


def _dot(a, b):
    return jnp.dot(a, b, preferred_element_type=F32)


def _rope_kernel(pos_ref, invf_ref, sgn_ref, cos_ref, sin_ref):
    ang = pos_ref[...].astype(F32) * invf_ref[...]
    sgn = sgn_ref[...]
    cos_ref[...] = jnp.cos(ang) * jnp.abs(sgn)
    sin_ref[...] = jnp.sin(ang) * sgn


def rope_tables(pos_col, invf, sgn, tm=1024):
    S = pos_col.shape[0]
    return pl.pallas_call(
        _rope_kernel,
        grid=(S // tm,),
        in_specs=[pl.BlockSpec((tm, 1), lambda i: (i, 0)),
                  pl.BlockSpec((1, LANES), lambda i: (0, 0)),
                  pl.BlockSpec((1, LANES), lambda i: (0, 0))],
        out_specs=[pl.BlockSpec((tm, LANES), lambda i: (i, 0))] * 2,
        out_shape=[jax.ShapeDtypeStruct((S, LANES), F32)] * 2,
        compiler_params=_cparams(("arbitrary",)),
        name="rope_tables",
    )(pos_col, invf, sgn)


def _in1_kernel(x_ref, g_ref, w_ref, qg_ref, kvg_ref, cos_ref, sin_ref, cq_ref, ckv_ref, kpe_ref):
    h = _rms(x_ref[...], g_ref[...]).astype(BF16)
    y = _dot(h, w_ref[...])
    cq_ref[...] = _rms(y[:, :Q_LORA], qg_ref[...]).astype(BF16)
    ckv_ref[...] = _rms(y[:, Q_LORA:Q_LORA + KV_LORA], kvg_ref[...]).astype(BF16)
    o = Q_LORA + KV_LORA
    kpe = y[:, o:o + LANES] * cos_ref[...] + y[:, o + LANES:o + 2 * LANES] * sin_ref[...]
    kpe_ref[...] = kpe.astype(BF16)


def in1(x, g, w1, qg, kvg, cosT, sinT, tm=512):
    S = x.shape[0]
    n1 = w1.shape[1]
    row = lambda i: (i, 0)
    fix = lambda i: (0, 0)
    return pl.pallas_call(
        _in1_kernel,
        grid=(S // tm,),
        in_specs=[pl.BlockSpec((tm, D_MODEL), row), pl.BlockSpec((1, D_MODEL), fix),
                  pl.BlockSpec((D_MODEL, n1), fix), pl.BlockSpec((1, Q_LORA), fix),
                  pl.BlockSpec((1, KV_LORA), fix), pl.BlockSpec((tm, LANES), row),
                  pl.BlockSpec((tm, LANES), row)],
        out_specs=[pl.BlockSpec((tm, Q_LORA), row), pl.BlockSpec((tm, KV_LORA), row),
                   pl.BlockSpec((tm, LANES), row)],
        out_shape=[jax.ShapeDtypeStruct((S, Q_LORA), BF16), jax.ShapeDtypeStruct((S, KV_LORA), BF16),
                   jax.ShapeDtypeStruct((S, LANES), BF16)],
        compiler_params=_cparams(("arbitrary",)),
        name="in1",
    )(x, g, w1, qg, kvg, cosT, sinT)


def _in2_kernel(layer, blocks_per_seg, x_ref, g_ref, w_ref, lbl_ref, o_ref, h_ref):
    j = pl.program_id(1)

    @pl.when(j == 0)
    def _():
        h_ref[...] = _rms(x_ref[...], g_ref[...]).astype(BF16)

    y = _dot(h_ref[...], w_ref[...])
    seg = j // blocks_per_seg

    @pl.when((seg == 0) | (seg == 3))
    def _():
        o_ref[...] = y * jax.nn.sigmoid(y)

    @pl.when(seg == 1)
    def _():
        lg = lbl_ref[...]
        e = jnp.exp(lg - jnp.max(lg, axis=0, keepdims=True))
        p = e / jnp.sum(e, axis=0, keepdims=True)
        lb = jnp.sum(p[:layer + 1], axis=0, keepdims=True) - p[0:1]
        o_ref[...] = lb + (1.0 - lb) * jax.nn.sigmoid(y)

    @pl.when(seg == 2)
    def _():
        o_ref[...] = y

    @pl.when(seg >= 4)
    def _():
        o_ref[...] = jax.nn.sigmoid(y)


def in2(layer, x, g, w2, lb_logits, tm=1024, tn=1024):
    S = x.shape[0]
    n2 = w2.shape[1]
    bps = D_MODEL // tn
    return pl.pallas_call(
        functools.partial(_in2_kernel, layer, bps),
        grid=(S // tm, n2 // tn),
        in_specs=[pl.BlockSpec((tm, D_MODEL), lambda i, j: (i, 0)),
                  pl.BlockSpec((1, D_MODEL), lambda i, j: (0, 0)),
                  pl.BlockSpec((D_MODEL, tn), lambda i, j: (0, j)),
                  pl.BlockSpec((DEPTH, tn), lambda i, j: (0, j % bps))],
        out_specs=pl.BlockSpec((tm, tn), lambda i, j: (i, j)),
        out_shape=jax.ShapeDtypeStruct((S, n2), F32),
        scratch_shapes=[pltpu.VMEM((tm, D_MODEL), BF16)],
        compiler_params=_cparams(("arbitrary", "arbitrary")),
        name="in2",
    )(x, g, w2, lb_logits)


def _qkv_kernel(scale, cq_ref, ckv_ref, kpe_ref, cos_ref, sin_ref, wq_ref, wkv_ref, q_ref, k_ref, v_ref):
    cq = cq_ref[...]
    ckv = ckv_ref[...]
    cos = cos_ref[...]
    sin = sin_ref[...]
    kpe = kpe_ref[...]
    for h in range(HEADS):
        y = _dot(cq, wq_ref[h])
        pe = y[:, NOPE:NOPE + LANES] * cos + y[:, NOPE + LANES:] * sin
        q_ref[h, :, :NOPE] = (y[:, :NOPE] * scale).astype(BF16)
        q_ref[h, :, NOPE:] = (pe * scale).astype(BF16)
        z = _dot(ckv, wkv_ref[h])
        k_ref[h, :, :NOPE] = z[:, :NOPE].astype(BF16)
        k_ref[h, :, NOPE:] = kpe
        v_ref[h] = z[:, NOPE:].astype(BF16)


def qkv(cq, ckv, kpe, cosT, sinT, wq, wkv, tm=512):
    S = cq.shape[0]
    scale = float((NOPE + ROPE) ** -0.5 * 1.4426950408889634)
    row = lambda i: (i, 0)
    return pl.pallas_call(
        functools.partial(_qkv_kernel, scale),
        grid=(S // tm,),
        in_specs=[pl.BlockSpec((tm, Q_LORA), row), pl.BlockSpec((tm, KV_LORA), row),
                  pl.BlockSpec((tm, LANES), row), pl.BlockSpec((tm, LANES), row),
                  pl.BlockSpec((tm, LANES), row),
                  pl.BlockSpec(wq.shape, lambda i: (0, 0, 0)),
                  pl.BlockSpec(wkv.shape, lambda i: (0, 0, 0))],
        out_specs=[pl.BlockSpec((HEADS, tm, QK_PAD), lambda i: (0, i, 0)),
                   pl.BlockSpec((HEADS, tm, QK_PAD), lambda i: (0, i, 0)),
                   pl.BlockSpec((HEADS, tm, VDIM), lambda i: (0, i, 0))],
        out_shape=[jax.ShapeDtypeStruct((HEADS, S, QK_PAD), BF16),
                   jax.ShapeDtypeStruct((HEADS, S, QK_PAD), BF16),
                   jax.ShapeDtypeStruct((HEADS, S, VDIM), BF16)],
        compiler_params=_cparams(("arbitrary",)),
        name="qkv",
    )(cq, ckv, kpe, cosT, sinT, wq, wkv)


def _attn_kernel(t, hpb, q_ref, k_ref, v_ref, o_ref, m_ref, l_ref, acc_ref):
    i = pl.program_id(1)
    nb = t // LANES
    m_ref[...] = jnp.full(m_ref.shape, -jnp.inf, F32)
    l_ref[...] = jnp.zeros(l_ref.shape, F32)
    acc_ref[...] = jnp.zeros(acc_ref.shape, F32)

    def lane_blocks(x):
        return [x[:, b * LANES:(b + 1) * LANES] for b in range(nb)]

    def step(j, masked):
        start = pl.multiple_of(j * t, t)
        for hh in range(hpb):
            k = k_ref[hh, pl.ds(start, t), :]
            v = v_ref[hh, pl.ds(start, t), :]
            s = lax.dot_general(q_ref[hh], k, (((1,), (1,)), ((), ())), preferred_element_type=F32)
            if masked:
                row = lax.broadcasted_iota(jnp.int32, (t, t), 0)
                col = lax.broadcasted_iota(jnp.int32, (t, t), 1)
                s = jnp.where(col <= row, s, MASK_VALUE)
            m_prev = m_ref[hh]
            m_cur = functools.reduce(jnp.maximum, lane_blocks(s))
            m_new = jnp.maximum(m_prev, jnp.max(m_cur, axis=-1, keepdims=True))
            alpha = jnp.exp2(m_prev - m_new)
            p = jnp.exp2(s - pltpu.repeat(m_new, nb, axis=1))
            l_ref[hh] = alpha * l_ref[hh] + functools.reduce(jnp.add, lane_blocks(p))
            acc_ref[hh] = alpha * acc_ref[hh] + _dot(p.astype(BF16), v)
            m_ref[hh] = m_new

    def body(j, c):
        step(j, False)
        return c

    lax.fori_loop(0, i, body, 0)
    step(i, True)
    for hh in range(hpb):
        l = jnp.sum(l_ref[hh], axis=-1, keepdims=True)
        o_ref[:, hh * VDIM:(hh + 1) * VDIM] = (acc_ref[hh] / l).astype(BF16)


def attention(q, k, v, t=512, hpb=2):
    S = q.shape[1]
    return pl.pallas_call(
        functools.partial(_attn_kernel, t, hpb),
        grid=(HEADS // hpb, S // t),
        in_specs=[pl.BlockSpec((hpb, t, QK_PAD), lambda h, i: (h, i, 0)),
                  pl.BlockSpec((hpb, S, QK_PAD), lambda h, i: (h, 0, 0)),
                  pl.BlockSpec((hpb, S, VDIM), lambda h, i: (h, 0, 0))],
        out_specs=pl.BlockSpec((t, hpb * VDIM), lambda h, i: (i, h)),
        out_shape=jax.ShapeDtypeStruct((S, HEADS * VDIM), BF16),
        scratch_shapes=[pltpu.VMEM((hpb, t, LANES), F32), pltpu.VMEM((hpb, t, LANES), F32),
                        pltpu.VMEM((hpb, t, VDIM), F32)],
        compiler_params=_cparams(("arbitrary", "arbitrary")),
        name="attention",
    )(q, k, v)


def _hgrn_kernel(nblk, q_ref, f_ref, v_ref, gate_ref, gn_ref, o_ref, st_ref):
    @pl.when(pl.program_id(1) == 0)
    def _():
        st_ref[...] = jnp.zeros(st_ref.shape, F32)

    B = LANES
    r = lax.broadcasted_iota(jnp.int32, (B, B), 0)
    c = lax.broadcasted_iota(jnp.int32, (B, B), 1)
    same = (r // SUB) == (c // SUB)
    incl = jnp.where(same & (c <= r), 1.0, 0.0).astype(F32)
    after = jnp.where(same & (c > r), 1.0, 0.0).astype(F32)
    t_idx = lax.broadcasted_iota(jnp.int32, (SUB, 1), 0)
    gn = gn_ref[...]

    def blk_body(bi, carry):
        base = pl.multiple_of(bi * B, B)
        rows = pl.ds(base, B)
        f = f_ref[rows, :]
        q = q_ref[rows, :]
        v = v_ref[rows, :]
        g = jnp.log(jnp.maximum(f, MIN_FORGET))
        b = jnp.dot(incl, g, precision=lax.Precision.HIGHEST, preferred_element_type=F32)
        suf = jnp.dot(after, g, precision=lax.Precision.HIGHEST, preferred_element_type=F32)
        eb = jnp.exp(b)
        kin = 1.0 - f
        qe = (q * eb).astype(BF16)
        kd = (kin * jnp.exp(suf)).astype(BF16)
        vb = v.astype(BF16)
        st = st_ref[...]
        outs = []
        for u in range(B // SUB):
            lo = u * SUB
            q16 = q[lo:lo + SUB]
            b16 = b[lo:lo + SUB]
            od = jnp.zeros((SUB, HG_D), F32)
            for s in range(SUB):
                rr = lo + s
                e = jnp.exp(jnp.where(t_idx >= s, b16 - b[rr:rr + 1], MASK_VALUE))
                a = jnp.sum(q16 * kin[rr:rr + 1] * e, axis=-1, keepdims=True)
                od = od + a * v[rr:rr + 1]
            o_int = lax.dot_general(qe[lo:lo + SUB], st.astype(BF16), (((1,), (1,)), ((), ())),
                                    preferred_element_type=F32)
            outs.append(od + o_int)
            upd = lax.dot_general(vb[lo:lo + SUB], kd[lo:lo + SUB], (((0,), (0,)), ((), ())),
                                  preferred_element_type=F32)
            st = st * eb[lo + SUB - 1:lo + SUB] + upd
        st_ref[...] = st
        o = jnp.concatenate(outs, axis=0)
        o = _rms(o, gn) * gate_ref[rows, :]
        o_ref[rows, :] = o.astype(BF16)
        return carry

    lax.fori_loop(0, nblk, blk_body, 0)


def hgrn(p2, gn, ts=1024):
    S = p2.shape[0]
    hb = D_MODEL // HG_D
    return pl.pallas_call(
        functools.partial(_hgrn_kernel, ts // LANES),
        grid=(HEADS, S // ts),
        in_specs=[pl.BlockSpec((ts, HG_D), lambda h, i: (i, h)),
                  pl.BlockSpec((ts, HG_D), lambda h, i: (i, hb + h)),
                  pl.BlockSpec((ts, HG_D), lambda h, i: (i, 2 * hb + h)),
                  pl.BlockSpec((ts, HG_D), lambda h, i: (i, 3 * hb + h)),
                  pl.BlockSpec((1, HG_D), lambda h, i: (0, 0))],
        out_specs=pl.BlockSpec((ts, HG_D), lambda h, i: (i, h)),
        out_shape=jax.ShapeDtypeStruct((S, D_MODEL), BF16),
        scratch_shapes=[pltpu.VMEM((HG_D, HG_D), F32)],
        compiler_params=_cparams(("arbitrary", "arbitrary")),
        name="hgrn",
    )(p2, p2, p2, p2, gn)


def _merge_kernel(nj, a_ref, r_ref, ga_ref, gb_ref, wa_ref, wb_ref, wo_ref, x_ref, o_ref, acc_ref):
    j = pl.program_id(1)
    ma = _dot(a_ref[...], wa_ref[...])
    mb = _dot(r_ref[...], wb_ref[...])
    mg = (ga_ref[...] * ma + gb_ref[...] * mb).astype(BF16)
    c = _dot(mg, wo_ref[...])

    @pl.when(j == 0)
    def _():
        acc_ref[...] = c

    @pl.when(j > 0)
    def _():
        acc_ref[...] += c

    @pl.when(j == nj - 1)
    def _():
        o_ref[...] = x_ref[...] + acc_ref[...]


def merge(attn, rec, p2, wa, wb, wo, x, tm=512, tn=256):
    S = x.shape[0]
    nj = D_MODEL // tn
    ga0 = 4 * D_MODEL // tn
    gb0 = 5 * D_MODEL // tn
    return pl.pallas_call(
        functools.partial(_merge_kernel, nj),
        grid=(S // tm, nj),
        in_specs=[pl.BlockSpec((tm, D_MODEL), lambda i, j: (i, 0)),
                  pl.BlockSpec((tm, D_MODEL), lambda i, j: (i, 0)),
                  pl.BlockSpec((tm, tn), lambda i, j: (i, ga0 + j)),
                  pl.BlockSpec((tm, tn), lambda i, j: (i, gb0 + j)),
                  pl.BlockSpec((D_MODEL, tn), lambda i, j: (0, j)),
                  pl.BlockSpec((D_MODEL, tn), lambda i, j: (0, j)),
                  pl.BlockSpec((tn, D_MODEL), lambda i, j: (j, 0)),
                  pl.BlockSpec((tm, D_MODEL), lambda i, j: (i, 0))],
        out_specs=pl.BlockSpec((tm, D_MODEL), lambda i, j: (i, 0)),
        out_shape=jax.ShapeDtypeStruct((S, D_MODEL), F32),
        scratch_shapes=[pltpu.VMEM((tm, D_MODEL), F32)],
        compiler_params=_cparams(("arbitrary", "arbitrary")),
        name="merge",
    )(attn, rec, p2, p2, wa, wb, wo, x)


def _ffn_kernel(final, nj, tm, x_ref, xh_ref, g_ref, wg_ref, wu_ref, cwg_ref, cwu_ref, cbg_ref, cbu_ref,
                wd_ref, fg_ref, o_ref, h_ref, ug_ref, uu_ref, acc_ref):
    m = pl.program_id(0)
    j = pl.program_id(1)

    @pl.when(j == 0)
    def _():
        g = g_ref[...]
        h_ref[HALO:, :] = _rms(x_ref[...], g).astype(BF16)
        hh = jnp.where(m > 0, _rms(xh_ref[...], g), 0.0)
        h_ref[:HALO, :] = hh.astype(BF16)

    h = h_ref[...]
    ug_ref[...] = _dot(h, wg_ref[...])
    uu_ref[...] = _dot(h, wu_ref[...])

    def conv(u_ref, cw_ref, cb_ref):
        y = cb_ref[...]
        for tap in range(3):
            y = y + cw_ref[tap:tap + 1, :] * u_ref[pl.ds(HALO - 2 + tap, tm), :]
        return y

    yg = conv(ug_ref, cwg_ref, cbg_ref)
    yu = conv(uu_ref, cwu_ref, cbu_ref)
    act = (yg * jax.nn.sigmoid(yg) * yu).astype(BF16)
    c = _dot(act, wd_ref[...])

    @pl.when(j == 0)
    def _():
        acc_ref[...] = c

    @pl.when(j > 0)
    def _():
        acc_ref[...] += c

    @pl.when(j == nj - 1)
    def _():
        y = x_ref[...] + acc_ref[...]
        if final:
            y = _rms(y, fg_ref[...])
        o_ref[...] = y


def ffn(final, x, g, w_up, conv_w, conv_b, w_down, fg, tm=512, tn=512):
    S = x.shape[0]
    nj = D_FF // tn
    hb = tm // HALO
    return pl.pallas_call(
        functools.partial(_ffn_kernel, final, nj, tm),
        grid=(S // tm, nj),
        in_specs=[pl.BlockSpec((tm, D_MODEL), lambda i, j: (i, 0)),
                  pl.BlockSpec((HALO, D_MODEL), lambda i, j: (jnp.maximum(i * hb - 1, 0), 0)),
                  pl.BlockSpec((1, D_MODEL), lambda i, j: (0, 0)),
                  pl.BlockSpec((D_MODEL, tn), lambda i, j: (0, j)),
                  pl.BlockSpec((D_MODEL, tn), lambda i, j: (0, nj + j)),
                  pl.BlockSpec((3, tn), lambda i, j: (0, j)),
                  pl.BlockSpec((3, tn), lambda i, j: (0, nj + j)),
                  pl.BlockSpec((1, tn), lambda i, j: (0, j)),
                  pl.BlockSpec((1, tn), lambda i, j: (0, nj + j)),
                  pl.BlockSpec((tn, D_MODEL), lambda i, j: (j, 0)),
                  pl.BlockSpec((1, D_MODEL), lambda i, j: (0, 0))],
        out_specs=pl.BlockSpec((tm, D_MODEL), lambda i, j: (i, 0)),
        out_shape=jax.ShapeDtypeStruct((S, D_MODEL), F32),
        scratch_shapes=[pltpu.VMEM((tm + HALO, D_MODEL), BF16),
                        pltpu.VMEM((tm + HALO, tn), F32),
                        pltpu.VMEM((tm + HALO, tn), F32),
                        pltpu.VMEM((tm, D_MODEL), F32)],
        compiler_params=_cparams(("arbitrary", "arbitrary")),
        name="ffn",
    )(x, x, g, w_up, w_up, conv_w, conv_w, conv_b, conv_b, w_down, fg)


def _rope_pair_cols(w):
    x1, x2 = w[..., :ROPE // 2], w[..., ROPE // 2:]
    z = jnp.zeros(w.shape[:-1] + (LANES - ROPE,), w.dtype)
    return jnp.concatenate([x1, x2, z], axis=-1), jnp.concatenate([x2, x1, z], axis=-1)


def kernel(x, positions, attn_norm_g, w_in, q_norm_g, w_uq, kv_norm_g, w_ukv, hgrn_lb_logits, hgrn_out_norm_g,
           w_branch_a, w_branch_b, w_out, ffn_norm_g, w_up, conv_w, conv_b, w_down, final_norm_g):
    B, S, D = x.shape
    assert B == 1 and D == D_MODEL
    xs = x.reshape(S, D)

    inv_freq = ROPE_THETA ** (-jnp.arange(0, ROPE, 2, dtype=F32) / ROPE)
    zeros = jnp.zeros((LANES - ROPE,), F32)
    invf = jnp.concatenate([inv_freq, inv_freq, zeros]).reshape(1, LANES)
    half = jnp.ones((ROPE // 2,), F32)
    sgn = jnp.concatenate([-half, half, zeros]).reshape(1, LANES)
    cosT, sinT = rope_tables(positions.reshape(S, 1), invf, sgn)

    n_lat = Q_LORA + KV_LORA
    lb_logits = hgrn_lb_logits.astype(F32)
    for l in range(DEPTH):
        wl = w_in[l]
        ka, kb = _rope_pair_cols(wl[:, n_lat:n_lat + ROPE])
        w1 = jnp.concatenate([wl[:, :n_lat], ka, kb], axis=-1).astype(BF16)
        w2 = wl[:, n_lat + ROPE:].astype(BF16)
        wq = w_uq[l].reshape(Q_LORA, HEADS, NOPE + ROPE)
        qa, qb = _rope_pair_cols(wq[..., NOPE:])
        wq = jnp.concatenate([wq[..., :NOPE], qa, qb], axis=-1).transpose(1, 0, 2).astype(BF16)
        wkv = w_ukv[l].reshape(KV_LORA, HEADS, NOPE + VDIM).transpose(1, 0, 2).astype(BF16)

        cq, ckv, kpe = in1(xs, attn_norm_g[l].reshape(1, D), w1, q_norm_g[l].reshape(1, Q_LORA),
                           kv_norm_g[l].reshape(1, KV_LORA), cosT, sinT)
        p2 = in2(l, xs, attn_norm_g[l].reshape(1, D), w2, lb_logits)
        q, k, v = qkv(cq, ckv, kpe, cosT, sinT, wq, wkv)
        attn = attention(q, k, v)
        rec = hgrn(p2, hgrn_out_norm_g[l].reshape(1, HG_D))
        xs = merge(attn, rec, p2, w_branch_a[l].astype(BF16), w_branch_b[l].astype(BF16),
                   w_out[l].astype(BF16), xs)
        xs = ffn(l == DEPTH - 1, xs, ffn_norm_g[l].reshape(1, D), w_up[l].astype(BF16), conv_w[l],
                 conv_b[l].reshape(1, 2 * D_FF), w_down[l].astype(BF16), final_norm_g.reshape(1, D))
    return xs.reshape(B, S, D)
```

```python
import functools

import jax
import jax.numpy as jnp
from jax import lax
from jax.experimental import pallas as pl
from jax.experimental.pallas import tpu as pltpu

F32 = jnp.float32
BF16 = jnp.bfloat16

D_MODEL = 2048
DEPTH = 4
HEADS = 16
Q_LORA = 512
KV_LORA = 512
NOPE = 128
ROPE = 64
VDIM = 128
ROPE_THETA = 10000.0
MASK_VALUE = -1e30
HG_D = 128
SUB = 16
MIN_FORGET = 1e-30
D_FF = 5632
EPS = 1e-6
LOG2E = 1.4426950408889634
LANES = 128
QK_PAD = 256
HALO = 16

VMEM_LIMIT = 52 * 1024 * 1024


def _cparams(sem):
    return pltpu.CompilerParams(dimension_semantics=sem, vmem_limit_bytes=VMEM_LIMIT)


def _rms(x, g):
    return x * lax.rsqrt(jnp.mean(x * x, axis=-1, keepdims=True) + EPS) * g


def _dot(a, b):
    return jnp.dot(a, b, preferred_element_type=F32)


def _rope_kernel(pos_ref, invf_ref, sgn_ref, cos_ref, sin_ref):
    ang = pos_ref[...].astype(F32) * invf_ref[...]
    sgn = sgn_ref[...]
    cos_ref[...] = jnp.cos(ang) * jnp.abs(sgn)
    sin_ref[...] = jnp.sin(ang) * sgn


def rope_tables(pos_col, invf, sgn, tm=1024):
    S = pos_col.shape[0]
    return pl.pallas_call(
        _rope_kernel,
        grid=(S // tm,),
        in_specs=[pl.BlockSpec((tm, 1), lambda i: (i, 0)),
                  pl.BlockSpec((1, LANES), lambda i: (0, 0)),
                  pl.BlockSpec((1, LANES), lambda i: (0, 0))],
        out_specs=[pl.BlockSpec((tm, LANES), lambda i: (i, 0))] * 2,
        out_shape=[jax.ShapeDtypeStruct((S, LANES), F32)] * 2,
        compiler_params=_cparams(("arbitrary",)),
        name="rope_tables",
    )(pos_col, invf, sgn)


def _in1_kernel(x_ref, g_ref, w_ref, qg_ref, kvg_ref, cos_ref, sin_ref, cq_ref, ckv_ref, kpe_ref):
    h = _rms(x_ref[...], g_ref[...]).astype(BF16)
    y = _dot(h, w_ref[...])
    cq_ref[...] = _rms(y[:, :Q_LORA], qg_ref[...]).astype(BF16)
    ckv_ref[...] = _rms(y[:, Q_LORA:Q_LORA + KV_LORA], kvg_ref[...]).astype(BF16)
    o = Q_LORA + KV_LORA
    kpe = y[:, o:o + LANES] * cos_ref[...] + y[:, o + LANES:o + 2 * LANES] * sin_ref[...]
    kpe_ref[...] = kpe.astype(BF16)


def in1(x, g, w1, qg, kvg, cosT, sinT, tm=512):
    S = x.shape[0]
    n1 = w1.shape[1]
    row = lambda i: (i, 0)
    fix = lambda i: (0, 0)
    return pl.pallas_call(
        _in1_kernel,
        grid=(S // tm,),
        in_specs=[pl.BlockSpec((tm, D_MODEL), row), pl.BlockSpec((1, D_MODEL), fix),
                  pl.BlockSpec((D_MODEL, n1), fix), pl.BlockSpec((1, Q_LORA), fix),
                  pl.BlockSpec((1, KV_LORA), fix), pl.BlockSpec((tm, LANES), row),
                  pl.BlockSpec((tm, LANES), row)],
        out_specs=[pl.BlockSpec((tm, Q_LORA), row), pl.BlockSpec((tm, KV_LORA), row),
                   pl.BlockSpec((tm, LANES), row)],
        out_shape=[jax.ShapeDtypeStruct((S, Q_LORA), BF16), jax.ShapeDtypeStruct((S, KV_LORA), BF16),
                   jax.ShapeDtypeStruct((S, LANES), BF16)],
        compiler_params=_cparams(("arbitrary",)),
        name="in1",
    )(x, g, w1, qg, kvg, cosT, sinT)


def _in2_kernel(layer, blocks_per_seg, x_ref, g_ref, w_ref, lbl_ref, o_ref, h_ref):
    j = pl.program_id(1)

    @pl.when(j == 0)
    def _():
        h_ref[...] = _rms(x_ref[...], g_ref[...]).astype(BF16)

    y = _dot(h_ref[...], w_ref[...])
    seg = j // blocks_per_seg

    @pl.when((seg == 0) | (seg == 3))
    def _():
        o_ref[...] = y * jax.nn.sigmoid(y)

    @pl.when(seg == 1)
    def _():
        lg = lbl_ref[...]
        e = jnp.exp(lg - jnp.max(lg, axis=0, keepdims=True))
        p = e / jnp.sum(e, axis=0, keepdims=True)
        lb = jnp.sum(p[:layer + 1], axis=0, keepdims=True) - p[0:1]
        o_ref[...] = lb + (1.0 - lb) * jax.nn.sigmoid(y)

    @pl.when(seg == 2)
    def _():
        o_ref[...] = y

    @pl.when(seg >= 4)
    def _():
        o_ref[...] = jax.nn.sigmoid(y)


def in2(layer, x, g, w2, lb_logits, tm=1024, tn=1024):
    S = x.shape[0]
    n2 = w2.shape[1]
    bps = D_MODEL // tn
    return pl.pallas_call(
        functools.partial(_in2_kernel, layer, bps),
        grid=(S // tm, n2 // tn),
        in_specs=[pl.BlockSpec((tm, D_MODEL), lambda i, j: (i, 0)),
                  pl.BlockSpec((1, D_MODEL), lambda i, j: (0, 0)),
                  pl.BlockSpec((D_MODEL, tn), lambda i, j: (0, j)),
                  pl.BlockSpec((DEPTH, tn), lambda i, j: (0, j % bps))],
        out_specs=pl.BlockSpec((tm, tn), lambda i, j: (i, j)),
        out_shape=jax.ShapeDtypeStruct((S, n2), F32),
        scratch_shapes=[pltpu.VMEM((tm, D_MODEL), BF16)],
        compiler_params=_cparams(("arbitrary", "arbitrary")),
        name="in2",
    )(x, g, w2, lb_logits)


def _qkv_kernel(scale, cq_ref, ckv_ref, kpe_ref, cos_ref, sin_ref, wq_ref, wkv_ref, q_ref, k_ref, v_ref):
    cq = cq_ref[...]
    ckv = ckv_ref[...]
    cos = cos_ref[...]
    sin = sin_ref[...]
    kpe = kpe_ref[...]
    for h in range(HEADS):
        y = _dot(cq, wq_ref[h])
        pe = y[:, NOPE:NOPE + LANES] * cos + y[:, NOPE + LANES:] * sin
        q_ref[h, :, :NOPE] = (y[:, :NOPE] * scale).astype(BF16)
        q_ref[h, :, NOPE:] = (pe * scale).astype(BF16)
        z = _dot(ckv, wkv_ref[h])
        k_ref[h, :, :NOPE] = z[:, :NOPE].astype(BF16)
        k_ref[h, :, NOPE:] = kpe
        v_ref[h] = z[:, NOPE:].astype(BF16)


def qkv(cq, ckv, kpe, cosT, sinT, wq, wkv, tm=512):
    S = cq.shape[0]
    scale = float((NOPE + ROPE) ** -0.5 * LOG2E)
    row = lambda i: (i, 0)
    return pl.pallas_call(
        functools.partial(_qkv_kernel, scale),
        grid=(S // tm,),
        in_specs=[pl.BlockSpec((tm, Q_LORA), row), pl.BlockSpec((tm, KV_LORA), row),
                  pl.BlockSpec((tm, LANES), row), pl.BlockSpec((tm, LANES), row),
                  pl.BlockSpec((tm, LANES), row),
                  pl.BlockSpec(wq.shape, lambda i: (0, 0, 0)),
                  pl.BlockSpec(wkv.shape, lambda i: (0, 0, 0))],
        out_specs=[pl.BlockSpec((HEADS, tm, QK_PAD), lambda i: (0, i, 0)),
                   pl.BlockSpec((HEADS, tm, QK_PAD), lambda i: (0, i, 0)),
                   pl.BlockSpec((HEADS, tm, VDIM), lambda i: (0, i, 0))],
        out_shape=[jax.ShapeDtypeStruct((HEADS, S, QK_PAD), BF16),
                   jax.ShapeDtypeStruct((HEADS, S, QK_PAD), BF16),
                   jax.ShapeDtypeStruct((HEADS, S, VDIM), BF16)],
        compiler_params=_cparams(("arbitrary",)),
        name="qkv",
    )(cq, ckv, kpe, cosT, sinT, wq, wkv)


def _attn_kernel(t, hpb, q_ref, k_ref, v_ref, o_ref, m_ref, l_ref, acc_ref):
    i = pl.program_id(1)
    nb = t // LANES
    m_ref[...] = jnp.full(m_ref.shape, -jnp.inf, F32)
    l_ref[...] = jnp.zeros(l_ref.shape, F32)
    acc_ref[...] = jnp.zeros(acc_ref.shape, F32)

    def lane_blocks(x):
        return [x[:, b * LANES:(b + 1) * LANES] for b in range(nb)]

    def step(j, masked):
        start = pl.multiple_of(j * t, t)
        for hh in range(hpb):
            k = k_ref[hh, pl.ds(start, t), :]
            v = v_ref[hh, pl.ds(start, t), :]
            s = lax.dot_general(q_ref[hh], k, (((1,), (1,)), ((), ())), preferred_element_type=F32)
            if masked:
                row = lax.broadcasted_iota(jnp.int32, (t, t), 0)
                col = lax.broadcasted_iota(jnp.int32, (t, t), 1)
                s = jnp.where(col <= row, s, MASK_VALUE)
            m_prev = m_ref[hh]
            m_cur = functools.reduce(jnp.maximum, lane_blocks(s))
            m_new = jnp.maximum(m_prev, jnp.max(m_cur, axis=-1, keepdims=True))
            alpha = jnp.exp2(m_prev - m_new)
            p = jnp.exp2(s - jnp.concatenate([m_new] * nb, axis=1))
            l_ref[hh] = alpha * l_ref[hh] + functools.reduce(jnp.add, lane_blocks(p))
            acc_ref[hh] = alpha * acc_ref[hh] + _dot(p.astype(BF16), v)
            m_ref[hh] = m_new

    def body(j, c):
        step(j, False)
        return c

    lax.fori_loop(0, i, body, 0)
    step(i, True)
    for hh in range(hpb):
        l = jnp.sum(l_ref[hh], axis=-1, keepdims=True)
        o_ref[:, hh * VDIM:(hh + 1) * VDIM] = (acc_ref[hh] / l).astype(BF16)


def attention(q, k, v, t=512, hpb=2):
    S = q.shape[1]
    return pl.pallas_call(
        functools.partial(_attn_kernel, t, hpb),
        grid=(HEADS // hpb, S // t),
        in_specs=[pl.BlockSpec((hpb, t, QK_PAD), lambda h, i: (h, i, 0)),
                  pl.BlockSpec((hpb, S, QK_PAD), lambda h, i: (h, 0, 0)),
                  pl.BlockSpec((hpb, S, VDIM), lambda h, i: (h, 0, 0))],
        out_specs=pl.BlockSpec((t, hpb * VDIM), lambda h, i: (i, h)),
        out_shape=jax.ShapeDtypeStruct((S, HEADS * VDIM), BF16),
        scratch_shapes=[pltpu.VMEM((hpb, t, LANES), F32), pltpu.VMEM((hpb, t, LANES), F32),
                        pltpu.VMEM((hpb, t, VDIM), F32)],
        compiler_params=_cparams(("arbitrary", "arbitrary")),
        name="attention",
    )(q, k, v)


def _hgrn_kernel(nblk, hpb, q_ref, f_ref, v_ref, gate_ref, gn_ref, o_ref, st_ref, bs2_ref):
    @pl.when(pl.program_id(1) == 0)
    def _():
        st_ref[...] = jnp.zeros(st_ref.shape, F32)

    B = LANES
    r = lax.broadcasted_iota(jnp.int32, (B, B), 0)
    c = lax.broadcasted_iota(jnp.int32, (B, B), 1)
    same = (r // SUB) == (c // SUB)
    causal = same & (c <= r)
    incl = jnp.where(causal, 1.0, 0.0).astype(BF16)
    nsub = B // SUB
    row_sub = r // SUB
    col_sub = c // SUB
    lane = lax.broadcasted_iota(jnp.int32, (SUB, B), 1)
    ones = jnp.ones((HG_D, B), BF16)
    gn = gn_ref[...]

    def one_head(hh, rows):
        cols = slice(hh * HG_D, (hh + 1) * HG_D)
        f = f_ref[rows, cols]
        q = q_ref[rows, cols]
        g = jnp.log(jnp.maximum(f, MIN_FORGET))
        g0 = g.astype(BF16)
        r1 = g - g0.astype(F32)
        g1 = r1.astype(BF16)
        g2 = (r1 - g1.astype(F32)).astype(BF16)
        b3 = _dot(incl, jnp.concatenate([g0, g1, g2], axis=1))
        b = b3[:, :HG_D] + b3[:, HG_D:2 * HG_D] + b3[:, 2 * HG_D:]
        b_last = jnp.concatenate(
            [jnp.broadcast_to(b[(u + 1) * SUB - 1:(u + 1) * SUB], (SUB, HG_D)) for u in range(nsub)], axis=0)
        eb = jnp.exp(b)
        kin = jnp.maximum(1.0 - f, 0.0)
        qe = (q * eb).astype(BF16)
        kd = (kin * jnp.exp(b_last - b)).astype(BF16)
        v = v_ref[rows, cols]
        vb = v.astype(BF16)
        b2 = b * LOG2E
        bs2_ref[hh] = b2 - jnp.log2(kin)
        yield
        ps = []
        for u in range(nsub):
            lo = u * SUB
            q16 = q[lo:lo + SUB]
            b16 = b2[lo:lo + SUB]
            for s in range(SUB):
                e = jnp.exp2(b16 - bs2_ref[hh, pl.ds(lo + s, 1), :])
                ps.append((q16 * e).astype(BF16))
        rs = _dot(jnp.concatenate(ps, axis=0), ones)
        yield
        a_rows = []
        for u in range(nsub):
            a = jnp.zeros((SUB, B), F32)
            for s in range(SUB):
                rr = u * SUB + s
                a = jnp.where(lane == rr, rs[rr * SUB:(rr + 1) * SUB], a)
            a_rows.append(a)
        a_blk = jnp.where(causal, jnp.concatenate(a_rows, axis=0), 0.0).astype(BF16)

        vt = jnp.transpose(v).astype(BF16)
        vt_u = jnp.concatenate([jnp.where(col_sub == u, vt, 0.0) for u in range(nsub)], axis=0)
        upd = _dot(vt_u, kd)
        yield
        st = st_ref[hh]
        sts = []
        for u in range(nsub):
            sts.append(st.astype(BF16))
            st = st * eb[(u + 1) * SUB - 1:(u + 1) * SUB] + upd[u * B:(u + 1) * B]
        st_ref[hh] = st
        qe_u = jnp.concatenate([jnp.where(row_sub == u, qe, 0.0) for u in range(nsub)], axis=1)
        o_int = lax.dot_general(qe_u, jnp.concatenate(sts, axis=1), (((1,), (1,)), ((), ())),
                                preferred_element_type=F32)
        o = _dot(a_blk, vb) + o_int
        o = _rms(o, gn) * gate_ref[rows, cols]
        o_ref[rows, cols] = o.astype(BF16)

    def blk_body(bi, carry):
        rows = pl.ds(pl.multiple_of(bi * B, B), B)
        live = [one_head(hh, rows) for hh in range(hpb)]
        while live:
            live = [gen for gen in live if next(gen, live) is not live]
        return carry

    lax.fori_loop(0, nblk, blk_body, 0)


def hgrn(p2, gn, ts=1024, hpb=4):
    S = p2.shape[0]
    w = hpb * HG_D
    hb = D_MODEL // w
    return pl.pallas_call(
        functools.partial(_hgrn_kernel, ts // LANES, hpb),
        grid=(HEADS // hpb, S // ts),
        in_specs=[pl.BlockSpec((ts, w), lambda h, i: (i, h)),
                  pl.BlockSpec((ts, w), lambda h, i: (i, hb + h)),
                  pl.BlockSpec((ts, w), lambda h, i: (i, 2 * hb + h)),
                  pl.BlockSpec((ts, w), lambda h, i: (i, 3 * hb + h)),
                  pl.BlockSpec((1, HG_D), lambda h, i: (0, 0))],
        out_specs=pl.BlockSpec((ts, w), lambda h, i: (i, h)),
        out_shape=jax.ShapeDtypeStruct((S, D_MODEL), BF16),
        scratch_shapes=[pltpu.VMEM((hpb, HG_D, HG_D), F32), pltpu.VMEM((hpb, LANES, HG_D), F32)],
        compiler_params=_cparams(("arbitrary", "arbitrary")),
        name="hgrn",
    )(p2, p2, p2, p2, gn)


def _merge_kernel(nj, a_ref, r_ref, ga_ref, gb_ref, wa_ref, wb_ref, wo_ref, x_ref, o_ref, acc_ref):
    j = pl.program_id(1)
    ma = _dot(a_ref[...], wa_ref[...])
    mb = _dot(r_ref[...], wb_ref[...])
    mg = (ga_ref[...] * ma + gb_ref[...] * mb).astype(BF16)
    c = _dot(mg, wo_ref[...])

    @pl.when(j == 0)
    def _():
        acc_ref[...] = c

    @pl.when(j > 0)
    def _():
        acc_ref[...] += c

    @pl.when(j == nj - 1)
    def _():
        o_ref[...] = x_ref[...] + acc_ref[...]


def merge(attn, rec, p2, wa, wb, wo, x, tm=512, tn=256):
    S = x.shape[0]
    nj = D_MODEL // tn
    ga0 = 4 * D_MODEL // tn
    gb0 = 5 * D_MODEL // tn
    return pl.pallas_call(
        functools.partial(_merge_kernel, nj),
        grid=(S // tm, nj),
        in_specs=[pl.BlockSpec((tm, D_MODEL), lambda i, j: (i, 0)),
                  pl.BlockSpec((tm, D_MODEL), lambda i, j: (i, 0)),
                  pl.BlockSpec((tm, tn), lambda i, j: (i, ga0 + j)),
                  pl.BlockSpec((tm, tn), lambda i, j: (i, gb0 + j)),
                  pl.BlockSpec((D_MODEL, tn), lambda i, j: (0, j)),
                  pl.BlockSpec((D_MODEL, tn), lambda i, j: (0, j)),
                  pl.BlockSpec((tn, D_MODEL), lambda i, j: (j, 0)),
                  pl.BlockSpec((tm, D_MODEL), lambda i, j: (i, 0))],
        out_specs=pl.BlockSpec((tm, D_MODEL), lambda i, j: (i, 0)),
        out_shape=jax.ShapeDtypeStruct((S, D_MODEL), F32),
        scratch_shapes=[pltpu.VMEM((tm, D_MODEL), F32)],
        compiler_params=_cparams(("arbitrary", "arbitrary")),
        name="merge",
    )(attn, rec, p2, p2, wa, wb, wo, x)


def _ffn_kernel(final, nj, tm, x_ref, xh_ref, g_ref, wg_ref, wu_ref, cwg_ref, cwu_ref, cbg_ref, cbu_ref,
                wd_ref, fg_ref, o_ref, h_ref, ug_ref, uu_ref, acc_ref):
    m = pl.program_id(0)
    j = pl.program_id(1)

    @pl.when(j == 0)
    def _():
        g = g_ref[...]
        h_ref[HALO:, :] = _rms(x_ref[...], g).astype(BF16)
        hh = jnp.where(m > 0, _rms(xh_ref[...], g), 0.0)
        h_ref[:HALO, :] = hh.astype(BF16)

    h = h_ref[...]
    ug_ref[...] = _dot(h, wg_ref[...])
    uu_ref[...] = _dot(h, wu_ref[...])

    def conv(u_ref, cw_ref, cb_ref):
        y = cb_ref[...]
        for tap in range(3):
            y = y + cw_ref[tap:tap + 1, :] * u_ref[pl.ds(HALO - 2 + tap, tm), :]
        return y

    yg = conv(ug_ref, cwg_ref, cbg_ref)
    yu = conv(uu_ref, cwu_ref, cbu_ref)
    act = (yg * jax.nn.sigmoid(yg) * yu).astype(BF16)
    c = _dot(act, wd_ref[...])

    @pl.when(j == 0)
    def _():
        acc_ref[...] = c

    @pl.when(j > 0)
    def _():
        acc_ref[...] += c

    @pl.when(j == nj - 1)
    def _():
        y = x_ref[...] + acc_ref[...]
        if final:
            y = _rms(y, fg_ref[...])
        o_ref[...] = y


def ffn(final, x, g, w_up, conv_w, conv_b, w_down, fg, tm=512, tn=512):
    S = x.shape[0]
    nj = D_FF // tn
    hb = tm // HALO
    return pl.pallas_call(
        functools.partial(_ffn_kernel, final, nj, tm),
        grid=(S // tm, nj),
        in_specs=[pl.BlockSpec((tm, D_MODEL), lambda i, j: (i, 0)),
                  pl.BlockSpec((HALO, D_MODEL), lambda i, j: (jnp.maximum(i * hb - 1, 0), 0)),
                  pl.BlockSpec((1, D_MODEL), lambda i, j: (0, 0)),
                  pl.BlockSpec((D_MODEL, tn), lambda i, j: (0, j)),
                  pl.BlockSpec((D_MODEL, tn), lambda i, j: (0, nj + j)),
                  pl.BlockSpec((3, tn), lambda i, j: (0, j)),
                  pl.BlockSpec((3, tn), lambda i, j: (0, nj + j)),
                  pl.BlockSpec((1, tn), lambda i, j: (0, j)),
                  pl.BlockSpec((1, tn), lambda i, j: (0, nj + j)),
                  pl.BlockSpec((tn, D_MODEL), lambda i, j: (j, 0)),
                  pl.BlockSpec((1, D_MODEL), lambda i, j: (0, 0))],
        out_specs=pl.BlockSpec((tm, D_MODEL), lambda i, j: (i, 0)),
        out_shape=jax.ShapeDtypeStruct((S, D_MODEL), F32),
        scratch_shapes=[pltpu.VMEM((tm + HALO, D_MODEL), BF16),
                        pltpu.VMEM((tm + HALO, tn), F32),
                        pltpu.VMEM((tm + HALO, tn), F32),
                        pltpu.VMEM((tm, D_MODEL), F32)],
        compiler_params=_cparams(("arbitrary", "arbitrary")),
        name="ffn",
    )(x, x, g, w_up, w_up, conv_w, conv_w, conv_b, conv_b, w_down, fg)


def _rope_pair_cols(w):
    x1, x2 = w[..., :ROPE // 2], w[..., ROPE // 2:]
    z = jnp.zeros(w.shape[:-1] + (LANES - ROPE,), w.dtype)
    return jnp.concatenate([x1, x2, z], axis=-1), jnp.concatenate([x2, x1, z], axis=-1)


def kernel(x, positions, attn_norm_g, w_in, q_norm_g, w_uq, kv_norm_g, w_ukv, hgrn_lb_logits, hgrn_out_norm_g,
           w_branch_a, w_branch_b, w_out, ffn_norm_g, w_up, conv_w, conv_b, w_down, final_norm_g):
    B, S, D = x.shape
    assert B == 1 and D == D_MODEL
    xs = x.reshape(S, D)

    inv_freq = ROPE_THETA ** (-jnp.arange(0, ROPE, 2, dtype=F32) / ROPE)
    zeros = jnp.zeros((LANES - ROPE,), F32)
    invf = jnp.concatenate([inv_freq, inv_freq, zeros]).reshape(1, LANES)
    half = jnp.ones((ROPE // 2,), F32)
    sgn = jnp.concatenate([-half, half, zeros]).reshape(1, LANES)
    cosT, sinT = rope_tables(positions.reshape(S, 1), invf, sgn)

    n_lat = Q_LORA + KV_LORA
    lb_logits = hgrn_lb_logits.astype(F32)
    for l in range(DEPTH):
        wl = w_in[l]
        ka, kb = _rope_pair_cols(wl[:, n_lat:n_lat + ROPE])
        w1 = jnp.concatenate([wl[:, :n_lat], ka, kb], axis=-1).astype(BF16)
        w2 = wl[:, n_lat + ROPE:].astype(BF16)
        wq = w_uq[l].reshape(Q_LORA, HEADS, NOPE + ROPE)
        qa, qb = _rope_pair_cols(wq[..., NOPE:])
        wq = jnp.concatenate([wq[..., :NOPE], qa, qb], axis=-1).transpose(1, 0, 2).astype(BF16)
        wkv = w_ukv[l].reshape(KV_LORA, HEADS, NOPE + VDIM).transpose(1, 0, 2).astype(BF16)

        cq, ckv, kpe = in1(xs, attn_norm_g[l].reshape(1, D), w1, q_norm_g[l].reshape(1, Q_LORA),
                           kv_norm_g[l].reshape(1, KV_LORA), cosT, sinT)
        p2 = in2(l, xs, attn_norm_g[l].reshape(1, D), w2, lb_logits)
        q, k, v = qkv(cq, ckv, kpe, cosT, sinT, wq, wkv)
        attn = attention(q, k, v)
        rec = hgrn(p2, hgrn_out_norm_g[l].reshape(1, HG_D))
        xs = merge(attn, rec, p2, w_branch_a[l].astype(BF16), w_branch_b[l].astype(BF16),
                   w_out[l].astype(BF16), xs)
        xs = ffn(l == DEPTH - 1, xs, ffn_norm_g[l].reshape(1, D), w_up[l].astype(BF16), conv_w[l],
                 conv_b[l].reshape(1, 2 * D_FF), w_down[l].astype(BF16), final_norm_g.reshape(1, D))
    return xs.reshape(B, S, D)
```

```python
import functools

import jax
import jax.numpy as jnp
from jax import lax
from jax.experimental import pallas as pl
from jax.experimental.pallas import tpu as pltpu

F32 = jnp.float32
BF16 = jnp.bfloat16

D_MODEL = 2048
DEPTH = 4
HEADS = 16
Q_LORA = 512
KV_LORA = 512
NOPE = 128
ROPE = 64
VDIM = 128
ROPE_THETA = 10000.0
MASK_VALUE = -1e30
HG_D = 128
SUB = 16
MIN_FORGET = 1e-30
D_FF = 5632
EPS = 1e-6
LOG2E = 1.4426950408889634
LANES = 128
QK_PAD = 256
HALO = 16

VMEM_LIMIT = 52 * 1024 * 1024


def _cparams(sem):
    return pltpu.CompilerParams(dimension_semantics=sem, vmem_limit_bytes=VMEM_LIMIT)


def _rms(x, g):
    return x * lax.rsqrt(jnp.mean(x * x, axis=-1, keepdims=True) + EPS) * g


def _dot(a, b):
    return jnp.dot(a, b, preferred_element_type=F32)


def _rope_kernel(pos_ref, invf_ref, sgn_ref, cos_ref, sin_ref):
    ang = pos_ref[...].astype(F32) * invf_ref[...]
    sgn = sgn_ref[...]
    cos_ref[...] = jnp.cos(ang) * jnp.abs(sgn)
    sin_ref[...] = jnp.sin(ang) * sgn


def rope_tables(pos_col, invf, sgn, tm=1024):
    S = pos_col.shape[0]
    return pl.pallas_call(
        _rope_kernel,
        grid=(S // tm,),
        in_specs=[pl.BlockSpec((tm, 1), lambda i: (i, 0)),
                  pl.BlockSpec((1, LANES), lambda i: (0, 0)),
                  pl.BlockSpec((1, LANES), lambda i: (0, 0))],
        out_specs=[pl.BlockSpec((tm, LANES), lambda i: (i, 0))] * 2,
        out_shape=[jax.ShapeDtypeStruct((S, LANES), F32)] * 2,
        compiler_params=_cparams(("arbitrary",)),
        name="rope_tables",
    )(pos_col, invf, sgn)


def _in1_kernel(x_ref, g_ref, w_ref, qg_ref, kvg_ref, cos_ref, sin_ref, cq_ref, ckv_ref, kpe_ref):
    h = _rms(x_ref[...], g_ref[...]).astype(BF16)
    y = _dot(h, w_ref[...])
    cq_ref[...] = _rms(y[:, :Q_LORA], qg_ref[...]).astype(BF16)
    ckv_ref[...] = _rms(y[:, Q_LORA:Q_LORA + KV_LORA], kvg_ref[...]).astype(BF16)
    o = Q_LORA + KV_LORA
    kpe = y[:, o:o + LANES] * cos_ref[...] + y[:, o + LANES:o + 2 * LANES] * sin_ref[...]
    kpe_ref[...] = kpe.astype(BF16)


def in1(x, g, w1, qg, kvg, cosT, sinT, tm=512):
    S = x.shape[0]
    n1 = w1.shape[1]
    row = lambda i: (i, 0)
    fix = lambda i: (0, 0)
    return pl.pallas_call(
        _in1_kernel,
        grid=(S // tm,),
        in_specs=[pl.BlockSpec((tm, D_MODEL), row), pl.BlockSpec((1, D_MODEL), fix),
                  pl.BlockSpec((D_MODEL, n1), fix), pl.BlockSpec((1, Q_LORA), fix),
                  pl.BlockSpec((1, KV_LORA), fix), pl.BlockSpec((tm, LANES), row),
                  pl.BlockSpec((tm, LANES), row)],
        out_specs=[pl.BlockSpec((tm, Q_LORA), row), pl.BlockSpec((tm, KV_LORA), row),
                   pl.BlockSpec((tm, LANES), row)],
        out_shape=[jax.ShapeDtypeStruct((S, Q_LORA), BF16), jax.ShapeDtypeStruct((S, KV_LORA), BF16),
                   jax.ShapeDtypeStruct((S, LANES), BF16)],
        compiler_params=_cparams(("arbitrary",)),
        name="in1",
    )(x, g, w1, qg, kvg, cosT, sinT)


def _in2_kernel(layer, x_ref, g_ref, wq_ref, wf_ref, wi_ref, wg_ref, wa_ref, wb_ref, lbl_ref,
                oq_ref, of_ref, oi_ref, og_ref, oa_ref, ob_ref, h_ref):
    @pl.when(pl.program_id(1) == 0)
    def _():
        h_ref[...] = _rms(x_ref[...], g_ref[...]).astype(BF16)

    h = h_ref[...]
    y = _dot(h, wq_ref[...])
    oq_ref[...] = y * jax.nn.sigmoid(y)
    y = _dot(h, wf_ref[...])
    lg = lbl_ref[...]
    e = jnp.exp(lg - jnp.max(lg, axis=0, keepdims=True))
    p = e / jnp.sum(e, axis=0, keepdims=True)
    lb = jnp.sum(p[:layer + 1], axis=0, keepdims=True) - p[0:1]
    of_ref[...] = lb + (1.0 - lb) * jax.nn.sigmoid(y)
    y = _dot(h, wi_ref[...])
    oi_ref[...] = y.astype(BF16)
    y = _dot(h, wg_ref[...])
    og_ref[...] = y * jax.nn.sigmoid(y)
    y = _dot(h, wa_ref[...])
    oa_ref[...] = jax.nn.sigmoid(y)
    y = _dot(h, wb_ref[...])
    ob_ref[...] = jax.nn.sigmoid(y)


def in2(layer, x, g, w2, lb_logits, tm=1024, tn=256):
    S = x.shape[0]
    nseg = w2.shape[1] // D_MODEL
    bps = D_MODEL // tn
    def seg_cols(s, i, j):
        return 0, s * bps + j

    w_specs = [pl.BlockSpec((D_MODEL, tn), functools.partial(seg_cols, s)) for s in range(nseg)]
    out_dtypes = [F32, F32, BF16, F32, F32, F32]
    return pl.pallas_call(
        functools.partial(_in2_kernel, layer),
        grid=(S // tm, bps),
        in_specs=[pl.BlockSpec((tm, D_MODEL), lambda i, j: (i, 0)),
                  pl.BlockSpec((1, D_MODEL), lambda i, j: (0, 0))] + w_specs +
                 [pl.BlockSpec((DEPTH, tn), lambda i, j: (0, j))],
        out_specs=[pl.BlockSpec((tm, tn), lambda i, j: (i, j))] * nseg,
        out_shape=[jax.ShapeDtypeStruct((S, D_MODEL), dt) for dt in out_dtypes],
        scratch_shapes=[pltpu.VMEM((tm, D_MODEL), BF16)],
        compiler_params=_cparams(("arbitrary", "arbitrary")),
        name="in2",
    )(x, g, *([w2] * nseg), lb_logits)


def _qkv_kernel(scale, cq_ref, ckv_ref, kpe_ref, cos_ref, sin_ref, wq_ref, wkv_ref, q_ref, k_ref, v_ref):
    cq = cq_ref[...]
    ckv = ckv_ref[...]
    cos = cos_ref[...]
    sin = sin_ref[...]
    kpe = kpe_ref[...]
    for h in range(HEADS):
        y = _dot(cq, wq_ref[h])
        pe = y[:, NOPE:NOPE + LANES] * cos + y[:, NOPE + LANES:] * sin
        q_ref[h, :, :NOPE] = (y[:, :NOPE] * scale).astype(BF16)
        q_ref[h, :, NOPE:] = (pe * scale).astype(BF16)
        z = _dot(ckv, wkv_ref[h])
        k_ref[h, :, :NOPE] = z[:, :NOPE].astype(BF16)
        k_ref[h, :, NOPE:] = kpe
        v_ref[h] = z[:, NOPE:].astype(BF16)


def qkv(cq, ckv, kpe, cosT, sinT, wq, wkv, tm=512):
    S = cq.shape[0]
    scale = float((NOPE + ROPE) ** -0.5 * LOG2E)
    row = lambda i: (i, 0)
    return pl.pallas_call(
        functools.partial(_qkv_kernel, scale),
        grid=(S // tm,),
        in_specs=[pl.BlockSpec((tm, Q_LORA), row), pl.BlockSpec((tm, KV_LORA), row),
                  pl.BlockSpec((tm, LANES), row), pl.BlockSpec((tm, LANES), row),
                  pl.BlockSpec((tm, LANES), row),
                  pl.BlockSpec(wq.shape, lambda i: (0, 0, 0)),
                  pl.BlockSpec(wkv.shape, lambda i: (0, 0, 0))],
        out_specs=[pl.BlockSpec((HEADS, tm, QK_PAD), lambda i: (0, i, 0)),
                   pl.BlockSpec((HEADS, tm, QK_PAD), lambda i: (0, i, 0)),
                   pl.BlockSpec((HEADS, tm, VDIM), lambda i: (0, i, 0))],
        out_shape=[jax.ShapeDtypeStruct((HEADS, S, QK_PAD), BF16),
                   jax.ShapeDtypeStruct((HEADS, S, QK_PAD), BF16),
                   jax.ShapeDtypeStruct((HEADS, S, VDIM), BF16)],
        compiler_params=_cparams(("arbitrary",)),
        name="qkv",
    )(cq, ckv, kpe, cosT, sinT, wq, wkv)


def _run_phases(gens):
    live = list(gens)
    while live:
        live = [g for g in live if next(g, live) is not live]


def _attn_kernel(t, hpb, nsplit, q_ref, k_ref, v_ref, o_ref, m_ref, l_ref, acc_ref):
    i = pl.program_id(1)
    nb = t // LANES
    m_ref[...] = jnp.full(m_ref.shape, -jnp.inf, F32)
    l_ref[...] = jnp.zeros(l_ref.shape, F32)
    acc_ref[...] = jnp.zeros(acc_ref.shape, F32)

    def lane_blocks(x):
        return [x[:, b * LANES:(b + 1) * LANES] for b in range(nb)]

    def chain(hh, part, j, masked):
        start = pl.multiple_of(j * t, t)
        rows = slice(part * t, (part + 1) * t)
        k = k_ref[hh, pl.ds(start, t), :]
        s = lax.dot_general(q_ref[hh, rows, :], k, (((1,), (1,)), ((), ())), preferred_element_type=F32)
        yield
        if masked:
            row = lax.broadcasted_iota(jnp.int32, (t, t), 0)
            col = lax.broadcasted_iota(jnp.int32, (t, t), 1)
            s = jnp.where(col <= row, s, MASK_VALUE)
        m_prev = m_ref[hh, rows, :]
        m_cur = functools.reduce(jnp.maximum, lane_blocks(s))
        m_new = jnp.maximum(m_prev, jnp.max(m_cur, axis=-1, keepdims=True))
        alpha = jnp.exp2(m_prev - m_new)
        p = jnp.exp2(s - jnp.concatenate([m_new] * nb, axis=1))
        l_ref[hh, rows, :] = alpha * l_ref[hh, rows, :] + functools.reduce(jnp.add, lane_blocks(p))
        v = v_ref[hh, pl.ds(start, t), :]
        acc_ref[hh, rows, :] = alpha * acc_ref[hh, rows, :] + _dot(p.astype(BF16), v)
        m_ref[hh, rows, :] = m_new

    def body(j, c):
        _run_phases([chain(hh, part, j, False) for hh in range(hpb) for part in range(nsplit)])
        return c

    lax.fori_loop(0, nsplit * i, body, 0)
    for d in range(nsplit):
        _run_phases([chain(hh, part, nsplit * i + d, part == d)
                     for hh in range(hpb) for part in range(d, nsplit)])
    for hh in range(hpb):
        l = jnp.sum(l_ref[hh], axis=-1, keepdims=True)
        o_ref[:, hh * VDIM:(hh + 1) * VDIM] = (acc_ref[hh] / l).astype(BF16)


def attention(q, k, v, t=512, hpb=2, nsplit=2):
    S = q.shape[1]
    tq = nsplit * t
    return pl.pallas_call(
        functools.partial(_attn_kernel, t, hpb, nsplit),
        grid=(HEADS // hpb, S // tq),
        in_specs=[pl.BlockSpec((hpb, tq, QK_PAD), lambda h, i: (h, i, 0)),
                  pl.BlockSpec((hpb, S, QK_PAD), lambda h, i: (h, 0, 0)),
                  pl.BlockSpec((hpb, S, VDIM), lambda h, i: (h, 0, 0))],
        out_specs=pl.BlockSpec((tq, hpb * VDIM), lambda h, i: (i, h)),
        out_shape=jax.ShapeDtypeStruct((S, HEADS * VDIM), BF16),
        scratch_shapes=[pltpu.VMEM((hpb, tq, LANES), F32), pltpu.VMEM((hpb, tq, LANES), F32),
                        pltpu.VMEM((hpb, tq, VDIM), F32)],
        compiler_params=_cparams(("arbitrary", "arbitrary")),
        name="attention",
    )(q, k, v)


def _hgrn_kernel(nblk, hpb, q_ref, f_ref, v_ref, gate_ref, gn_ref, o_ref, st_ref, bs2_ref):
    @pl.when(pl.program_id(1) == 0)
    def _():
        st_ref[...] = jnp.zeros(st_ref.shape, F32)

    B = LANES
    r = lax.broadcasted_iota(jnp.int32, (B, B), 0)
    c = lax.broadcasted_iota(jnp.int32, (B, B), 1)
    same = (r // SUB) == (c // SUB)
    causal = same & (c <= r)
    incl = jnp.where(causal, 1.0, 0.0).astype(BF16)
    nsub = B // SUB
    row_sub = r // SUB
    col_sub = c // SUB
    lane = lax.broadcasted_iota(jnp.int32, (SUB, B), 1)
    ones = jnp.ones((HG_D, B), BF16)
    gn = gn_ref[...]

    def one_head(hh, rows):
        cols = slice(hh * HG_D, (hh + 1) * HG_D)
        f = f_ref[rows, cols]
        q = q_ref[rows, cols]
        g = jnp.log(jnp.maximum(f, MIN_FORGET))
        g0 = g.astype(BF16)
        r1 = g - g0.astype(F32)
        g1 = r1.astype(BF16)
        g2 = (r1 - g1.astype(F32)).astype(BF16)
        b3 = _dot(incl, jnp.concatenate([g0, g1, g2], axis=1))
        b = b3[:, :HG_D] + b3[:, HG_D:2 * HG_D] + b3[:, 2 * HG_D:]
        b_last = jnp.concatenate(
            [jnp.broadcast_to(b[(u + 1) * SUB - 1:(u + 1) * SUB], (SUB, HG_D)) for u in range(nsub)], axis=0)
        eb = jnp.exp(b)
        kin = jnp.maximum(1.0 - f, 0.0)
        qe = (q * eb).astype(BF16)
        kd = (kin * jnp.exp(b_last - b)).astype(BF16)
        vb = v_ref[rows, cols]
        b2 = b * LOG2E
        bs2_ref[hh] = b2 - jnp.log2(kin)
        yield
        ps = []
        for u in range(nsub):
            lo = u * SUB
            q16 = q[lo:lo + SUB]
            b16 = b2[lo:lo + SUB]
            for s in range(SUB):
                e = jnp.exp2(b16 - bs2_ref[hh, pl.ds(lo + s, 1), :])
                ps.append((q16 * e).astype(BF16))
        rs = _dot(jnp.concatenate(ps, axis=0), ones)
        yield
        a_rows = []
        for u in range(nsub):
            a = jnp.zeros((SUB, B), F32)
            for s in range(SUB):
                rr = u * SUB + s
                a = jnp.where(lane == rr, rs[rr * SUB:(rr + 1) * SUB], a)
            a_rows.append(a)
        a_blk = jnp.where(causal, jnp.concatenate(a_rows, axis=0), 0.0).astype(BF16)

        vt = jnp.transpose(vb)
        vt_u = jnp.concatenate([jnp.where(col_sub == u, vt, 0.0) for u in range(nsub)], axis=0)
        upd = _dot(vt_u, kd)
        yield
        st = st_ref[hh]
        sts = []
        for u in range(nsub):
            sts.append(st.astype(BF16))
            st = st * eb[(u + 1) * SUB - 1:(u + 1) * SUB] + upd[u * B:(u + 1) * B]
        st_ref[hh] = st
        qe_u = jnp.concatenate([jnp.where(row_sub == u, qe, 0.0) for u in range(nsub)], axis=1)
        o_int = lax.dot_general(qe_u, jnp.concatenate(sts, axis=1), (((1,), (1,)), ((), ())),
                                preferred_element_type=F32)
        o = _dot(a_blk, vb) + o_int
        o = _rms(o, gn) * gate_ref[rows, cols]
        o_ref[rows, cols] = o.astype(BF16)

    def blk_body(bi, carry):
        rows = pl.ds(pl.multiple_of(bi * B, B), B)
        _run_phases([one_head(hh, rows) for hh in range(hpb)])
        return carry

    lax.fori_loop(0, nblk, blk_body, 0)


def hgrn(q, f, v, gate, gn, ts=1024, hpb=4):
    S = q.shape[0]
    w = hpb * HG_D
    blk = pl.BlockSpec((ts, w), lambda h, i: (i, h))
    return pl.pallas_call(
        functools.partial(_hgrn_kernel, ts // LANES, hpb),
        grid=(HEADS // hpb, S // ts),
        in_specs=[blk, blk, blk, blk, pl.BlockSpec((1, HG_D), lambda h, i: (0, 0))],
        out_specs=blk,
        out_shape=jax.ShapeDtypeStruct((S, D_MODEL), BF16),
        scratch_shapes=[pltpu.VMEM((hpb, HG_D, HG_D), F32), pltpu.VMEM((hpb, LANES, HG_D), F32)],
        compiler_params=_cparams(("arbitrary", "arbitrary")),
        name="hgrn",
    )(q, f, v, gate, gn)


def _merge_kernel(a_ref, r_ref, ga_ref, gb_ref, wa_ref, wb_ref, wo_ref, x_ref, o_ref):
    @pl.when(pl.program_id(1) == 0)
    def _():
        o_ref[...] = x_ref[...]

    ma = _dot(a_ref[...], wa_ref[...])
    mb = _dot(r_ref[...], wb_ref[...])
    mg = (ga_ref[...] * ma + gb_ref[...] * mb).astype(BF16)
    o_ref[...] += _dot(mg, wo_ref[...])


def merge(attn, rec, ga, gb, wa, wb, wo, x, tm=1024, tn=256):
    S = x.shape[0]
    row_once = dict(pipeline_mode=pl.Buffered(1))
    return pl.pallas_call(
        _merge_kernel,
        grid=(S // tm, D_MODEL // tn),
        in_specs=[pl.BlockSpec((tm, D_MODEL), lambda i, j: (i, 0)),
                  pl.BlockSpec((tm, D_MODEL), lambda i, j: (i, 0)),
                  pl.BlockSpec((tm, tn), lambda i, j: (i, j)),
                  pl.BlockSpec((tm, tn), lambda i, j: (i, j)),
                  pl.BlockSpec((D_MODEL, tn), lambda i, j: (0, j)),
                  pl.BlockSpec((D_MODEL, tn), lambda i, j: (0, j)),
                  pl.BlockSpec((tn, D_MODEL), lambda i, j: (j, 0)),
                  pl.BlockSpec((tm, D_MODEL), lambda i, j: (i, 0), **row_once)],
        out_specs=pl.BlockSpec((tm, D_MODEL), lambda i, j: (i, 0), **row_once),
        out_shape=jax.ShapeDtypeStruct((S, D_MODEL), F32),
        compiler_params=_cparams(("arbitrary", "arbitrary")),
        name="merge",
    )(attn, rec, ga, gb, wa, wb, wo, x)


def _ffn_kernel(final, nj, tm, x_ref, xh_ref, g_ref, wg_ref, wu_ref, cwg_ref, cwu_ref, cbg_ref, cbu_ref,
                wd_ref, fg_ref, o_ref, h_ref, ug_ref, uu_ref):
    m = pl.program_id(0)
    j = pl.program_id(1)

    @pl.when(j == 0)
    def _():
        g = g_ref[...]
        x = x_ref[...]
        o_ref[...] = x
        h_ref[HALO:, :] = _rms(x, g).astype(BF16)
        hh = jnp.where(m > 0, _rms(xh_ref[...], g), 0.0)
        h_ref[:HALO, :] = hh.astype(BF16)

    h = h_ref[...]
    ug_ref[...] = _dot(h, wg_ref[...])
    uu_ref[...] = _dot(h, wu_ref[...])

    def conv(u_ref, cw_ref, cb_ref):
        y = cb_ref[...]
        for tap in range(3):
            y = y + cw_ref[tap:tap + 1, :] * u_ref[pl.ds(HALO - 2 + tap, tm), :]
        return y

    yg = conv(ug_ref, cwg_ref, cbg_ref)
    yu = conv(uu_ref, cwu_ref, cbu_ref)
    act = (yg * jax.nn.sigmoid(yg) * yu).astype(BF16)
    o_ref[...] += _dot(act, wd_ref[...])

    if final:
        @pl.when(j == nj - 1)
        def _():
            o_ref[...] = _rms(o_ref[...], fg_ref[...])


def ffn(final, x, g, w_up, conv_w, conv_b, w_down, fg, tm=1024, tn=512):
    S = x.shape[0]
    nj = D_FF // tn
    hb = tm // HALO
    row_once = dict(pipeline_mode=pl.Buffered(1))
    return pl.pallas_call(
        functools.partial(_ffn_kernel, final, nj, tm),
        grid=(S // tm, nj),
        in_specs=[pl.BlockSpec((tm, D_MODEL), lambda i, j: (i, 0), **row_once),
                  pl.BlockSpec((HALO, D_MODEL), lambda i, j: (jnp.maximum(i * hb - 1, 0), 0)),
                  pl.BlockSpec((1, D_MODEL), lambda i, j: (0, 0)),
                  pl.BlockSpec((D_MODEL, tn), lambda i, j: (0, j)),
                  pl.BlockSpec((D_MODEL, tn), lambda i, j: (0, nj + j)),
                  pl.BlockSpec((3, tn), lambda i, j: (0, j)),
                  pl.BlockSpec((3, tn), lambda i, j: (0, nj + j)),
                  pl.BlockSpec((1, tn), lambda i, j: (0, j)),
                  pl.BlockSpec((1, tn), lambda i, j: (0, nj + j)),
                  pl.BlockSpec((tn, D_MODEL), lambda i, j: (j, 0)),
                  pl.BlockSpec((1, D_MODEL), lambda i, j: (0, 0))],
        out_specs=pl.BlockSpec((tm, D_MODEL), lambda i, j: (i, 0), **row_once),
        out_shape=jax.ShapeDtypeStruct((S, D_MODEL), F32),
        scratch_shapes=[pltpu.VMEM((tm + HALO, D_MODEL), BF16),
                        pltpu.VMEM((tm + HALO, tn), F32),
                        pltpu.VMEM((tm + HALO, tn), F32)],
        compiler_params=_cparams(("arbitrary", "arbitrary")),
        name="ffn",
    )(x, x, g, w_up, w_up, conv_w, conv_w, conv_b, conv_b, w_down, fg)


def _rope_pair_cols(w):
    x1, x2 = w[..., :ROPE // 2], w[..., ROPE // 2:]
    z = jnp.zeros(w.shape[:-1] + (LANES - ROPE,), w.dtype)
    return jnp.concatenate([x1, x2, z], axis=-1), jnp.concatenate([x2, x1, z], axis=-1)


def kernel(x, positions, attn_norm_g, w_in, q_norm_g, w_uq, kv_norm_g, w_ukv, hgrn_lb_logits, hgrn_out_norm_g,
           w_branch_a, w_branch_b, w_out, ffn_norm_g, w_up, conv_w, conv_b, w_down, final_norm_g):
    B, S, D = x.shape
    assert B == 1 and D == D_MODEL
    xs = x.reshape(S, D)

    inv_freq = ROPE_THETA ** (-jnp.arange(0, ROPE, 2, dtype=F32) / ROPE)
    zeros = jnp.zeros((LANES - ROPE,), F32)
    invf = jnp.concatenate([inv_freq, inv_freq, zeros]).reshape(1, LANES)
    half = jnp.ones((ROPE // 2,), F32)
    sgn = jnp.concatenate([-half, half, zeros]).reshape(1, LANES)
    cosT, sinT = rope_tables(positions.reshape(S, 1), invf, sgn)

    n_lat = Q_LORA + KV_LORA
    lb_logits = hgrn_lb_logits.astype(F32)
    for l in range(DEPTH):
        wl = w_in[l]
        ka, kb = _rope_pair_cols(wl[:, n_lat:n_lat + ROPE])
        w1 = jnp.concatenate([wl[:, :n_lat], ka, kb], axis=-1).astype(BF16)
        w2 = wl[:, n_lat + ROPE:].astype(BF16)
        wq = w_uq[l].reshape(Q_LORA, HEADS, NOPE + ROPE)
        qa, qb = _rope_pair_cols(wq[..., NOPE:])
        wq = jnp.concatenate([wq[..., :NOPE], qa, qb], axis=-1).transpose(1, 0, 2).astype(BF16)
        wkv = w_ukv[l].reshape(KV_LORA, HEADS, NOPE + VDIM).transpose(1, 0, 2).astype(BF16)

        cq, ckv, kpe = in1(xs, attn_norm_g[l].reshape(1, D), w1, q_norm_g[l].reshape(1, Q_LORA),
                           kv_norm_g[l].reshape(1, KV_LORA), cosT, sinT)
        hq, hf, hi, hg, ga, gb = in2(l, xs, attn_norm_g[l].reshape(1, D), w2, lb_logits)
        q, k, v = qkv(cq, ckv, kpe, cosT, sinT, wq, wkv)
        attn = attention(q, k, v)
        rec = hgrn(hq, hf, hi, hg, hgrn_out_norm_g[l].reshape(1, HG_D))
        xs = merge(attn, rec, ga, gb, w_branch_a[l].astype(BF16), w_branch_b[l].astype(BF16),
                   w_out[l].astype(BF16), xs)
        xs = ffn(l == DEPTH - 1, xs, ffn_norm_g[l].reshape(1, D), w_up[l].astype(BF16), conv_w[l],
                 conv_b[l].reshape(1, 2 * D_FF), w_down[l].astype(BF16), final_norm_g.reshape(1, D))
    return xs.reshape(B, S, D)
```

```python
import functools

import jax
import jax.numpy as jnp
from jax import lax
from jax.experimental import pallas as pl
from jax.experimental.pallas import tpu as pltpu

F32 = jnp.float32
BF16 = jnp.bfloat16

D_MODEL = 2048
DEPTH = 4
HEADS = 16
Q_LORA = 512
KV_LORA = 512
NOPE = 128
ROPE = 64
VDIM = 128
ROPE_THETA = 10000.0
MASK_VALUE = -1e30
HG_D = 128
SUB = 16
MIN_FORGET = 1e-30
D_FF = 5632
EPS = 1e-6
LOG2E = 1.4426950408889634
LANES = 128
QK_PAD = 256
HALO = 16

VMEM_LIMIT = 52 * 1024 * 1024


def _cparams(sem):
    return pltpu.CompilerParams(dimension_semantics=sem, vmem_limit_bytes=VMEM_LIMIT)


def _rms(x, g):
    return x * lax.rsqrt(jnp.mean(x * x, axis=-1, keepdims=True) + EPS) * g


def _dot(a, b):
    return jnp.dot(a, b, preferred_element_type=F32)


def _rope_kernel(pos_ref, invf_ref, sgn_ref, cos_ref, sin_ref):
    ang = pos_ref[...].astype(F32) * invf_ref[...]
    sgn = sgn_ref[...]
    cos_ref[...] = jnp.cos(ang) * jnp.abs(sgn)
    sin_ref[...] = jnp.sin(ang) * sgn


def rope_tables(pos_col, invf, sgn, tm=1024):
    S = pos_col.shape[0]
    return pl.pallas_call(
        _rope_kernel,
        grid=(S // tm,),
        in_specs=[pl.BlockSpec((tm, 1), lambda i: (i, 0)),
                  pl.BlockSpec((1, LANES), lambda i: (0, 0)),
                  pl.BlockSpec((1, LANES), lambda i: (0, 0))],
        out_specs=[pl.BlockSpec((tm, LANES), lambda i: (i, 0))] * 2,
        out_shape=[jax.ShapeDtypeStruct((S, LANES), F32)] * 2,
        compiler_params=_cparams(("arbitrary",)),
        name="rope_tables",
    )(pos_col, invf, sgn)


def _in1_kernel(x_ref, g_ref, w_ref, qg_ref, kvg_ref, cos_ref, sin_ref, cq_ref, ckv_ref, kpe_ref):
    h = _rms(x_ref[...], g_ref[...]).astype(BF16)
    y = _dot(h, w_ref[...])
    cq_ref[...] = _rms(y[:, :Q_LORA], qg_ref[...]).astype(BF16)
    ckv_ref[...] = _rms(y[:, Q_LORA:Q_LORA + KV_LORA], kvg_ref[...]).astype(BF16)
    o = Q_LORA + KV_LORA
    kpe = y[:, o:o + LANES] * cos_ref[...] + y[:, o + LANES:o + 2 * LANES] * sin_ref[...]
    kpe_ref[...] = kpe.astype(BF16)


def in1(layer, x, g, w1, qg, kvg, cosT, sinT, tm=512):
    S = x.shape[0]
    n1 = w1.shape[2]
    row = lambda i: (i, 0)
    lyr = lambda i: (layer, 0, 0)
    return pl.pallas_call(
        _in1_kernel,
        grid=(S // tm,),
        in_specs=[pl.BlockSpec((tm, D_MODEL), row), pl.BlockSpec((None, 1, D_MODEL), lyr),
                  pl.BlockSpec((None, D_MODEL, n1), lyr), pl.BlockSpec((None, 1, Q_LORA), lyr),
                  pl.BlockSpec((None, 1, KV_LORA), lyr), pl.BlockSpec((tm, LANES), row),
                  pl.BlockSpec((tm, LANES), row)],
        out_specs=[pl.BlockSpec((tm, Q_LORA), row), pl.BlockSpec((tm, KV_LORA), row),
                   pl.BlockSpec((tm, LANES), row)],
        out_shape=[jax.ShapeDtypeStruct((S, Q_LORA), BF16), jax.ShapeDtypeStruct((S, KV_LORA), BF16),
                   jax.ShapeDtypeStruct((S, LANES), BF16)],
        compiler_params=_cparams(("arbitrary",)),
        name="in1",
    )(x, g, w1, qg, kvg, cosT, sinT)


def _in2_kernel(layer, x_ref, g_ref, wq_ref, wf_ref, wi_ref, wg_ref, wa_ref, wb_ref, lbl_ref,
                oq_ref, of_ref, oi_ref, og_ref, oa_ref, ob_ref, h_ref):
    @pl.when(pl.program_id(1) == 0)
    def _():
        h_ref[...] = _rms(x_ref[...], g_ref[...]).astype(BF16)

    h = h_ref[...]
    y = _dot(h, wq_ref[...])
    oq_ref[...] = (y * jax.nn.sigmoid(y)).astype(BF16)
    y = _dot(h, wf_ref[...])
    lg = lbl_ref[...]
    e = jnp.exp(lg - jnp.max(lg, axis=0, keepdims=True))
    p = e / jnp.sum(e, axis=0, keepdims=True)
    lb = jnp.sum(p[:layer + 1], axis=0, keepdims=True) - p[0:1]
    of_ref[...] = lb + (1.0 - lb) * jax.nn.sigmoid(y)
    y = _dot(h, wi_ref[...])
    oi_ref[...] = y.astype(BF16)
    y = _dot(h, wg_ref[...])
    og_ref[...] = (y * jax.nn.sigmoid(y)).astype(BF16)
    y = _dot(h, wa_ref[...])
    oa_ref[...] = jax.nn.sigmoid(y).astype(BF16)
    y = _dot(h, wb_ref[...])
    ob_ref[...] = jax.nn.sigmoid(y).astype(BF16)


def in2(layer, x, g, w2, lb_logits, tm=1024, tn=256):
    S = x.shape[0]
    nseg = w2.shape[2] // D_MODEL
    bps = D_MODEL // tn

    def seg_cols(s, i, j):
        return layer, 0, s * bps + j

    w_specs = [pl.BlockSpec((None, D_MODEL, tn), functools.partial(seg_cols, s)) for s in range(nseg)]
    out_dtypes = [BF16, F32, BF16, BF16, BF16, BF16]
    return pl.pallas_call(
        functools.partial(_in2_kernel, layer),
        grid=(S // tm, bps),
        in_specs=[pl.BlockSpec((tm, D_MODEL), lambda i, j: (i, 0)),
                  pl.BlockSpec((None, 1, D_MODEL), lambda i, j: (layer, 0, 0))] + w_specs +
                 [pl.BlockSpec((DEPTH, tn), lambda i, j: (0, j))],
        out_specs=[pl.BlockSpec((tm, tn), lambda i, j: (i, j))] * nseg,
        out_shape=[jax.ShapeDtypeStruct((S, D_MODEL), dt) for dt in out_dtypes],
        scratch_shapes=[pltpu.VMEM((tm, D_MODEL), BF16)],
        compiler_params=_cparams(("arbitrary", "arbitrary")),
        name="in2",
    )(x, g, *([w2] * nseg), lb_logits)


def _qkv_kernel(scale, cq_ref, ckv_ref, kpe_ref, cos_ref, sin_ref, wq_ref, wkv_ref, q_ref, k_ref, v_ref):
    cq = cq_ref[...]
    ckv = ckv_ref[...]
    cos = cos_ref[...]
    sin = sin_ref[...]
    kpe = kpe_ref[...]
    qw = NOPE + 2 * LANES
    kvw = NOPE + VDIM
    for h in range(HEADS):
        y = _dot(cq, wq_ref[:, h * qw:(h + 1) * qw])
        pe = y[:, NOPE:NOPE + LANES] * cos + y[:, NOPE + LANES:] * sin
        q_ref[h, :, :NOPE] = (y[:, :NOPE] * scale).astype(BF16)
        q_ref[h, :, NOPE:] = (pe * scale).astype(BF16)
        z = _dot(ckv, wkv_ref[:, h * kvw:(h + 1) * kvw])
        k_ref[h, :, :NOPE] = z[:, :NOPE].astype(BF16)
        k_ref[h, :, NOPE:] = kpe
        v_ref[h] = z[:, NOPE:].astype(BF16)


def qkv(layer, cq, ckv, kpe, cosT, sinT, wq, wkv, tm=512):
    S = cq.shape[0]
    scale = float((NOPE + ROPE) ** -0.5 * LOG2E)
    row = lambda i: (i, 0)
    return pl.pallas_call(
        functools.partial(_qkv_kernel, scale),
        grid=(S // tm,),
        in_specs=[pl.BlockSpec((tm, Q_LORA), row), pl.BlockSpec((tm, KV_LORA), row),
                  pl.BlockSpec((tm, LANES), row), pl.BlockSpec((tm, LANES), row),
                  pl.BlockSpec((tm, LANES), row),
                  pl.BlockSpec((None,) + wq.shape[1:], lambda i: (layer, 0, 0)),
                  pl.BlockSpec((None,) + wkv.shape[1:], lambda i: (layer, 0, 0))],
        out_specs=[pl.BlockSpec((HEADS, tm, QK_PAD), lambda i: (0, i, 0)),
                   pl.BlockSpec((HEADS, tm, QK_PAD), lambda i: (0, i, 0)),
                   pl.BlockSpec((HEADS, tm, VDIM), lambda i: (0, i, 0))],
        out_shape=[jax.ShapeDtypeStruct((HEADS, S, QK_PAD), BF16),
                   jax.ShapeDtypeStruct((HEADS, S, QK_PAD), BF16),
                   jax.ShapeDtypeStruct((HEADS, S, VDIM), BF16)],
        compiler_params=_cparams(("arbitrary",)),
        name="qkv",
    )(cq, ckv, kpe, cosT, sinT, wq, wkv)


def _run_phases(gens):
    live = list(gens)
    while live:
        live = [g for g in live if next(g, live) is not live]


def _attn_kernel(t, hpb, nsplit, q_ref, k_ref, v_ref, o_ref, m_ref, l_ref, acc_ref):
    i = pl.program_id(1)
    nb = t // LANES
    m_ref[...] = jnp.full(m_ref.shape, -jnp.inf, F32)
    l_ref[...] = jnp.zeros(l_ref.shape, F32)
    acc_ref[...] = jnp.zeros(acc_ref.shape, F32)

    def lane_blocks(x):
        return [x[:, b * LANES:(b + 1) * LANES] for b in range(nb)]

    def chain(hh, part, j, masked):
        start = pl.multiple_of(j * t, t)
        rows = slice(part * t, (part + 1) * t)
        k = k_ref[hh, pl.ds(start, t), :]
        s = lax.dot_general(q_ref[hh, rows, :], k, (((1,), (1,)), ((), ())), preferred_element_type=F32)
        yield
        if masked:
            row = lax.broadcasted_iota(jnp.int32, (t, t), 0)
            col = lax.broadcasted_iota(jnp.int32, (t, t), 1)
            s = jnp.where(col <= row, s, MASK_VALUE)
        m_prev = m_ref[hh, rows, :]
        m_cur = functools.reduce(jnp.maximum, lane_blocks(s))
        m_new = jnp.maximum(m_prev, jnp.max(m_cur, axis=-1, keepdims=True))
        alpha = jnp.exp2(m_prev - m_new)
        p = jnp.exp2(s - jnp.concatenate([m_new] * nb, axis=1))
        l_ref[hh, rows, :] = alpha * l_ref[hh, rows, :] + functools.reduce(jnp.add, lane_blocks(p))
        v = v_ref[hh, pl.ds(start, t), :]
        acc_ref[hh, rows, :] = alpha * acc_ref[hh, rows, :] + _dot(p.astype(BF16), v)
        m_ref[hh, rows, :] = m_new

    def body(j, c):
        _run_phases([chain(hh, part, j, False) for hh in range(hpb) for part in range(nsplit)])
        return c

    lax.fori_loop(0, nsplit * i, body, 0)
    for d in range(nsplit):
        _run_phases([chain(hh, part, nsplit * i + d, part == d)
                     for hh in range(hpb) for part in range(d, nsplit)])
    for hh in range(hpb):
        l = jnp.sum(l_ref[hh], axis=-1, keepdims=True)
        o_ref[:, hh * VDIM:(hh + 1) * VDIM] = (acc_ref[hh] / l).astype(BF16)


def attention(q, k, v, t=512, hpb=2, nsplit=2):
    S = q.shape[1]
    tq = nsplit * t
    return pl.pallas_call(
        functools.partial(_attn_kernel, t, hpb, nsplit),
        grid=(HEADS // hpb, S // tq),
        in_specs=[pl.BlockSpec((hpb, tq, QK_PAD), lambda h, i: (h, i, 0)),
                  pl.BlockSpec((hpb, S, QK_PAD), lambda h, i: (h, 0, 0)),
                  pl.BlockSpec((hpb, S, VDIM), lambda h, i: (h, 0, 0))],
        out_specs=pl.BlockSpec((tq, hpb * VDIM), lambda h, i: (i, h)),
        out_shape=jax.ShapeDtypeStruct((S, HEADS * VDIM), BF16),
        scratch_shapes=[pltpu.VMEM((hpb, tq, LANES), F32), pltpu.VMEM((hpb, tq, LANES), F32),
                        pltpu.VMEM((hpb, tq, VDIM), F32)],
        compiler_params=_cparams(("arbitrary", "arbitrary")),
        name="attention",
    )(q, k, v)


def _hgrn_kernel(nblk, hpb, q_ref, f_ref, v_ref, gate_ref, gn_ref, o_ref, st_ref, bs2_ref):
    @pl.when(pl.program_id(1) == 0)
    def _():
        st_ref[...] = jnp.zeros(st_ref.shape, F32)

    B = LANES
    r = lax.broadcasted_iota(jnp.int32, (B, B), 0)
    c = lax.broadcasted_iota(jnp.int32, (B, B), 1)
    same = (r // SUB) == (c // SUB)
    causal = same & (c <= r)
    incl = jnp.where(causal, 1.0, 0.0).astype(BF16)
    nsub = B // SUB
    row_sub = r // SUB
    col_sub = c // SUB
    lane = lax.broadcasted_iota(jnp.int32, (SUB, B), 1)
    ones = jnp.ones((HG_D, B), BF16)
    gn = gn_ref[...]

    def one_head(hh, rows):
        cols = slice(hh * HG_D, (hh + 1) * HG_D)
        f = f_ref[rows, cols]
        q = q_ref[rows, cols].astype(F32)
        g = jnp.log(jnp.maximum(f, MIN_FORGET))
        g0 = g.astype(BF16)
        r1 = g - g0.astype(F32)
        g1 = r1.astype(BF16)
        g2 = (r1 - g1.astype(F32)).astype(BF16)
        b3 = _dot(incl, jnp.concatenate([g0, g1, g2], axis=1))
        b = b3[:, :HG_D] + b3[:, HG_D:2 * HG_D] + b3[:, 2 * HG_D:]
        b_last = jnp.concatenate(
            [jnp.broadcast_to(b[(u + 1) * SUB - 1:(u + 1) * SUB], (SUB, HG_D)) for u in range(nsub)], axis=0)
        eb = jnp.exp(b)
        kin = jnp.maximum(1.0 - f, 0.0)
        qe = (q * eb).astype(BF16)
        kd = (kin * jnp.exp(b_last - b)).astype(BF16)
        vb = v_ref[rows, cols]
        b2 = b * LOG2E
        bs2_ref[hh] = b2 - jnp.log2(kin)
        yield
        ps = []
        for u in range(nsub):
            lo = u * SUB
            q16 = q[lo:lo + SUB]
            b16 = b2[lo:lo + SUB]
            for s in range(SUB):
                e = jnp.exp2(b16 - bs2_ref[hh, pl.ds(lo + s, 1), :])
                ps.append((q16 * e).astype(BF16))
        rs = _dot(jnp.concatenate(ps, axis=0), ones)
        yield
        a_rows = []
        for u in range(nsub):
            a = jnp.zeros((SUB, B), F32)
            for s in range(SUB):
                rr = u * SUB + s
                a = jnp.where(lane == rr, rs[rr * SUB:(rr + 1) * SUB], a)
            a_rows.append(a)
        a_blk = jnp.where(causal, jnp.concatenate(a_rows, axis=0), 0.0).astype(BF16)

        vt = jnp.transpose(vb)
        vt_u = jnp.concatenate([jnp.where(col_sub == u, vt, 0.0) for u in range(nsub)], axis=0)
        upd = _dot(vt_u, kd)
        yield
        st = st_ref[hh]
        sts = []
        for u in range(nsub):
            sts.append(st.astype(BF16))
            st = st * eb[(u + 1) * SUB - 1:(u + 1) * SUB] + upd[u * B:(u + 1) * B]
        st_ref[hh] = st
        qe_u = jnp.concatenate([jnp.where(row_sub == u, qe, 0.0) for u in range(nsub)], axis=1)
        o_int = lax.dot_general(qe_u, jnp.concatenate(sts, axis=1), (((1,), (1,)), ((), ())),
                                preferred_element_type=F32)
        o = _dot(a_blk, vb) + o_int
        o = _rms(o, gn) * gate_ref[rows, cols].astype(F32)
        o_ref[rows, cols] = o.astype(BF16)

    def blk_body(bi, carry):
        rows = pl.ds(pl.multiple_of(bi * B, B), B)
        _run_phases([one_head(hh, rows) for hh in range(hpb)])
        return carry

    lax.fori_loop(0, nblk, blk_body, 0)


def hgrn(layer, q, f, v, gate, gn, ts=1024, hpb=4):
    S = q.shape[0]
    w = hpb * HG_D
    blk = pl.BlockSpec((ts, w), lambda h, i: (i, h))
    return pl.pallas_call(
        functools.partial(_hgrn_kernel, ts // LANES, hpb),
        grid=(HEADS // hpb, S // ts),
        in_specs=[blk, blk, blk, blk, pl.BlockSpec((None, 1, HG_D), lambda h, i: (layer, 0, 0))],
        out_specs=blk,
        out_shape=jax.ShapeDtypeStruct((S, D_MODEL), BF16),
        scratch_shapes=[pltpu.VMEM((hpb, HG_D, HG_D), F32), pltpu.VMEM((hpb, LANES, HG_D), F32)],
        compiler_params=_cparams(("arbitrary", "arbitrary")),
        name="hgrn",
    )(q, f, v, gate, gn)


def _merge_kernel(a_ref, r_ref, ga_ref, gb_ref, wa_ref, wb_ref, wo_ref, x_ref, o_ref):
    @pl.when(pl.program_id(1) == 0)
    def _():
        o_ref[...] = x_ref[...]

    ma = _dot(a_ref[...], wa_ref[...])
    mb = _dot(r_ref[...], wb_ref[...])
    mg = (ga_ref[...].astype(F32) * ma + gb_ref[...].astype(F32) * mb).astype(BF16)
    o_ref[...] += _dot(mg, wo_ref[...])


def merge(layer, attn, rec, ga, gb, wa, wb, wo, x, tm=1024, tn=512):
    S = x.shape[0]
    row_once = dict(pipeline_mode=pl.Buffered(1))
    return pl.pallas_call(
        _merge_kernel,
        grid=(S // tm, D_MODEL // tn),
        in_specs=[pl.BlockSpec((tm, D_MODEL), lambda i, j: (i, 0)),
                  pl.BlockSpec((tm, D_MODEL), lambda i, j: (i, 0)),
                  pl.BlockSpec((tm, tn), lambda i, j: (i, j)),
                  pl.BlockSpec((tm, tn), lambda i, j: (i, j)),
                  pl.BlockSpec((None, D_MODEL, tn), lambda i, j: (layer, 0, j)),
                  pl.BlockSpec((None, D_MODEL, tn), lambda i, j: (layer, 0, j)),
                  pl.BlockSpec((None, tn, D_MODEL), lambda i, j: (layer, j, 0)),
                  pl.BlockSpec((tm, D_MODEL), lambda i, j: (i, 0), **row_once)],
        out_specs=pl.BlockSpec((tm, D_MODEL), lambda i, j: (i, 0), **row_once),
        out_shape=jax.ShapeDtypeStruct((S, D_MODEL), F32),
        compiler_params=_cparams(("arbitrary", "arbitrary")),
        name="merge",
    )(attn, rec, ga, gb, wa, wb, wo, x)


def _ffn_kernel(final, nj, tm, x_ref, xh_ref, g_ref, wg_ref, wu_ref, cwg_ref, cwu_ref, cbg_ref, cbu_ref,
                wd_ref, fg_ref, o_ref, h_ref, ug_ref, uu_ref):
    m = pl.program_id(0)
    j = pl.program_id(1)

    @pl.when(j == 0)
    def _():
        g = g_ref[...]
        x = x_ref[...]
        o_ref[...] = x
        h_ref[HALO:, :] = _rms(x, g).astype(BF16)
        hh = jnp.where(m > 0, _rms(xh_ref[...], g), 0.0)
        h_ref[:HALO, :] = hh.astype(BF16)

    h = h_ref[...]
    ug_ref[...] = _dot(h, wg_ref[...])
    uu_ref[...] = _dot(h, wu_ref[...])

    def conv(u_ref, cw_ref, cb_ref):
        y = cb_ref[...]
        for tap in range(3):
            y = y + cw_ref[tap:tap + 1, :] * u_ref[pl.ds(HALO - 2 + tap, tm), :]
        return y

    yg = conv(ug_ref, cwg_ref, cbg_ref)
    yu = conv(uu_ref, cwu_ref, cbu_ref)
    act = (yg * jax.nn.sigmoid(yg) * yu).astype(BF16)
    o_ref[...] += _dot(act, wd_ref[...])

    if final:
        @pl.when(j == nj - 1)
        def _():
            o_ref[...] = _rms(o_ref[...], fg_ref[...])


def ffn(layer, x, g, w_up, conv_w, conv_b, w_down, fg, tm=1024, tn=512):
    S = x.shape[0]
    nj = D_FF // tn
    hb = tm // HALO
    final = layer == DEPTH - 1
    row_once = dict(pipeline_mode=pl.Buffered(1))
    return pl.pallas_call(
        functools.partial(_ffn_kernel, final, nj, tm),
        grid=(S // tm, nj),
        in_specs=[pl.BlockSpec((tm, D_MODEL), lambda i, j: (i, 0), **row_once),
                  pl.BlockSpec((HALO, D_MODEL), lambda i, j: (jnp.maximum(i * hb - 1, 0), 0)),
                  pl.BlockSpec((None, 1, D_MODEL), lambda i, j: (layer, 0, 0)),
                  pl.BlockSpec((None, D_MODEL, tn), lambda i, j: (layer, 0, j)),
                  pl.BlockSpec((None, D_MODEL, tn), lambda i, j: (layer, 0, nj + j)),
                  pl.BlockSpec((None, 3, tn), lambda i, j: (layer, 0, j)),
                  pl.BlockSpec((None, 3, tn), lambda i, j: (layer, 0, nj + j)),
                  pl.BlockSpec((None, 1, tn), lambda i, j: (layer, 0, j)),
                  pl.BlockSpec((None, 1, tn), lambda i, j: (layer, 0, nj + j)),
                  pl.BlockSpec((None, tn, D_MODEL), lambda i, j: (layer, j, 0)),
                  pl.BlockSpec((1, D_MODEL), lambda i, j: (0, 0))],
        out_specs=pl.BlockSpec((tm, D_MODEL), lambda i, j: (i, 0), **row_once),
        out_shape=jax.ShapeDtypeStruct((S, D_MODEL), F32),
        scratch_shapes=[pltpu.VMEM((tm + HALO, D_MODEL), BF16),
                        pltpu.VMEM((tm + HALO, tn), F32),
                        pltpu.VMEM((tm + HALO, tn), F32)],
        compiler_params=_cparams(("arbitrary", "arbitrary")),
        name="ffn",
    )(x, x, g, w_up, w_up, conv_w, conv_w, conv_b, conv_b, w_down, fg)


def _rope_pair_cols(w):
    x1, x2 = w[..., :ROPE // 2], w[..., ROPE // 2:]
    z = jnp.zeros(w.shape[:-1] + (LANES - ROPE,), w.dtype)
    return jnp.concatenate([x1, x2, z], axis=-1), jnp.concatenate([x2, x1, z], axis=-1)


def kernel(x, positions, attn_norm_g, w_in, q_norm_g, w_uq, kv_norm_g, w_ukv, hgrn_lb_logits, hgrn_out_norm_g,
           w_branch_a, w_branch_b, w_out, ffn_norm_g, w_up, conv_w, conv_b, w_down, final_norm_g):
    B, S, D = x.shape
    assert B == 1 and D == D_MODEL
    xs = x.reshape(S, D)

    inv_freq = ROPE_THETA ** (-jnp.arange(0, ROPE, 2, dtype=F32) / ROPE)
    zeros = jnp.zeros((LANES - ROPE,), F32)
    invf = jnp.concatenate([inv_freq, inv_freq, zeros]).reshape(1, LANES)
    half = jnp.ones((ROPE // 2,), F32)
    sgn = jnp.concatenate([-half, half, zeros]).reshape(1, LANES)
    cosT, sinT = rope_tables(positions.reshape(S, 1), invf, sgn)

    n_lat = Q_LORA + KV_LORA
    ka, kb = _rope_pair_cols(w_in[:, :, n_lat:n_lat + ROPE])
    w1 = jnp.concatenate([w_in[:, :, :n_lat], ka, kb], axis=-1).astype(BF16)
    w2 = w_in[:, :, n_lat + ROPE:].astype(BF16)
    wq = w_uq.reshape(DEPTH, Q_LORA, HEADS, NOPE + ROPE)
    qa, qb = _rope_pair_cols(wq[..., NOPE:])
    wq = jnp.concatenate([wq[..., :NOPE], qa, qb], axis=-1).reshape(DEPTH, Q_LORA, -1).astype(BF16)
    wkv = w_ukv.astype(BF16)
    wa, wb, wo = w_branch_a.astype(BF16), w_branch_b.astype(BF16), w_out.astype(BF16)
    wup, wdn = w_up.astype(BF16), w_down.astype(BF16)
    conv_b3 = conv_b.reshape(DEPTH, 1, 2 * D_FF)
    lb_logits = hgrn_lb_logits.astype(F32)
    fg = final_norm_g.reshape(1, D)

    rows3 = lambda a: a.reshape(DEPTH, 1, a.shape[-1])
    g_attn, g_q, g_kv, g_hg, g_ffn = map(rows3, (attn_norm_g, q_norm_g, kv_norm_g, hgrn_out_norm_g, ffn_norm_g))

    for l in range(DEPTH):
        cq, ckv, kpe = in1(l, xs, g_attn, w1, g_q, g_kv, cosT, sinT)
        hq, hf, hi, hg, ga, gb = in2(l, xs, g_attn, w2, lb_logits)
        q, k, v = qkv(l, cq, ckv, kpe, cosT, sinT, wq, wkv)
        attn = attention(q, k, v)
        rec = hgrn(l, hq, hf, hi, hg, g_hg)
        xs = merge(l, attn, rec, ga, gb, wa, wb, wo, xs)
        xs = ffn(l, xs, g_ffn, wup, conv_w, conv_b3, wdn, fg)
    return xs.reshape(B, S, D)
```

```python
import functools

import jax
import jax.numpy as jnp
from jax import lax
from jax.experimental import pallas as pl
from jax.experimental.pallas import tpu as pltpu

F32 = jnp.float32
BF16 = jnp.bfloat16

D_MODEL = 2048
DEPTH = 4
HEADS = 16
Q_LORA = 512
KV_LORA = 512
NOPE = 128
ROPE = 64
VDIM = 128
ROPE_THETA = 10000.0
MASK_VALUE = -1e30
HG_D = 128
SUB = 16
MIN_FORGET = 1e-30
D_FF = 5632
EPS = 1e-6
LOG2E = 1.4426950408889634
LANES = 128
QK_PAD = 256
HALO = 16

VMEM_LIMIT = 52 * 1024 * 1024


def _cparams(sem):
    return pltpu.CompilerParams(dimension_semantics=sem, vmem_limit_bytes=VMEM_LIMIT)


def _rms(x, g):
    return x * lax.rsqrt(jnp.mean(x * x, axis=-1, keepdims=True) + EPS) * g


def _dot(a, b):
    return jnp.dot(a, b, preferred_element_type=F32)


def _rope_kernel(pos_ref, invf_ref, sgn_ref, cos_ref, sin_ref):
    ang = pos_ref[...].astype(F32) * invf_ref[...]
    sgn = sgn_ref[...]
    cos_ref[...] = jnp.cos(ang) * jnp.abs(sgn)
    sin_ref[...] = jnp.sin(ang) * sgn


def rope_tables(pos_col, invf, sgn, tm=1024):
    S = pos_col.shape[0]
    return pl.pallas_call(
        _rope_kernel,
        grid=(S // tm,),
        in_specs=[pl.BlockSpec((tm, 1), lambda i: (i, 0)),
                  pl.BlockSpec((1, LANES), lambda i: (0, 0)),
                  pl.BlockSpec((1, LANES), lambda i: (0, 0))],
        out_specs=[pl.BlockSpec((tm, LANES), lambda i: (i, 0))] * 2,
        out_shape=[jax.ShapeDtypeStruct((S, LANES), F32)] * 2,
        compiler_params=_cparams(("arbitrary",)),
        name="rope_tables",
    )(pos_col, invf, sgn)


def _in1_kernel(x_ref, g_ref, w_ref, qg_ref, kvg_ref, cos_ref, sin_ref, cq_ref, ckv_ref, kpe_ref):
    h = _rms(x_ref[...], g_ref[...]).astype(BF16)
    y = _dot(h, w_ref[...])
    cq_ref[...] = _rms(y[:, :Q_LORA], qg_ref[...]).astype(BF16)
    ckv_ref[...] = _rms(y[:, Q_LORA:Q_LORA + KV_LORA], kvg_ref[...]).astype(BF16)
    o = Q_LORA + KV_LORA
    kpe = y[:, o:o + LANES] * cos_ref[...] + y[:, o + LANES:o + 2 * LANES] * sin_ref[...]
    kpe_ref[...] = kpe.astype(BF16)


def in1(layer, x, g, w1, qg, kvg, cosT, sinT, tm=512):
    S = x.shape[0]
    n1 = w1.shape[2]
    row = lambda i: (i, 0)
    lyr = lambda i: (layer, 0, 0)
    return pl.pallas_call(
        _in1_kernel,
        grid=(S // tm,),
        in_specs=[pl.BlockSpec((tm, D_MODEL), row), pl.BlockSpec((None, 1, D_MODEL), lyr),
                  pl.BlockSpec((None, D_MODEL, n1), lyr), pl.BlockSpec((None, 1, Q_LORA), lyr),
                  pl.BlockSpec((None, 1, KV_LORA), lyr), pl.BlockSpec((tm, LANES), row),
                  pl.BlockSpec((tm, LANES), row)],
        out_specs=[pl.BlockSpec((tm, Q_LORA), row), pl.BlockSpec((tm, KV_LORA), row),
                   pl.BlockSpec((tm, LANES), row)],
        out_shape=[jax.ShapeDtypeStruct((S, Q_LORA), BF16), jax.ShapeDtypeStruct((S, KV_LORA), BF16),
                   jax.ShapeDtypeStruct((S, LANES), BF16)],
        compiler_params=_cparams(("arbitrary",)),
        name="in1",
    )(x, g, w1, qg, kvg, cosT, sinT)


def _in2_kernel(layer, x_ref, g_ref, wq_ref, wf_ref, wi_ref, wg_ref, wa_ref, wb_ref, lbl_ref,
                oq_ref, of_ref, oi_ref, og_ref, oa_ref, ob_ref, h_ref):
    @pl.when(pl.program_id(1) == 0)
    def _():
        h_ref[...] = _rms(x_ref[...], g_ref[...]).astype(BF16)

    h = h_ref[...]
    y = _dot(h, wq_ref[...])
    oq_ref[...] = (y * jax.nn.sigmoid(y)).astype(BF16)
    y = _dot(h, wf_ref[...])
    lg = lbl_ref[...]
    e = jnp.exp(lg - jnp.max(lg, axis=0, keepdims=True))
    p = e / jnp.sum(e, axis=0, keepdims=True)
    lb = jnp.sum(p[:layer + 1], axis=0, keepdims=True) - p[0:1]
    of_ref[...] = lb + (1.0 - lb) * jax.nn.sigmoid(y)
    y = _dot(h, wi_ref[...])
    oi_ref[...] = y.astype(BF16)
    y = _dot(h, wg_ref[...])
    og_ref[...] = (y * jax.nn.sigmoid(y)).astype(BF16)
    y = _dot(h, wa_ref[...])
    oa_ref[...] = jax.nn.sigmoid(y).astype(BF16)
    y = _dot(h, wb_ref[...])
    ob_ref[...] = jax.nn.sigmoid(y).astype(BF16)


def in2(layer, x, g, w2, lb_logits, tm=1024, tn=256):
    S = x.shape[0]
    nseg = w2.shape[2] // D_MODEL
    bps = D_MODEL // tn

    def seg_cols(s, i, j):
        return layer, 0, s * bps + j

    w_specs = [pl.BlockSpec((None, D_MODEL, tn), functools.partial(seg_cols, s)) for s in range(nseg)]
    out_dtypes = [BF16, F32, BF16, BF16, BF16, BF16]
    return pl.pallas_call(
        functools.partial(_in2_kernel, layer),
        grid=(S // tm, bps),
        in_specs=[pl.BlockSpec((tm, D_MODEL), lambda i, j: (i, 0)),
                  pl.BlockSpec((None, 1, D_MODEL), lambda i, j: (layer, 0, 0))] + w_specs +
                 [pl.BlockSpec((DEPTH, tn), lambda i, j: (0, j))],
        out_specs=[pl.BlockSpec((tm, tn), lambda i, j: (i, j))] * nseg,
        out_shape=[jax.ShapeDtypeStruct((S, D_MODEL), dt) for dt in out_dtypes],
        scratch_shapes=[pltpu.VMEM((tm, D_MODEL), BF16)],
        compiler_params=_cparams(("arbitrary", "arbitrary")),
        name="in2",
    )(x, g, *([w2] * nseg), lb_logits)


def _qkv_kernel(scale, cq_ref, ckv_ref, kpe_ref, cos_ref, sin_ref, wq_ref, wkv_ref, q_ref, k_ref, v_ref):
    cq = cq_ref[...]
    ckv = ckv_ref[...]
    cos = cos_ref[...]
    sin = sin_ref[...]
    kpe = kpe_ref[...]
    qw = NOPE + 2 * LANES
    kvw = NOPE + VDIM
    for h in range(HEADS):
        y = _dot(cq, wq_ref[:, h * qw:(h + 1) * qw])
        pe = y[:, NOPE:NOPE + LANES] * cos + y[:, NOPE + LANES:] * sin
        q_ref[h, :, :NOPE] = (y[:, :NOPE] * scale).astype(BF16)
        q_ref[h, :, NOPE:] = (pe * scale).astype(BF16)
        z = _dot(ckv, wkv_ref[:, h * kvw:(h + 1) * kvw])
        k_ref[h, :, :NOPE] = z[:, :NOPE].astype(BF16)
        k_ref[h, :, NOPE:] = kpe
        v_ref[h] = z[:, NOPE:].astype(BF16)


def qkv(layer, cq, ckv, kpe, cosT, sinT, wq, wkv, tm=512):
    S = cq.shape[0]
    scale = float((NOPE + ROPE) ** -0.5 * LOG2E)
    row = lambda i: (i, 0)
    return pl.pallas_call(
        functools.partial(_qkv_kernel, scale),
        grid=(S // tm,),
        in_specs=[pl.BlockSpec((tm, Q_LORA), row), pl.BlockSpec((tm, KV_LORA), row),
                  pl.BlockSpec((tm, LANES), row), pl.BlockSpec((tm, LANES), row),
                  pl.BlockSpec((tm, LANES), row),
                  pl.BlockSpec((None,) + wq.shape[1:], lambda i: (layer, 0, 0)),
                  pl.BlockSpec((None,) + wkv.shape[1:], lambda i: (layer, 0, 0))],
        out_specs=[pl.BlockSpec((HEADS, tm, QK_PAD), lambda i: (0, i, 0)),
                   pl.BlockSpec((HEADS, tm, QK_PAD), lambda i: (0, i, 0)),
                   pl.BlockSpec((HEADS, tm, VDIM), lambda i: (0, i, 0))],
        out_shape=[jax.ShapeDtypeStruct((HEADS, S, QK_PAD), BF16),
                   jax.ShapeDtypeStruct((HEADS, S, QK_PAD), BF16),
                   jax.ShapeDtypeStruct((HEADS, S, VDIM), BF16)],
        compiler_params=_cparams(("arbitrary",)),
        name="qkv",
    )(cq, ckv, kpe, cosT, sinT, wq, wkv)


def _run_phases(gens):
    live = list(gens)
    while live:
        live = [g for g in live if next(g, live) is not live]


def _attn_kernel(t, hpb, nsplit, q_ref, k_ref, v_ref, o_ref, m_ref, l_ref, acc_ref):
    i = pl.program_id(1)
    nb = t // LANES
    m_ref[...] = jnp.full(m_ref.shape, -jnp.inf, F32)
    l_ref[...] = jnp.zeros(l_ref.shape, F32)
    acc_ref[...] = jnp.zeros(acc_ref.shape, F32)

    def lane_blocks(x):
        return [x[:, b * LANES:(b + 1) * LANES] for b in range(nb)]

    def chain(hh, part, j, masked):
        start = pl.multiple_of(j * t, t)
        rows = slice(part * t, (part + 1) * t)
        k = k_ref[hh, pl.ds(start, t), :]
        s = lax.dot_general(q_ref[hh, rows, :], k, (((1,), (1,)), ((), ())), preferred_element_type=F32)
        yield
        if masked:
            row = lax.broadcasted_iota(jnp.int32, (t, t), 0)
            col = lax.broadcasted_iota(jnp.int32, (t, t), 1)
            s = jnp.where(col <= row, s, MASK_VALUE)
        m_prev = m_ref[hh, rows, :]
        m_cur = functools.reduce(jnp.maximum, lane_blocks(s))
        m_new = jnp.maximum(m_prev, jnp.max(m_cur, axis=-1, keepdims=True))
        alpha = jnp.exp2(m_prev - m_new)
        p = jnp.exp2(s - jnp.concatenate([m_new] * nb, axis=1))
        l_ref[hh, rows, :] = alpha * l_ref[hh, rows, :] + functools.reduce(jnp.add, lane_blocks(p))
        v = v_ref[hh, pl.ds(start, t), :]
        acc_ref[hh, rows, :] = alpha * acc_ref[hh, rows, :] + _dot(p.astype(BF16), v)
        m_ref[hh, rows, :] = m_new

    def body(j, c):
        _run_phases([chain(hh, part, j, False) for hh in range(hpb) for part in range(nsplit)])
        return c

    lax.fori_loop(0, nsplit * i, body, 0)
    for d in range(nsplit):
        _run_phases([chain(hh, part, nsplit * i + d, part == d)
                     for hh in range(hpb) for part in range(d, nsplit)])
    for hh in range(hpb):
        l = jnp.sum(l_ref[hh], axis=-1, keepdims=True)
        o_ref[:, hh * VDIM:(hh + 1) * VDIM] = (acc_ref[hh] / l).astype(BF16)


def attention(q, k, v, t=512, hpb=2, nsplit=2):
    S = q.shape[1]
    tq = nsplit * t
    return pl.pallas_call(
        functools.partial(_attn_kernel, t, hpb, nsplit),
        grid=(HEADS // hpb, S // tq),
        in_specs=[pl.BlockSpec((hpb, tq, QK_PAD), lambda h, i: (h, i, 0)),
                  pl.BlockSpec((hpb, S, QK_PAD), lambda h, i: (h, 0, 0)),
                  pl.BlockSpec((hpb, S, VDIM), lambda h, i: (h, 0, 0))],
        out_specs=pl.BlockSpec((tq, hpb * VDIM), lambda h, i: (i, h)),
        out_shape=jax.ShapeDtypeStruct((S, HEADS * VDIM), BF16),
        scratch_shapes=[pltpu.VMEM((hpb, tq, LANES), F32), pltpu.VMEM((hpb, tq, LANES), F32),
                        pltpu.VMEM((hpb, tq, VDIM), F32)],
        compiler_params=_cparams(("arbitrary", "arbitrary")),
        name="attention",
    )(q, k, v)


def _hgrn_kernel(nblk, hpb, q_ref, f_ref, v_ref, gate_ref, gn_ref, o_ref, st_ref, bs2_ref):
    @pl.when(pl.program_id(1) == 0)
    def _():
        st_ref[...] = jnp.zeros(st_ref.shape, F32)

    B = LANES
    r = lax.broadcasted_iota(jnp.int32, (B, B), 0)
    c = lax.broadcasted_iota(jnp.int32, (B, B), 1)
    same = (r // SUB) == (c // SUB)
    causal = same & (c <= r)
    incl = jnp.where(causal, 1.0, 0.0).astype(BF16)
    nsub = B // SUB
    row_sub = r // SUB
    col_sub = c // SUB
    lane = lax.broadcasted_iota(jnp.int32, (SUB, B), 1)
    ones = jnp.ones((HG_D, B), BF16)
    gn = gn_ref[...]

    def one_head(hh, rows):
        cols = slice(hh * HG_D, (hh + 1) * HG_D)
        f = f_ref[rows, cols]
        q = q_ref[rows, cols].astype(F32)
        g = jnp.log(jnp.maximum(f, MIN_FORGET))
        g0 = g.astype(BF16)
        r1 = g - g0.astype(F32)
        g1 = r1.astype(BF16)
        g2 = (r1 - g1.astype(F32)).astype(BF16)
        b3 = _dot(incl, jnp.concatenate([g0, g1, g2], axis=1))
        b = b3[:, :HG_D] + b3[:, HG_D:2 * HG_D] + b3[:, 2 * HG_D:]
        b_last = jnp.concatenate(
            [jnp.broadcast_to(b[(u + 1) * SUB - 1:(u + 1) * SUB], (SUB, HG_D)) for u in range(nsub)], axis=0)
        eb = jnp.exp(b)
        kin = jnp.maximum(1.0 - f, 0.0)
        qe = (q * eb).astype(BF16)
        kd = (kin * jnp.exp(b_last - b)).astype(BF16)
        vb = v_ref[rows, cols]
        b2 = b * LOG2E
        bs2_ref[hh] = b2 - jnp.log2(kin)
        yield
        ps = []
        sums = {}
        for u in range(nsub):
            lo = u * SUB
            q16 = q[lo:lo + SUB]
            b16 = b2[lo:lo + SUB]
            for s in range(SUB):
                p = q16 * jnp.exp2(b16 - bs2_ref[hh, pl.ds(lo + s, 1), :])
                if s % 2 == 0:
                    ps.append(p.astype(BF16))
                else:
                    sums[lo + s] = jnp.sum(p, axis=-1, keepdims=True)
        rs = _dot(jnp.concatenate(ps, axis=0), ones)
        yield
        for n, rr in enumerate(range(0, B, 2)):
            sums[rr] = rs[n * SUB:(n + 1) * SUB]
        a_rows = []
        for u in range(nsub):
            a = jnp.zeros((SUB, B), F32)
            for s in range(SUB):
                rr = u * SUB + s
                a = jnp.where(lane == rr, sums[rr], a)
            a_rows.append(a)
        a_blk = jnp.where(causal, jnp.concatenate(a_rows, axis=0), 0.0).astype(BF16)

        vt = jnp.transpose(vb)
        vt_u = jnp.concatenate([jnp.where(col_sub == u, vt, 0.0) for u in range(nsub)], axis=0)
        upd = _dot(vt_u, kd)
        yield
        st = st_ref[hh]
        sts = []
        for u in range(nsub):
            sts.append(st.astype(BF16))
            st = st * eb[(u + 1) * SUB - 1:(u + 1) * SUB] + upd[u * B:(u + 1) * B]
        st_ref[hh] = st
        qe_u = jnp.concatenate([jnp.where(row_sub == u, qe, 0.0) for u in range(nsub)], axis=1)
        o_int = lax.dot_general(qe_u, jnp.concatenate(sts, axis=1), (((1,), (1,)), ((), ())),
                                preferred_element_type=F32)
        o = _dot(a_blk, vb) + o_int
        o = _rms(o, gn) * gate_ref[rows, cols].astype(F32)
        o_ref[rows, cols] = o.astype(BF16)

    def blk_body(bi, carry):
        rows = pl.ds(pl.multiple_of(bi * B, B), B)
        _run_phases([one_head(hh, rows) for hh in range(hpb)])
        return carry

    lax.fori_loop(0, nblk, blk_body, 0)


def hgrn(layer, q, f, v, gate, gn, ts=1024, hpb=4):
    S = q.shape[0]
    w = hpb * HG_D
    blk = pl.BlockSpec((ts, w), lambda h, i: (i, h))
    return pl.pallas_call(
        functools.partial(_hgrn_kernel, ts // LANES, hpb),
        grid=(HEADS // hpb, S // ts),
        in_specs=[blk, blk, blk, blk, pl.BlockSpec((None, 1, HG_D), lambda h, i: (layer, 0, 0))],
        out_specs=blk,
        out_shape=jax.ShapeDtypeStruct((S, D_MODEL), BF16),
        scratch_shapes=[pltpu.VMEM((hpb, HG_D, HG_D), F32), pltpu.VMEM((hpb, LANES, HG_D), F32)],
        compiler_params=_cparams(("arbitrary", "arbitrary")),
        name="hgrn",
    )(q, f, v, gate, gn)


def _merge_kernel(a_ref, r_ref, ga_ref, gb_ref, wa_ref, wb_ref, wo_ref, x_ref, o_ref):
    @pl.when(pl.program_id(1) == 0)
    def _():
        o_ref[...] = x_ref[...]

    ma = _dot(a_ref[...], wa_ref[...])
    mb = _dot(r_ref[...], wb_ref[...])
    mg = (ga_ref[...].astype(F32) * ma + gb_ref[...].astype(F32) * mb).astype(BF16)
    o_ref[...] += _dot(mg, wo_ref[...])


def merge(layer, attn, rec, ga, gb, wa, wb, wo, x, tm=1024, tn=512):
    S = x.shape[0]
    row_once = dict(pipeline_mode=pl.Buffered(1))
    return pl.pallas_call(
        _merge_kernel,
        grid=(S // tm, D_MODEL // tn),
        in_specs=[pl.BlockSpec((tm, D_MODEL), lambda i, j: (i, 0)),
                  pl.BlockSpec((tm, D_MODEL), lambda i, j: (i, 0)),
                  pl.BlockSpec((tm, tn), lambda i, j: (i, j)),
                  pl.BlockSpec((tm, tn), lambda i, j: (i, j)),
                  pl.BlockSpec((None, D_MODEL, tn), lambda i, j: (layer, 0, j)),
                  pl.BlockSpec((None, D_MODEL, tn), lambda i, j: (layer, 0, j)),
                  pl.BlockSpec((None, tn, D_MODEL), lambda i, j: (layer, j, 0)),
                  pl.BlockSpec((tm, D_MODEL), lambda i, j: (i, 0), **row_once)],
        out_specs=pl.BlockSpec((tm, D_MODEL), lambda i, j: (i, 0), **row_once),
        out_shape=jax.ShapeDtypeStruct((S, D_MODEL), F32),
        compiler_params=_cparams(("arbitrary", "arbitrary")),
        name="merge",
    )(attn, rec, ga, gb, wa, wb, wo, x)


def _ffn_kernel(final, nj, tm, x_ref, xh_ref, g_ref, wg_ref, wu_ref, cwg_ref, cwu_ref, cbg_ref, cbu_ref,
                wd_ref, fg_ref, o_ref, h_ref, ug_ref, uu_ref):
    m = pl.program_id(0)
    j = pl.program_id(1)

    @pl.when(j == 0)
    def _():
        g = g_ref[...]
        x = x_ref[...]
        o_ref[...] = x
        h_ref[HALO:, :] = _rms(x, g).astype(BF16)
        hh = jnp.where(m > 0, _rms(xh_ref[...], g), 0.0)
        h_ref[:HALO, :] = hh.astype(BF16)

    h = h_ref[...]
    ug_ref[...] = _dot(h, wg_ref[...])
    uu_ref[...] = _dot(h, wu_ref[...])

    def conv(u_ref, cw_ref, cb_ref):
        y = cb_ref[...]
        for tap in range(3):
            y = y + cw_ref[tap:tap + 1, :] * u_ref[pl.ds(HALO - 2 + tap, tm), :]
        return y

    yg = conv(ug_ref, cwg_ref, cbg_ref)
    yu = conv(uu_ref, cwu_ref, cbu_ref)
    act = (yg * jax.nn.sigmoid(yg) * yu).astype(BF16)
    o_ref[...] += _dot(act, wd_ref[...])

    if final:
        @pl.when(j == nj - 1)
        def _():
            o_ref[...] = _rms(o_ref[...], fg_ref[...])


def ffn(layer, x, g, w_up, conv_w, conv_b, w_down, fg, tm=1024, tn=512):
    S = x.shape[0]
    nj = D_FF // tn
    hb = tm // HALO
    final = layer == DEPTH - 1
    row_once = dict(pipeline_mode=pl.Buffered(1))
    return pl.pallas_call(
        functools.partial(_ffn_kernel, final, nj, tm),
        grid=(S // tm, nj),
        in_specs=[pl.BlockSpec((tm, D_MODEL), lambda i, j: (i, 0), **row_once),
                  pl.BlockSpec((HALO, D_MODEL), lambda i, j: (jnp.maximum(i * hb - 1, 0), 0)),
                  pl.BlockSpec((None, 1, D_MODEL), lambda i, j: (layer, 0, 0)),
                  pl.BlockSpec((None, D_MODEL, tn), lambda i, j: (layer, 0, j)),
                  pl.BlockSpec((None, D_MODEL, tn), lambda i, j: (layer, 0, nj + j)),
                  pl.BlockSpec((None, 3, tn), lambda i, j: (layer, 0, j)),
                  pl.BlockSpec((None, 3, tn), lambda i, j: (layer, 0, nj + j)),
                  pl.BlockSpec((None, 1, tn), lambda i, j: (layer, 0, j)),
                  pl.BlockSpec((None, 1, tn), lambda i, j: (layer, 0, nj + j)),
                  pl.BlockSpec((None, tn, D_MODEL), lambda i, j: (layer, j, 0)),
                  pl.BlockSpec((1, D_MODEL), lambda i, j: (0, 0))],
        out_specs=pl.BlockSpec((tm, D_MODEL), lambda i, j: (i, 0), **row_once),
        out_shape=jax.ShapeDtypeStruct((S, D_MODEL), F32),
        scratch_shapes=[pltpu.VMEM((tm + HALO, D_MODEL), BF16),
                        pltpu.VMEM((tm + HALO, tn), F32),
                        pltpu.VMEM((tm + HALO, tn), F32)],
        compiler_params=_cparams(("arbitrary", "arbitrary")),
        name="ffn",
    )(x, x, g, w_up, w_up, conv_w, conv_w, conv_b, conv_b, w_down, fg)


def _w2_kernel(a_ref, b_ref, o_ref):
    sh = LANES // 2
    o_ref[...] = jnp.concatenate([a_ref[:, sh:], b_ref[:, :sh]], axis=1).astype(BF16)


def cast_w2(w_in, off, tr=1024, tn=1024):
    depth, rows, width = w_in.shape
    assert off % LANES == LANES // 2 and (off - LANES // 2) % tn == 0
    n = width - off
    a0 = (off - LANES // 2) // tn
    b0 = (off - LANES // 2) // LANES
    return pl.pallas_call(
        _w2_kernel,
        grid=(depth, rows // tr, n // tn),
        in_specs=[pl.BlockSpec((None, tr, tn), lambda l, i, j: (l, i, a0 + j)),
                  pl.BlockSpec((None, tr, LANES), lambda l, i, j: (l, i, b0 + (j + 1) * (tn // LANES)))],
        out_specs=pl.BlockSpec((None, tr, tn), lambda l, i, j: (l, i, j)),
        out_shape=jax.ShapeDtypeStruct((depth, rows, n), BF16),
        compiler_params=_cparams(("arbitrary", "arbitrary", "arbitrary")),
        name="cast_w2",
    )(w_in, w_in)


def _rope_pair_cols(w):
    x1, x2 = w[..., :ROPE // 2], w[..., ROPE // 2:]
    z = jnp.zeros(w.shape[:-1] + (LANES - ROPE,), w.dtype)
    return jnp.concatenate([x1, x2, z], axis=-1), jnp.concatenate([x2, x1, z], axis=-1)


def kernel(x, positions, attn_norm_g, w_in, q_norm_g, w_uq, kv_norm_g, w_ukv, hgrn_lb_logits, hgrn_out_norm_g,
           w_branch_a, w_branch_b, w_out, ffn_norm_g, w_up, conv_w, conv_b, w_down, final_norm_g):
    B, S, D = x.shape
    assert B == 1 and D == D_MODEL
    xs = x.reshape(S, D)

    inv_freq = ROPE_THETA ** (-jnp.arange(0, ROPE, 2, dtype=F32) / ROPE)
    zeros = jnp.zeros((LANES - ROPE,), F32)
    invf = jnp.concatenate([inv_freq, inv_freq, zeros]).reshape(1, LANES)
    half = jnp.ones((ROPE // 2,), F32)
    sgn = jnp.concatenate([-half, half, zeros]).reshape(1, LANES)
    cosT, sinT = rope_tables(positions.reshape(S, 1), invf, sgn)

    n_lat = Q_LORA + KV_LORA
    ka, kb = _rope_pair_cols(w_in[:, :, n_lat:n_lat + ROPE])
    w1 = jnp.concatenate([w_in[:, :, :n_lat], ka, kb], axis=-1).astype(BF16)
    w2 = cast_w2(w_in, n_lat + ROPE)
    wq = w_uq.reshape(DEPTH, Q_LORA, HEADS, NOPE + ROPE)
    qa, qb = _rope_pair_cols(wq[..., NOPE:])
    wq = jnp.concatenate([wq[..., :NOPE], qa, qb], axis=-1).reshape(DEPTH, Q_LORA, -1).astype(BF16)
    wkv = w_ukv.astype(BF16)
    wa, wb, wo = w_branch_a.astype(BF16), w_branch_b.astype(BF16), w_out.astype(BF16)
    wup, wdn = w_up.astype(BF16), w_down.astype(BF16)
    conv_b3 = conv_b.reshape(DEPTH, 1, 2 * D_FF)
    lb_logits = hgrn_lb_logits.astype(F32)
    fg = final_norm_g.reshape(1, D)

    rows3 = lambda a: a.reshape(DEPTH, 1, a.shape[-1])
    g_attn, g_q, g_kv, g_hg, g_ffn = map(rows3, (attn_norm_g, q_norm_g, kv_norm_g, hgrn_out_norm_g, ffn_norm_g))

    for l in range(DEPTH):
        cq, ckv, kpe = in1(l, xs, g_attn, w1, g_q, g_kv, cosT, sinT)
        hq, hf, hi, hg, ga, gb = in2(l, xs, g_attn, w2, lb_logits)
        q, k, v = qkv(l, cq, ckv, kpe, cosT, sinT, wq, wkv)
        attn = attention(q, k, v)
        rec = hgrn(l, hq, hf, hi, hg, g_hg)
        xs = merge(l, attn, rec, ga, gb, wa, wb, wo, xs)
        xs = ffn(l, xs, g_ffn, wup, conv_w, conv_b3, wdn, fg)
    return xs.reshape(B, S, D)
```

```python
import functools

import jax
import jax.numpy as jnp
from jax import lax
from jax.experimental import pallas as pl
from jax.experimental.pallas import tpu as pltpu

F32 = jnp.float32
BF16 = jnp.bfloat16

D_MODEL = 2048
DEPTH = 4
HEADS = 16
Q_LORA = 512
KV_LORA = 512
NOPE = 128
ROPE = 64
VDIM = 128
ROPE_THETA = 10000.0
MASK_VALUE = -1e30
HG_D = 128
SUB = 16
MIN_FORGET = 1e-30
D_FF = 5632
EPS = 1e-6
LOG2E = 1.4426950408889634
LANES = 128
QK_PAD = 256
HALO = 16

VMEM_LIMIT = 52 * 1024 * 1024


def _cparams(sem):
    return pltpu.CompilerParams(dimension_semantics=sem, vmem_limit_bytes=VMEM_LIMIT)


def _rms(x, g):
    return x * lax.rsqrt(jnp.mean(x * x, axis=-1, keepdims=True) + EPS) * g


def _dot(a, b):
    return jnp.dot(a, b, preferred_element_type=F32)


def _dot_t(a, bt):
    return lax.dot_general(a, bt, (((1,), (1,)), ((), ())), preferred_element_type=F32)


def _rope_kernel(pos_ref, invf_ref, sgn_ref, cos_ref, sin_ref):
    ang = pos_ref[...].astype(F32) * invf_ref[...]
    sgn = sgn_ref[...]
    cos_ref[...] = jnp.cos(ang) * jnp.abs(sgn)
    sin_ref[...] = jnp.sin(ang) * sgn


def rope_tables(pos_col, invf, sgn, tm=1024):
    S = pos_col.shape[0]
    return pl.pallas_call(
        _rope_kernel,
        grid=(S // tm,),
        in_specs=[pl.BlockSpec((tm, 1), lambda i: (i, 0)),
                  pl.BlockSpec((1, LANES), lambda i: (0, 0)),
                  pl.BlockSpec((1, LANES), lambda i: (0, 0))],
        out_specs=[pl.BlockSpec((tm, LANES), lambda i: (i, 0))] * 2,
        out_shape=[jax.ShapeDtypeStruct((S, LANES), F32)] * 2,
        compiler_params=_cparams(("arbitrary",)),
        name="rope_tables",
    )(pos_col, invf, sgn)


def _in1_kernel(x_ref, g_ref, w_ref, qg_ref, kvg_ref, cos_ref, sin_ref, cq_ref, ckv_ref, kpe_ref):
    h = _rms(x_ref[...], g_ref[...]).astype(BF16)
    y = _dot_t(h, w_ref[...])
    cq_ref[...] = _rms(y[:, :Q_LORA], qg_ref[...]).astype(BF16)
    ckv_ref[...] = _rms(y[:, Q_LORA:Q_LORA + KV_LORA], kvg_ref[...]).astype(BF16)
    o = Q_LORA + KV_LORA
    kpe = y[:, o:o + LANES] * cos_ref[...] + y[:, o + LANES:o + 2 * LANES] * sin_ref[...]
    kpe_ref[...] = kpe.astype(BF16)


def in1(layer, x, g, w1, qg, kvg, cosT, sinT, tm=512):
    S = x.shape[0]
    n1 = w1.shape[1]
    row = lambda i: (i, 0)
    lyr = lambda i: (layer, 0, 0)
    return pl.pallas_call(
        _in1_kernel,
        grid=(S // tm,),
        in_specs=[pl.BlockSpec((tm, D_MODEL), row), pl.BlockSpec((None, 1, D_MODEL), lyr),
                  pl.BlockSpec((None, n1, D_MODEL), lyr), pl.BlockSpec((None, 1, Q_LORA), lyr),
                  pl.BlockSpec((None, 1, KV_LORA), lyr), pl.BlockSpec((tm, LANES), row),
                  pl.BlockSpec((tm, LANES), row)],
        out_specs=[pl.BlockSpec((tm, Q_LORA), row), pl.BlockSpec((tm, KV_LORA), row),
                   pl.BlockSpec((tm, LANES), row)],
        out_shape=[jax.ShapeDtypeStruct((S, Q_LORA), BF16), jax.ShapeDtypeStruct((S, KV_LORA), BF16),
                   jax.ShapeDtypeStruct((S, LANES), BF16)],
        compiler_params=_cparams(("arbitrary",)),
        name="in1",
    )(x, g, w1, qg, kvg, cosT, sinT)


def _in2_kernel(layer, x_ref, g_ref, wq_ref, wf_ref, wi_ref, wg_ref, wa_ref, wb_ref, lbl_ref,
                oq_ref, of_ref, oi_ref, og_ref, oa_ref, ob_ref, h_ref):
    @pl.when(pl.program_id(1) == 0)
    def _():
        h_ref[...] = _rms(x_ref[...], g_ref[...]).astype(BF16)

    h = h_ref[...]
    y = _dot_t(h, wq_ref[0])
    oq_ref[...] = (y * jax.nn.sigmoid(y)).astype(BF16)
    y = _dot_t(h, wf_ref[0])
    lg = lbl_ref[...]
    e = jnp.exp(lg - jnp.max(lg, axis=0, keepdims=True))
    p = e / jnp.sum(e, axis=0, keepdims=True)
    lb = jnp.sum(p[:layer + 1], axis=0, keepdims=True) - p[0:1]
    of_ref[...] = lb + (1.0 - lb) * jax.nn.sigmoid(y)
    y = _dot_t(h, wi_ref[0])
    oi_ref[...] = y.astype(BF16)
    y = _dot_t(h, wg_ref[0])
    og_ref[...] = (y * jax.nn.sigmoid(y)).astype(BF16)
    y = _dot_t(h, wa_ref[0])
    oa_ref[...] = jax.nn.sigmoid(y).astype(BF16)
    y = _dot_t(h, wb_ref[0])
    ob_ref[...] = jax.nn.sigmoid(y).astype(BF16)


def in2(layer, x, g, wt, off, lb_logits, tm=1024, tn=256):
    S = x.shape[0]
    nseg = (wt.shape[1] - off) // D_MODEL
    bps = D_MODEL // tn

    def seg_rows(s, i, j):
        return layer, pl.multiple_of(off + (s * bps + j) * tn, HALO), 0

    w_block = (pl.Element(1), pl.Element(tn), pl.Element(D_MODEL))
    w_specs = [pl.BlockSpec(w_block, functools.partial(seg_rows, s)) for s in range(nseg)]
    out_dtypes = [BF16, F32, BF16, BF16, BF16, BF16]
    return pl.pallas_call(
        functools.partial(_in2_kernel, layer),
        grid=(S // tm, bps),
        in_specs=[pl.BlockSpec((tm, D_MODEL), lambda i, j: (i, 0)),
                  pl.BlockSpec((None, 1, D_MODEL), lambda i, j: (layer, 0, 0))] + w_specs +
                 [pl.BlockSpec((DEPTH, tn), lambda i, j: (0, j))],
        out_specs=[pl.BlockSpec((tm, tn), lambda i, j: (i, j))] * nseg,
        out_shape=[jax.ShapeDtypeStruct((S, D_MODEL), dt) for dt in out_dtypes],
        scratch_shapes=[pltpu.VMEM((tm, D_MODEL), BF16)],
        compiler_params=_cparams(("arbitrary", "arbitrary")),
        name="in2",
    )(x, g, *([wt] * nseg), lb_logits)


def _qkv_kernel(scale, cq_ref, ckv_ref, kpe_ref, cos_ref, sin_ref, wq_ref, wkv_ref, q_ref, k_ref, v_ref):
    cq = cq_ref[...]
    ckv = ckv_ref[...]
    cos = cos_ref[...]
    sin = sin_ref[...]
    kpe = kpe_ref[...]
    qw = NOPE + 2 * LANES
    kvw = NOPE + VDIM
    for h in range(HEADS):
        y = _dot(cq, wq_ref[:, h * qw:(h + 1) * qw])
        pe = y[:, NOPE:NOPE + LANES] * cos + y[:, NOPE + LANES:] * sin
        q_ref[h, :, :NOPE] = (y[:, :NOPE] * scale).astype(BF16)
        q_ref[h, :, NOPE:] = (pe * scale).astype(BF16)
        z = _dot(ckv, wkv_ref[:, h * kvw:(h + 1) * kvw])
        k_ref[h, :, :NOPE] = z[:, :NOPE].astype(BF16)
        k_ref[h, :, NOPE:] = kpe
        v_ref[h] = z[:, NOPE:].astype(BF16)


def qkv(layer, cq, ckv, kpe, cosT, sinT, wq, wkv, tm=512):
    S = cq.shape[0]
    scale = float((NOPE + ROPE) ** -0.5 * LOG2E)
    row = lambda i: (i, 0)
    return pl.pallas_call(
        functools.partial(_qkv_kernel, scale),
        grid=(S // tm,),
        in_specs=[pl.BlockSpec((tm, Q_LORA), row), pl.BlockSpec((tm, KV_LORA), row),
                  pl.BlockSpec((tm, LANES), row), pl.BlockSpec((tm, LANES), row),
                  pl.BlockSpec((tm, LANES), row),
                  pl.BlockSpec((None,) + wq.shape[1:], lambda i: (layer, 0, 0)),
                  pl.BlockSpec((None,) + wkv.shape[1:], lambda i: (layer, 0, 0))],
        out_specs=[pl.BlockSpec((HEADS, tm, QK_PAD), lambda i: (0, i, 0)),
                   pl.BlockSpec((HEADS, tm, QK_PAD), lambda i: (0, i, 0)),
                   pl.BlockSpec((HEADS, tm, VDIM), lambda i: (0, i, 0))],
        out_shape=[jax.ShapeDtypeStruct((HEADS, S, QK_PAD), BF16),
                   jax.ShapeDtypeStruct((HEADS, S, QK_PAD), BF16),
                   jax.ShapeDtypeStruct((HEADS, S, VDIM), BF16)],
        compiler_params=_cparams(("arbitrary",)),
        name="qkv",
    )(cq, ckv, kpe, cosT, sinT, wq, wkv)


def _run_phases(gens):
    live = list(gens)
    while live:
        live = [g for g in live if next(g, live) is not live]


def _attn_kernel(t, hpb, nsplit, q_ref, k_ref, v_ref, o_ref, m_ref, l_ref, acc_ref):
    i = pl.program_id(1)
    nb = t // LANES
    m_ref[...] = jnp.full(m_ref.shape, -jnp.inf, F32)
    l_ref[...] = jnp.zeros(l_ref.shape, F32)
    acc_ref[...] = jnp.zeros(acc_ref.shape, F32)

    def lane_blocks(x):
        return [x[:, b * LANES:(b + 1) * LANES] for b in range(nb)]

    def chain(hh, part, j, masked):
        start = pl.multiple_of(j * t, t)
        rows = slice(part * t, (part + 1) * t)
        k = k_ref[hh, pl.ds(start, t), :]
        s = lax.dot_general(q_ref[hh, rows, :], k, (((1,), (1,)), ((), ())), preferred_element_type=F32)
        yield
        if masked:
            row = lax.broadcasted_iota(jnp.int32, (t, t), 0)
            col = lax.broadcasted_iota(jnp.int32, (t, t), 1)
            s = jnp.where(col <= row, s, MASK_VALUE)
        m_prev = m_ref[hh, rows, :]
        m_cur = functools.reduce(jnp.maximum, lane_blocks(s))
        m_new = jnp.maximum(m_prev, jnp.max(m_cur, axis=-1, keepdims=True))
        alpha = jnp.exp2(m_prev - m_new)
        p = jnp.exp2(s - jnp.concatenate([m_new] * nb, axis=1))
        l_ref[hh, rows, :] = alpha * l_ref[hh, rows, :] + functools.reduce(jnp.add, lane_blocks(p))
        v = v_ref[hh, pl.ds(start, t), :]
        acc_ref[hh, rows, :] = alpha * acc_ref[hh, rows, :] + _dot(p.astype(BF16), v)
        m_ref[hh, rows, :] = m_new

    def body(j, c):
        _run_phases([chain(hh, part, j, False) for hh in range(hpb) for part in range(nsplit)])
        return c

    lax.fori_loop(0, nsplit * i, body, 0)
    for d in range(nsplit):
        _run_phases([chain(hh, part, nsplit * i + d, part == d)
                     for hh in range(hpb) for part in range(d, nsplit)])
    for hh in range(hpb):
        l = jnp.sum(l_ref[hh], axis=-1, keepdims=True)
        o_ref[:, hh * VDIM:(hh + 1) * VDIM] = (acc_ref[hh] / l).astype(BF16)


def attention(q, k, v, t=512, hpb=2, nsplit=2):
    S = q.shape[1]
    tq = nsplit * t
    return pl.pallas_call(
        functools.partial(_attn_kernel, t, hpb, nsplit),
        grid=(HEADS // hpb, S // tq),
        in_specs=[pl.BlockSpec((hpb, tq, QK_PAD), lambda h, i: (h, i, 0)),
                  pl.BlockSpec((hpb, S, QK_PAD), lambda h, i: (h, 0, 0)),
                  pl.BlockSpec((hpb, S, VDIM), lambda h, i: (h, 0, 0))],
        out_specs=pl.BlockSpec((tq, hpb * VDIM), lambda h, i: (i, h)),
        out_shape=jax.ShapeDtypeStruct((S, HEADS * VDIM), BF16),
        scratch_shapes=[pltpu.VMEM((hpb, tq, LANES), F32), pltpu.VMEM((hpb, tq, LANES), F32),
                        pltpu.VMEM((hpb, tq, VDIM), F32)],
        compiler_params=_cparams(("arbitrary", "arbitrary")),
        name="attention",
    )(q, k, v)


def _hgrn_kernel(nblk, hpb, q_ref, f_ref, v_ref, gate_ref, gn_ref, o_ref, st_ref, bs2_ref):
    @pl.when(pl.program_id(1) == 0)
    def _():
        st_ref[...] = jnp.zeros(st_ref.shape, F32)

    B = LANES
    r = lax.broadcasted_iota(jnp.int32, (B, B), 0)
    c = lax.broadcasted_iota(jnp.int32, (B, B), 1)
    same = (r // SUB) == (c // SUB)
    causal = same & (c <= r)
    incl = jnp.where(causal, 1.0, 0.0).astype(BF16)
    nsub = B // SUB
    row_sub = r // SUB
    col_sub = c // SUB
    lane = lax.broadcasted_iota(jnp.int32, (SUB, B), 1)
    ones = jnp.ones((HG_D, B), BF16)
    gn = gn_ref[...]

    def one_head(hh, rows):
        cols = slice(hh * HG_D, (hh + 1) * HG_D)
        f = f_ref[rows, cols]
        q = q_ref[rows, cols].astype(F32)
        g = jnp.log(jnp.maximum(f, MIN_FORGET))
        g0 = g.astype(BF16)
        r1 = g - g0.astype(F32)
        g1 = r1.astype(BF16)
        g2 = (r1 - g1.astype(F32)).astype(BF16)
        b3 = _dot(incl, jnp.concatenate([g0, g1, g2], axis=1))
        b = b3[:, :HG_D] + b3[:, HG_D:2 * HG_D] + b3[:, 2 * HG_D:]
        b_last = jnp.concatenate(
            [jnp.broadcast_to(b[(u + 1) * SUB - 1:(u + 1) * SUB], (SUB, HG_D)) for u in range(nsub)], axis=0)
        eb = jnp.exp(b)
        kin = jnp.maximum(1.0 - f, 0.0)
        qe = (q * eb).astype(BF16)
        kd = (kin * jnp.exp(b_last - b)).astype(BF16)
        vb = v_ref[rows, cols]
        b2 = b * LOG2E
        bs2_ref[hh] = b2 - jnp.log2(kin)
        yield
        ps = []
        sums = {}
        for u in range(nsub):
            lo = u * SUB
            q16 = q[lo:lo + SUB]
            b16 = b2[lo:lo + SUB]
            for s in range(SUB):
                p = q16 * jnp.exp2(b16 - bs2_ref[hh, pl.ds(lo + s, 1), :])
                if s % 2 == 0:
                    ps.append(p.astype(BF16))
                else:
                    sums[lo + s] = jnp.sum(p, axis=-1, keepdims=True)
        rs = _dot(jnp.concatenate(ps, axis=0), ones)
        yield
        for n, rr in enumerate(range(0, B, 2)):
            sums[rr] = rs[n * SUB:(n + 1) * SUB]
        a_rows = []
        for u in range(nsub):
            a = jnp.zeros((SUB, B), F32)
            for s in range(SUB):
                rr = u * SUB + s
                a = jnp.where(lane == rr, sums[rr], a)
            a_rows.append(a)
        a_blk = jnp.where(causal, jnp.concatenate(a_rows, axis=0), 0.0).astype(BF16)

        vt = jnp.transpose(vb)
        vt_u = jnp.concatenate([jnp.where(col_sub == u, vt, 0.0) for u in range(nsub)], axis=0)
        upd = _dot(vt_u, kd)
        yield
        st = st_ref[hh]
        sts = []
        for u in range(nsub):
            sts.append(st.astype(BF16))
            st = st * eb[(u + 1) * SUB - 1:(u + 1) * SUB] + upd[u * B:(u + 1) * B]
        st_ref[hh] = st
        qe_u = jnp.concatenate([jnp.where(row_sub == u, qe, 0.0) for u in range(nsub)], axis=1)
        o_int = lax.dot_general(qe_u, jnp.concatenate(sts, axis=1), (((1,), (1,)), ((), ())),
                                preferred_element_type=F32)
        o = _dot(a_blk, vb) + o_int
        o = _rms(o, gn) * gate_ref[rows, cols].astype(F32)
        o_ref[rows, cols] = o.astype(BF16)

    def blk_body(bi, carry):
        rows = pl.ds(pl.multiple_of(bi * B, B), B)
        _run_phases([one_head(hh, rows) for hh in range(hpb)])
        return carry

    lax.fori_loop(0, nblk, blk_body, 0)


def hgrn(layer, q, f, v, gate, gn, ts=1024, hpb=4):
    S = q.shape[0]
    w = hpb * HG_D
    blk = pl.BlockSpec((ts, w), lambda h, i: (i, h))
    return pl.pallas_call(
        functools.partial(_hgrn_kernel, ts // LANES, hpb),
        grid=(HEADS // hpb, S // ts),
        in_specs=[blk, blk, blk, blk, pl.BlockSpec((None, 1, HG_D), lambda h, i: (layer, 0, 0))],
        out_specs=blk,
        out_shape=jax.ShapeDtypeStruct((S, D_MODEL), BF16),
        scratch_shapes=[pltpu.VMEM((hpb, HG_D, HG_D), F32), pltpu.VMEM((hpb, LANES, HG_D), F32)],
        compiler_params=_cparams(("arbitrary", "arbitrary")),
        name="hgrn",
    )(q, f, v, gate, gn)


def _merge_kernel(a_ref, r_ref, ga_ref, gb_ref, wa_ref, wb_ref, wo_ref, x_ref, o_ref):
    @pl.when(pl.program_id(1) == 0)
    def _():
        o_ref[...] = x_ref[...]

    ma = _dot(a_ref[...], wa_ref[...])
    mb = _dot(r_ref[...], wb_ref[...])
    mg = (ga_ref[...].astype(F32) * ma + gb_ref[...].astype(F32) * mb).astype(BF16)
    o_ref[...] += _dot(mg, wo_ref[...])


def merge(layer, attn, rec, ga, gb, wa, wb, wo, x, tm=1024, tn=512):
    S = x.shape[0]
    row_once = dict(pipeline_mode=pl.Buffered(1))
    return pl.pallas_call(
        _merge_kernel,
        grid=(S // tm, D_MODEL // tn),
        in_specs=[pl.BlockSpec((tm, D_MODEL), lambda i, j: (i, 0)),
                  pl.BlockSpec((tm, D_MODEL), lambda i, j: (i, 0)),
                  pl.BlockSpec((tm, tn), lambda i, j: (i, j)),
                  pl.BlockSpec((tm, tn), lambda i, j: (i, j)),
                  pl.BlockSpec((None, D_MODEL, tn), lambda i, j: (layer, 0, j)),
                  pl.BlockSpec((None, D_MODEL, tn), lambda i, j: (layer, 0, j)),
                  pl.BlockSpec((None, tn, D_MODEL), lambda i, j: (layer, j, 0)),
                  pl.BlockSpec((tm, D_MODEL), lambda i, j: (i, 0), **row_once)],
        out_specs=pl.BlockSpec((tm, D_MODEL), lambda i, j: (i, 0), **row_once),
        out_shape=jax.ShapeDtypeStruct((S, D_MODEL), F32),
        compiler_params=_cparams(("arbitrary", "arbitrary")),
        name="merge",
    )(attn, rec, ga, gb, wa, wb, wo, x)


def _ffn_kernel(final, nj, tm, x_ref, xh_ref, g_ref, wg_ref, wu_ref, cwg_ref, cwu_ref, cbg_ref, cbu_ref,
                wd_ref, fg_ref, o_ref, h_ref, ug_ref, uu_ref):
    m = pl.program_id(0)
    j = pl.program_id(1)

    @pl.when(j == 0)
    def _():
        g = g_ref[...]
        x = x_ref[...]
        o_ref[...] = x
        h_ref[HALO:, :] = _rms(x, g).astype(BF16)
        hh = jnp.where(m > 0, _rms(xh_ref[...], g), 0.0)
        h_ref[:HALO, :] = hh.astype(BF16)

    h = h_ref[...]
    ug_ref[...] = _dot(h, wg_ref[...])
    uu_ref[...] = _dot(h, wu_ref[...])

    def conv(u_ref, cw_ref, cb_ref):
        y = cb_ref[...]
        for tap in range(3):
            y = y + cw_ref[tap:tap + 1, :] * u_ref[pl.ds(HALO - 2 + tap, tm), :]
        return y

    yg = conv(ug_ref, cwg_ref, cbg_ref)
    yu = conv(uu_ref, cwu_ref, cbu_ref)
    act = (yg * jax.nn.sigmoid(yg) * yu).astype(BF16)
    o_ref[...] += _dot(act, wd_ref[...])

    if final:
        @pl.when(j == nj - 1)
        def _():
            o_ref[...] = _rms(o_ref[...], fg_ref[...])


def ffn(layer, x, g, w_up, conv_w, conv_b, w_down, fg, tm=1024, tn=512):
    S = x.shape[0]
    nj = D_FF // tn
    hb = tm // HALO
    final = layer == DEPTH - 1
    row_once = dict(pipeline_mode=pl.Buffered(1))
    return pl.pallas_call(
        functools.partial(_ffn_kernel, final, nj, tm),
        grid=(S // tm, nj),
        in_specs=[pl.BlockSpec((tm, D_MODEL), lambda i, j: (i, 0), **row_once),
                  pl.BlockSpec((HALO, D_MODEL), lambda i, j: (jnp.maximum(i * hb - 1, 0), 0)),
                  pl.BlockSpec((None, 1, D_MODEL), lambda i, j: (layer, 0, 0)),
                  pl.BlockSpec((None, D_MODEL, tn), lambda i, j: (layer, 0, j)),
                  pl.BlockSpec((None, D_MODEL, tn), lambda i, j: (layer, 0, nj + j)),
                  pl.BlockSpec((None, 3, tn), lambda i, j: (layer, 0, j)),
                  pl.BlockSpec((None, 3, tn), lambda i, j: (layer, 0, nj + j)),
                  pl.BlockSpec((None, 1, tn), lambda i, j: (layer, 0, j)),
                  pl.BlockSpec((None, 1, tn), lambda i, j: (layer, 0, nj + j)),
                  pl.BlockSpec((None, tn, D_MODEL), lambda i, j: (layer, j, 0)),
                  pl.BlockSpec((1, D_MODEL), lambda i, j: (0, 0))],
        out_specs=pl.BlockSpec((tm, D_MODEL), lambda i, j: (i, 0), **row_once),
        out_shape=jax.ShapeDtypeStruct((S, D_MODEL), F32),
        scratch_shapes=[pltpu.VMEM((tm + HALO, D_MODEL), BF16),
                        pltpu.VMEM((tm + HALO, tn), F32),
                        pltpu.VMEM((tm + HALO, tn), F32)],
        compiler_params=_cparams(("arbitrary", "arbitrary")),
        name="ffn",
    )(x, x, g, w_up, w_up, conv_w, conv_w, conv_b, conv_b, w_down, fg)


def _rope_pair_cols(w):
    x1, x2 = w[..., :ROPE // 2], w[..., ROPE // 2:]
    z = jnp.zeros(w.shape[:-1] + (LANES - ROPE,), w.dtype)
    return jnp.concatenate([x1, x2, z], axis=-1), jnp.concatenate([x2, x1, z], axis=-1)


def _rope_pair_rows(wt):
    x1, x2 = wt[..., :ROPE // 2, :], wt[..., ROPE // 2:, :]
    z = jnp.zeros(wt.shape[:-2] + (LANES - ROPE, wt.shape[-1]), wt.dtype)
    return jnp.concatenate([x1, x2, z], axis=-2), jnp.concatenate([x2, x1, z], axis=-2)


def kernel(x, positions, attn_norm_g, w_in, q_norm_g, w_uq, kv_norm_g, w_ukv, hgrn_lb_logits, hgrn_out_norm_g,
           w_branch_a, w_branch_b, w_out, ffn_norm_g, w_up, conv_w, conv_b, w_down, final_norm_g):
    B, S, D = x.shape
    assert B == 1 and D == D_MODEL
    xs = x.reshape(S, D)

    inv_freq = ROPE_THETA ** (-jnp.arange(0, ROPE, 2, dtype=F32) / ROPE)
    zeros = jnp.zeros((LANES - ROPE,), F32)
    invf = jnp.concatenate([inv_freq, inv_freq, zeros]).reshape(1, LANES)
    half = jnp.ones((ROPE // 2,), F32)
    sgn = jnp.concatenate([-half, half, zeros]).reshape(1, LANES)
    cosT, sinT = rope_tables(positions.reshape(S, 1), invf, sgn)

    n_lat = Q_LORA + KV_LORA
    wt = jnp.swapaxes(w_in, 1, 2).astype(BF16)
    ka, kb = _rope_pair_rows(wt[:, n_lat:n_lat + ROPE])
    w1 = jnp.concatenate([wt[:, :n_lat], ka, kb], axis=1)
    wq = w_uq.reshape(DEPTH, Q_LORA, HEADS, NOPE + ROPE)
    qa, qb = _rope_pair_cols(wq[..., NOPE:])
    wq = jnp.concatenate([wq[..., :NOPE], qa, qb], axis=-1).reshape(DEPTH, Q_LORA, -1).astype(BF16)
    wkv = w_ukv.astype(BF16)
    wa, wb, wo = w_branch_a.astype(BF16), w_branch_b.astype(BF16), w_out.astype(BF16)
    wup, wdn = w_up.astype(BF16), w_down.astype(BF16)
    conv_b3 = conv_b.reshape(DEPTH, 1, 2 * D_FF)
    lb_logits = hgrn_lb_logits.astype(F32)
    fg = final_norm_g.reshape(1, D)

    rows3 = lambda a: a.reshape(DEPTH, 1, a.shape[-1])
    g_attn, g_q, g_kv, g_hg, g_ffn = map(rows3, (attn_norm_g, q_norm_g, kv_norm_g, hgrn_out_norm_g, ffn_norm_g))

    for l in range(DEPTH):
        cq, ckv, kpe = in1(l, xs, g_attn, w1, g_q, g_kv, cosT, sinT)
        hq, hf, hi, hg, ga, gb = in2(l, xs, g_attn, wt, n_lat + ROPE, lb_logits)
        q, k, v = qkv(l, cq, ckv, kpe, cosT, sinT, wq, wkv)
        attn = attention(q, k, v)
        rec = hgrn(l, hq, hf, hi, hg, g_hg)
        xs = merge(l, attn, rec, ga, gb, wa, wb, wo, xs)
        xs = ffn(l, xs, g_ffn, wup, conv_w, conv_b3, wdn, fg)
    return xs.reshape(B, S, D)
```

```python
import functools

import jax
import jax.numpy as jnp
from jax import lax
from jax.experimental import pallas as pl
from jax.experimental.pallas import tpu as pltpu

F32 = jnp.float32
BF16 = jnp.bfloat16

D_MODEL = 2048
DEPTH = 4
HEADS = 16
Q_LORA = 512
KV_LORA = 512
NOPE = 128
ROPE = 64
VDIM = 128
ROPE_THETA = 10000.0
MASK_VALUE = -1e30
HG_D = 128
SUB = 16
MIN_FORGET = 1e-30
D_FF = 5632
EPS = 1e-6
LOG2E = 1.4426950408889634
LANES = 128
QK_PAD = 256
HALO = 16

VMEM_LIMIT = 52 * 1024 * 1024


def _cparams(sem):
    return pltpu.CompilerParams(dimension_semantics=sem, vmem_limit_bytes=VMEM_LIMIT)


def _rms(x, g):
    return x * lax.rsqrt(jnp.mean(x * x, axis=-1, keepdims=True) + EPS) * g


def _dot(a, b):
    return jnp.dot(a, b, preferred_element_type=F32)


def _dot_t(a, bt):
    return lax.dot_general(a, bt, (((1,), (1,)), ((), ())), preferred_element_type=F32)


def _rope_kernel(pos_ref, invf_ref, sgn_ref, cos_ref, sin_ref):
    ang = pos_ref[...].astype(F32) * invf_ref[...]
    sgn = sgn_ref[...]
    cos_ref[...] = jnp.cos(ang) * jnp.abs(sgn)
    sin_ref[...] = jnp.sin(ang) * sgn


def rope_tables(pos_col, invf, sgn, tm=1024):
    S = pos_col.shape[0]
    return pl.pallas_call(
        _rope_kernel,
        grid=(S // tm,),
        in_specs=[pl.BlockSpec((tm, 1), lambda i: (i, 0)),
                  pl.BlockSpec((1, LANES), lambda i: (0, 0)),
                  pl.BlockSpec((1, LANES), lambda i: (0, 0))],
        out_specs=[pl.BlockSpec((tm, LANES), lambda i: (i, 0))] * 2,
        out_shape=[jax.ShapeDtypeStruct((S, LANES), F32)] * 2,
        compiler_params=_cparams(("arbitrary",)),
        name="rope_tables",
    )(pos_col, invf, sgn)


def _in1_kernel(x_ref, g_ref, w_ref, qg_ref, kvg_ref, cos_ref, sin_ref, cq_ref, ckv_ref, kpe_ref):
    h = _rms(x_ref[...], g_ref[...]).astype(BF16)
    y = _dot_t(h, w_ref[...])
    cq_ref[...] = _rms(y[:, :Q_LORA], qg_ref[...]).astype(BF16)
    ckv_ref[...] = _rms(y[:, Q_LORA:Q_LORA + KV_LORA], kvg_ref[...]).astype(BF16)
    o = Q_LORA + KV_LORA
    kpe = y[:, o:o + LANES] * cos_ref[...] + y[:, o + LANES:o + 2 * LANES] * sin_ref[...]
    kpe_ref[...] = kpe.astype(BF16)


def in1(layer, x, g, w1, qg, kvg, cosT, sinT, tm=512):
    S = x.shape[0]
    n1 = w1.shape[1]
    row = lambda i: (i, 0)
    lyr = lambda i: (layer, 0, 0)
    return pl.pallas_call(
        _in1_kernel,
        grid=(S // tm,),
        in_specs=[pl.BlockSpec((tm, D_MODEL), row), pl.BlockSpec((None, 1, D_MODEL), lyr),
                  pl.BlockSpec((None, n1, D_MODEL), lyr), pl.BlockSpec((None, 1, Q_LORA), lyr),
                  pl.BlockSpec((None, 1, KV_LORA), lyr), pl.BlockSpec((tm, LANES), row),
                  pl.BlockSpec((tm, LANES), row)],
        out_specs=[pl.BlockSpec((tm, Q_LORA), row), pl.BlockSpec((tm, KV_LORA), row),
                   pl.BlockSpec((tm, LANES), row)],
        out_shape=[jax.ShapeDtypeStruct((S, Q_LORA), BF16), jax.ShapeDtypeStruct((S, KV_LORA), BF16),
                   jax.ShapeDtypeStruct((S, LANES), BF16)],
        compiler_params=_cparams(("arbitrary",)),
        name="in1",
    )(x, g, w1, qg, kvg, cosT, sinT)


def _in2_kernel(layer, x_ref, g_ref, wq_ref, wf_ref, wi_ref, wg_ref, wa_ref, wb_ref, lbl_ref,
                oq_ref, of_ref, oi_ref, og_ref, oa_ref, ob_ref, h_ref):
    @pl.when(pl.program_id(1) == 0)
    def _():
        h_ref[...] = _rms(x_ref[...], g_ref[...]).astype(BF16)

    h = h_ref[...]
    y = _dot_t(h, wq_ref[0])
    oq_ref[...] = (y * jax.nn.sigmoid(y)).astype(BF16)
    y = _dot_t(h, wf_ref[0])
    lg = lbl_ref[...]
    e = jnp.exp(lg - jnp.max(lg, axis=0, keepdims=True))
    p = e / jnp.sum(e, axis=0, keepdims=True)
    lb = jnp.sum(p[:layer + 1], axis=0, keepdims=True) - p[0:1]
    of_ref[...] = lb + (1.0 - lb) * jax.nn.sigmoid(y)
    y = _dot_t(h, wi_ref[0])
    oi_ref[...] = y.astype(BF16)
    y = _dot_t(h, wg_ref[0])
    og_ref[...] = (y * jax.nn.sigmoid(y)).astype(BF16)
    y = _dot_t(h, wa_ref[0])
    oa_ref[...] = jax.nn.sigmoid(y).astype(BF16)
    y = _dot_t(h, wb_ref[0])
    ob_ref[...] = jax.nn.sigmoid(y).astype(BF16)


def in2(layer, x, g, wt, off, lb_logits, tm=1024, tn=256):
    S = x.shape[0]
    nseg = (wt.shape[1] - off) // D_MODEL
    bps = D_MODEL // tn

    def seg_rows(s, i, j):
        return layer, pl.multiple_of(off + (s * bps + j) * tn, HALO), 0

    w_block = (pl.Element(1), pl.Element(tn), pl.Element(D_MODEL))
    w_specs = [pl.BlockSpec(w_block, functools.partial(seg_rows, s)) for s in range(nseg)]
    out_dtypes = [BF16, F32, BF16, BF16, BF16, BF16]
    return pl.pallas_call(
        functools.partial(_in2_kernel, layer),
        grid=(S // tm, bps),
        in_specs=[pl.BlockSpec((tm, D_MODEL), lambda i, j: (i, 0)),
                  pl.BlockSpec((None, 1, D_MODEL), lambda i, j: (layer, 0, 0))] + w_specs +
                 [pl.BlockSpec((DEPTH, tn), lambda i, j: (0, j))],
        out_specs=[pl.BlockSpec((tm, tn), lambda i, j: (i, j))] * nseg,
        out_shape=[jax.ShapeDtypeStruct((S, D_MODEL), dt) for dt in out_dtypes],
        scratch_shapes=[pltpu.VMEM((tm, D_MODEL), BF16)],
        compiler_params=_cparams(("arbitrary", "arbitrary")),
        name="in2",
    )(x, g, *([wt] * nseg), lb_logits)


def _qkv_kernel(scale, cq_ref, ckv_ref, kpe_ref, cos_ref, sin_ref, wq_ref, wkv_ref, q_ref, k_ref, v_ref):
    cq = cq_ref[...]
    ckv = ckv_ref[...]
    cos = cos_ref[...]
    sin = sin_ref[...]
    kpe = kpe_ref[...]
    qw = NOPE + 2 * LANES
    kvw = NOPE + VDIM
    grp = 4
    for g in range(HEADS // grp):
        yg = _dot(cq, wq_ref[:, g * grp * qw:(g + 1) * grp * qw])
        zg = _dot(ckv, wkv_ref[:, g * grp * kvw:(g + 1) * grp * kvw])
        for hh in range(grp):
            h = g * grp + hh
            y = yg[:, hh * qw:(hh + 1) * qw]
            pe = y[:, NOPE:NOPE + LANES] * cos + y[:, NOPE + LANES:] * sin
            q_ref[h, :, :NOPE] = (y[:, :NOPE] * scale).astype(BF16)
            q_ref[h, :, NOPE:] = (pe * scale).astype(BF16)
            z = zg[:, hh * kvw:(hh + 1) * kvw]
            k_ref[h, :, :NOPE] = z[:, :NOPE].astype(BF16)
            k_ref[h, :, NOPE:] = kpe
            v_ref[h] = z[:, NOPE:].astype(BF16)


def qkv(layer, cq, ckv, kpe, cosT, sinT, wq, wkv, tm=512):
    S = cq.shape[0]
    scale = float((NOPE + ROPE) ** -0.5 * LOG2E)
    row = lambda i: (i, 0)
    return pl.pallas_call(
        functools.partial(_qkv_kernel, scale),
        grid=(S // tm,),
        in_specs=[pl.BlockSpec((tm, Q_LORA), row), pl.BlockSpec((tm, KV_LORA), row),
                  pl.BlockSpec((tm, LANES), row), pl.BlockSpec((tm, LANES), row),
                  pl.BlockSpec((tm, LANES), row),
                  pl.BlockSpec((None,) + wq.shape[1:], lambda i: (layer, 0, 0)),
                  pl.BlockSpec((None,) + wkv.shape[1:], lambda i: (layer, 0, 0))],
        out_specs=[pl.BlockSpec((HEADS, tm, QK_PAD), lambda i: (0, i, 0)),
                   pl.BlockSpec((HEADS, tm, QK_PAD), lambda i: (0, i, 0)),
                   pl.BlockSpec((HEADS, tm, VDIM), lambda i: (0, i, 0))],
        out_shape=[jax.ShapeDtypeStruct((HEADS, S, QK_PAD), BF16),
                   jax.ShapeDtypeStruct((HEADS, S, QK_PAD), BF16),
                   jax.ShapeDtypeStruct((HEADS, S, VDIM), BF16)],
        compiler_params=_cparams(("arbitrary",)),
        name="qkv",
    )(cq, ckv, kpe, cosT, sinT, wq, wkv)


def _run_phases(gens):
    live = list(gens)
    while live:
        live = [g for g in live if next(g, live) is not live]


def _attn_kernel(t, hpb, nsplit, q_ref, k_ref, v_ref, o_ref, m_ref, l_ref, acc_ref):
    i = pl.program_id(1)
    tk = nsplit * t
    m_ref[...] = jnp.full(m_ref.shape, -jnp.inf, F32)
    l_ref[...] = jnp.zeros(l_ref.shape, F32)
    acc_ref[...] = jnp.zeros(acc_ref.shape, F32)

    def chain(hh, part, j, kw, masked):
        nb = kw // LANES
        start = pl.multiple_of(j * tk, tk)
        rows = slice(part * t, (part + 1) * t)
        k = k_ref[hh, pl.ds(start, kw), :]
        s = lax.dot_general(q_ref[hh, rows, :], k, (((1,), (1,)), ((), ())), preferred_element_type=F32)
        yield
        if masked:
            row = lax.broadcasted_iota(jnp.int32, (t, kw), 0) + part * t
            col = lax.broadcasted_iota(jnp.int32, (t, kw), 1)
            s = jnp.where(col <= row, s, MASK_VALUE)
        lane_blocks = lambda x: [x[:, b * LANES:(b + 1) * LANES] for b in range(nb)]
        m_prev = m_ref[hh, rows, :]
        m_cur = functools.reduce(jnp.maximum, lane_blocks(s))
        m_new = jnp.maximum(m_prev, jnp.max(m_cur, axis=-1, keepdims=True))
        alpha = jnp.exp2(m_prev - m_new)
        p = jnp.exp2(s - jnp.concatenate([m_new] * nb, axis=1))
        l_ref[hh, rows, :] = alpha * l_ref[hh, rows, :] + functools.reduce(jnp.add, lane_blocks(p))
        v = v_ref[hh, pl.ds(start, kw), :]
        acc_ref[hh, rows, :] = alpha * acc_ref[hh, rows, :] + _dot(p.astype(BF16), v)
        m_ref[hh, rows, :] = m_new

    def body(j, c):
        _run_phases([chain(hh, part, j, tk, False) for hh in range(hpb) for part in range(nsplit)])
        return c

    lax.fori_loop(0, i, body, 0)
    _run_phases([chain(hh, part, i, (part + 1) * t, True) for hh in range(hpb) for part in range(nsplit)])
    for hh in range(hpb):
        l = jnp.sum(l_ref[hh], axis=-1, keepdims=True)
        o_ref[:, hh * VDIM:(hh + 1) * VDIM] = (acc_ref[hh] / l).astype(BF16)


def attention(q, k, v, t=512, hpb=2, nsplit=2):
    S = q.shape[1]
    tq = nsplit * t
    return pl.pallas_call(
        functools.partial(_attn_kernel, t, hpb, nsplit),
        grid=(HEADS // hpb, S // tq),
        in_specs=[pl.BlockSpec((hpb, tq, QK_PAD), lambda h, i: (h, i, 0)),
                  pl.BlockSpec((hpb, S, QK_PAD), lambda h, i: (h, 0, 0)),
                  pl.BlockSpec((hpb, S, VDIM), lambda h, i: (h, 0, 0))],
        out_specs=pl.BlockSpec((tq, hpb * VDIM), lambda h, i: (i, h)),
        out_shape=jax.ShapeDtypeStruct((S, HEADS * VDIM), BF16),
        scratch_shapes=[pltpu.VMEM((hpb, tq, LANES), F32), pltpu.VMEM((hpb, tq, LANES), F32),
                        pltpu.VMEM((hpb, tq, VDIM), F32)],
        compiler_params=_cparams(("arbitrary", "arbitrary")),
        name="attention",
    )(q, k, v)


def _hgrn_kernel(nblk, hpb, q_ref, f_ref, v_ref, gate_ref, gn_ref, o_ref, st_ref, bs2_ref):
    @pl.when(pl.program_id(1) == 0)
    def _():
        st_ref[...] = jnp.zeros(st_ref.shape, F32)

    B = LANES
    r = lax.broadcasted_iota(jnp.int32, (B, B), 0)
    c = lax.broadcasted_iota(jnp.int32, (B, B), 1)
    same = (r // SUB) == (c // SUB)
    causal = same & (c <= r)
    incl = jnp.where(causal, 1.0, 0.0).astype(BF16)
    nsub = B // SUB
    row_sub = r // SUB
    col_sub = c // SUB
    lane = lax.broadcasted_iota(jnp.int32, (SUB, B), 1)
    ones = jnp.ones((HG_D, B), BF16)
    gn = gn_ref[...]

    def one_head(hh, rows):
        cols = slice(hh * HG_D, (hh + 1) * HG_D)
        f = f_ref[rows, cols]
        q = q_ref[rows, cols].astype(F32)
        g = jnp.log(jnp.maximum(f, MIN_FORGET))
        g0 = g.astype(BF16)
        r1 = g - g0.astype(F32)
        g1 = r1.astype(BF16)
        g2 = (r1 - g1.astype(F32)).astype(BF16)
        b3 = _dot(incl, jnp.concatenate([g0, g1, g2], axis=1))
        b = b3[:, :HG_D] + b3[:, HG_D:2 * HG_D] + b3[:, 2 * HG_D:]
        b_last = jnp.concatenate(
            [jnp.broadcast_to(b[(u + 1) * SUB - 1:(u + 1) * SUB], (SUB, HG_D)) for u in range(nsub)], axis=0)
        eb = jnp.exp(b)
        kin = jnp.maximum(1.0 - f, 0.0)
        qe = (q * eb).astype(BF16)
        kd = (kin * jnp.exp(b_last - b)).astype(BF16)
        vb = v_ref[rows, cols]
        b2 = b * LOG2E
        bs2_ref[hh] = b2 - jnp.log2(kin)
        yield
        ps = []
        sums = {}
        for u in range(nsub):
            lo = u * SUB
            q16 = q[lo:lo + SUB]
            b16 = b2[lo:lo + SUB]
            for s in range(SUB):
                p = q16 * jnp.exp2(b16 - bs2_ref[hh, pl.ds(lo + s, 1), :])
                if s % 2 == 0:
                    ps.append(p.astype(BF16))
                else:
                    sums[lo + s] = jnp.sum(p, axis=-1, keepdims=True)
        rs = _dot(jnp.concatenate(ps, axis=0), ones)
        yield
        for n, rr in enumerate(range(0, B, 2)):
            sums[rr] = rs[n * SUB:(n + 1) * SUB]
        a_rows = []
        for u in range(nsub):
            a = jnp.zeros((SUB, B), F32)
            for s in range(SUB):
                rr = u * SUB + s
                a = jnp.where(lane == rr, sums[rr], a)
            a_rows.append(a)
        a_blk = jnp.where(causal, jnp.concatenate(a_rows, axis=0), 0.0).astype(BF16)

        vt = jnp.transpose(vb)
        vt_u = jnp.concatenate([jnp.where(col_sub == u, vt, 0.0) for u in range(nsub)], axis=0)
        upd = _dot(vt_u, kd)
        yield
        st = st_ref[hh]
        sts = []
        for u in range(nsub):
            sts.append(st.astype(BF16))
            st = st * eb[(u + 1) * SUB - 1:(u + 1) * SUB] + upd[u * B:(u + 1) * B]
        st_ref[hh] = st
        qe_u = jnp.concatenate([jnp.where(row_sub == u, qe, 0.0) for u in range(nsub)], axis=1)
        o_int = lax.dot_general(qe_u, jnp.concatenate(sts, axis=1), (((1,), (1,)), ((), ())),
                                preferred_element_type=F32)
        o = _dot(a_blk, vb) + o_int
        o = _rms(o, gn) * gate_ref[rows, cols].astype(F32)
        o_ref[rows, cols] = o.astype(BF16)

    def blk_body(bi, carry):
        rows = pl.ds(pl.multiple_of(bi * B, B), B)
        _run_phases([one_head(hh, rows) for hh in range(hpb)])
        return carry

    lax.fori_loop(0, nblk, blk_body, 0)


def hgrn(layer, q, f, v, gate, gn, ts=1024, hpb=4):
    S = q.shape[0]
    w = hpb * HG_D
    blk = pl.BlockSpec((ts, w), lambda h, i: (i, h))
    return pl.pallas_call(
        functools.partial(_hgrn_kernel, ts // LANES, hpb),
        grid=(HEADS // hpb, S // ts),
        in_specs=[blk, blk, blk, blk, pl.BlockSpec((None, 1, HG_D), lambda h, i: (layer, 0, 0))],
        out_specs=blk,
        out_shape=jax.ShapeDtypeStruct((S, D_MODEL), BF16),
        scratch_shapes=[pltpu.VMEM((hpb, HG_D, HG_D), F32), pltpu.VMEM((hpb, LANES, HG_D), F32)],
        compiler_params=_cparams(("arbitrary", "arbitrary")),
        name="hgrn",
    )(q, f, v, gate, gn)


def _merge_kernel(a_ref, r_ref, ga_ref, gb_ref, wa_ref, wb_ref, wo_ref, x_ref, o_ref):
    @pl.when(pl.program_id(1) == 0)
    def _():
        o_ref[...] = x_ref[...]

    ma = _dot(a_ref[...], wa_ref[...])
    mb = _dot(r_ref[...], wb_ref[...])
    mg = (ga_ref[...].astype(F32) * ma + gb_ref[...].astype(F32) * mb).astype(BF16)
    o_ref[...] += _dot(mg, wo_ref[...])


def merge(layer, attn, rec, ga, gb, wa, wb, wo, x, tm=1024, tn=512):
    S = x.shape[0]
    row_once = dict(pipeline_mode=pl.Buffered(1))
    return pl.pallas_call(
        _merge_kernel,
        grid=(S // tm, D_MODEL // tn),
        in_specs=[pl.BlockSpec((tm, D_MODEL), lambda i, j: (i, 0)),
                  pl.BlockSpec((tm, D_MODEL), lambda i, j: (i, 0)),
                  pl.BlockSpec((tm, tn), lambda i, j: (i, j)),
                  pl.BlockSpec((tm, tn), lambda i, j: (i, j)),
                  pl.BlockSpec((None, D_MODEL, tn), lambda i, j: (layer, 0, j)),
                  pl.BlockSpec((None, D_MODEL, tn), lambda i, j: (layer, 0, j)),
                  pl.BlockSpec((None, tn, D_MODEL), lambda i, j: (layer, j, 0)),
                  pl.BlockSpec((tm, D_MODEL), lambda i, j: (i, 0), **row_once)],
        out_specs=pl.BlockSpec((tm, D_MODEL), lambda i, j: (i, 0), **row_once),
        out_shape=jax.ShapeDtypeStruct((S, D_MODEL), F32),
        compiler_params=_cparams(("arbitrary", "arbitrary")),
        name="merge",
    )(attn, rec, ga, gb, wa, wb, wo, x)


def _ffn_kernel(final, nj, tm, x_ref, xh_ref, g_ref, wg_ref, wu_ref, cwg_ref, cwu_ref, cbg_ref, cbu_ref,
                wd_ref, fg_ref, o_ref, h_ref, ug_ref, uu_ref):
    m = pl.program_id(0)
    j = pl.program_id(1)

    @pl.when(j == 0)
    def _():
        g = g_ref[...]
        x = x_ref[...]
        o_ref[...] = x
        h_ref[HALO:, :] = _rms(x, g).astype(BF16)
        hh = jnp.where(m > 0, _rms(xh_ref[...], g), 0.0)
        h_ref[:HALO, :] = hh.astype(BF16)

    h = h_ref[...]
    ug_ref[...] = _dot(h, wg_ref[...])
    uu_ref[...] = _dot(h, wu_ref[...])

    def conv(u_ref, cw_ref, cb_ref):
        y = cb_ref[...]
        for tap in range(3):
            y = y + cw_ref[tap:tap + 1, :] * u_ref[pl.ds(HALO - 2 + tap, tm), :]
        return y

    yg = conv(ug_ref, cwg_ref, cbg_ref)
    yu = conv(uu_ref, cwu_ref, cbu_ref)
    act = (yg * jax.nn.sigmoid(yg) * yu).astype(BF16)
    o_ref[...] += _dot(act, wd_ref[...])

    if final:
        @pl.when(j == nj - 1)
        def _():
            o_ref[...] = _rms(o_ref[...], fg_ref[...])


def ffn(layer, x, g, w_up, conv_w, conv_b, w_down, fg, tm=1024, tn=512):
    S = x.shape[0]
    nj = D_FF // tn
    hb = tm // HALO
    final = layer == DEPTH - 1
    row_once = dict(pipeline_mode=pl.Buffered(1))
    return pl.pallas_call(
        functools.partial(_ffn_kernel, final, nj, tm),
        grid=(S // tm, nj),
        in_specs=[pl.BlockSpec((tm, D_MODEL), lambda i, j: (i, 0), **row_once),
                  pl.BlockSpec((HALO, D_MODEL), lambda i, j: (jnp.maximum(i * hb - 1, 0), 0)),
                  pl.BlockSpec((None, 1, D_MODEL), lambda i, j: (layer, 0, 0)),
                  pl.BlockSpec((None, D_MODEL, tn), lambda i, j: (layer, 0, j)),
                  pl.BlockSpec((None, D_MODEL, tn), lambda i, j: (layer, 0, nj + j)),
                  pl.BlockSpec((None, 3, tn), lambda i, j: (layer, 0, j)),
                  pl.BlockSpec((None, 3, tn), lambda i, j: (layer, 0, nj + j)),
                  pl.BlockSpec((None, 1, tn), lambda i, j: (layer, 0, j)),
                  pl.BlockSpec((None, 1, tn), lambda i, j: (layer, 0, nj + j)),
                  pl.BlockSpec((None, tn, D_MODEL), lambda i, j: (layer, j, 0)),
                  pl.BlockSpec((1, D_MODEL), lambda i, j: (0, 0))],
        out_specs=pl.BlockSpec((tm, D_MODEL), lambda i, j: (i, 0), **row_once),
        out_shape=jax.ShapeDtypeStruct((S, D_MODEL), F32),
        scratch_shapes=[pltpu.VMEM((tm + HALO, D_MODEL), BF16),
                        pltpu.VMEM((tm + HALO, tn), F32),
                        pltpu.VMEM((tm + HALO, tn), F32)],
        compiler_params=_cparams(("arbitrary", "arbitrary")),
        name="ffn",
    )(x, x, g, w_up, w_up, conv_w, conv_w, conv_b, conv_b, w_down, fg)


def _rope_pair_cols(w):
    x1, x2 = w[..., :ROPE // 2], w[..., ROPE // 2:]
    z = jnp.zeros(w.shape[:-1] + (LANES - ROPE,), w.dtype)
    return jnp.concatenate([x1, x2, z], axis=-1), jnp.concatenate([x2, x1, z], axis=-1)


def _rope_pair_rows(wt):
    x1, x2 = wt[..., :ROPE // 2, :], wt[..., ROPE // 2:, :]
    z = jnp.zeros(wt.shape[:-2] + (LANES - ROPE, wt.shape[-1]), wt.dtype)
    return jnp.concatenate([x1, x2, z], axis=-2), jnp.concatenate([x2, x1, z], axis=-2)


def kernel(x, positions, attn_norm_g, w_in, q_norm_g, w_uq, kv_norm_g, w_ukv, hgrn_lb_logits, hgrn_out_norm_g,
           w_branch_a, w_branch_b, w_out, ffn_norm_g, w_up, conv_w, conv_b, w_down, final_norm_g):
    B, S, D = x.shape
    assert B == 1 and D == D_MODEL
    xs = x.reshape(S, D)

    inv_freq = ROPE_THETA ** (-jnp.arange(0, ROPE, 2, dtype=F32) / ROPE)
    zeros = jnp.zeros((LANES - ROPE,), F32)
    invf = jnp.concatenate([inv_freq, inv_freq, zeros]).reshape(1, LANES)
    half = jnp.ones((ROPE // 2,), F32)
    sgn = jnp.concatenate([-half, half, zeros]).reshape(1, LANES)
    cosT, sinT = rope_tables(positions.reshape(S, 1), invf, sgn)

    n_lat = Q_LORA + KV_LORA
    wt = jnp.swapaxes(w_in, 1, 2).astype(BF16)
    ka, kb = _rope_pair_rows(wt[:, n_lat:n_lat + ROPE])
    w1 = jnp.concatenate([wt[:, :n_lat], ka, kb], axis=1)
    wq = w_uq.reshape(DEPTH, Q_LORA, HEADS, NOPE + ROPE)
    qa, qb = _rope_pair_cols(wq[..., NOPE:])
    wq = jnp.concatenate([wq[..., :NOPE], qa, qb], axis=-1).reshape(DEPTH, Q_LORA, -1).astype(BF16)
    wkv = w_ukv.astype(BF16)
    wa, wb, wo = w_branch_a.astype(BF16), w_branch_b.astype(BF16), w_out.astype(BF16)
    wup, wdn = w_up.astype(BF16), w_down.astype(BF16)
    conv_b3 = conv_b.reshape(DEPTH, 1, 2 * D_FF)
    lb_logits = hgrn_lb_logits.astype(F32)
    fg = final_norm_g.reshape(1, D)

    rows3 = lambda a: a.reshape(DEPTH, 1, a.shape[-1])
    g_attn, g_q, g_kv, g_hg, g_ffn = map(rows3, (attn_norm_g, q_norm_g, kv_norm_g, hgrn_out_norm_g, ffn_norm_g))

    for l in range(DEPTH):
        cq, ckv, kpe = in1(l, xs, g_attn, w1, g_q, g_kv, cosT, sinT)
        hq, hf, hi, hg, ga, gb = in2(l, xs, g_attn, wt, n_lat + ROPE, lb_logits)
        q, k, v = qkv(l, cq, ckv, kpe, cosT, sinT, wq, wkv)
        attn = attention(q, k, v)
        rec = hgrn(l, hq, hf, hi, hg, g_hg)
        xs = merge(l, attn, rec, ga, gb, wa, wb, wo, xs)
        xs = ffn(l, xs, g_ffn, wup, conv_w, conv_b3, wdn, fg)
    return xs.reshape(B, S, D)
```

```python
import functools

import jax
import jax.numpy as jnp
from jax import lax
from jax.experimental import pallas as pl
from jax.experimental.pallas import tpu as pltpu

F32 = jnp.float32
BF16 = jnp.bfloat16

D_MODEL = 2048
DEPTH = 4
HEADS = 16
Q_LORA = 512
KV_LORA = 512
NOPE = 128
ROPE = 64
VDIM = 128
ROPE_THETA = 10000.0
MASK_VALUE = -1e30
HG_D = 128
SUB = 16
MIN_FORGET = 1e-30
D_FF = 5632
EPS = 1e-6
LOG2E = 1.4426950408889634
LANES = 128
QK_PAD = 256
HALO = 16

VMEM_LIMIT = 52 * 1024 * 1024


def _cparams(sem):
    return pltpu.CompilerParams(dimension_semantics=sem, vmem_limit_bytes=VMEM_LIMIT)


def _rms(x, g):
    return x * lax.rsqrt(jnp.mean(x * x, axis=-1, keepdims=True) + EPS) * g


def _dot(a, b):
    return jnp.dot(a, b, preferred_element_type=F32)


def _dot_t(a, bt):
    return lax.dot_general(a, bt, (((1,), (1,)), ((), ())), preferred_element_type=F32)


def _rope_kernel(pos_ref, invf_ref, sgn_ref, cos_ref, sin_ref):
    ang = pos_ref[...].astype(F32) * invf_ref[...]
    sgn = sgn_ref[...]
    cos_ref[...] = jnp.cos(ang) * jnp.abs(sgn)
    sin_ref[...] = jnp.sin(ang) * sgn


def rope_tables(pos_col, invf, sgn, tm=1024):
    S = pos_col.shape[0]
    return pl.pallas_call(
        _rope_kernel,
        grid=(S // tm,),
        in_specs=[pl.BlockSpec((tm, 1), lambda i: (i, 0)),
                  pl.BlockSpec((1, LANES), lambda i: (0, 0)),
                  pl.BlockSpec((1, LANES), lambda i: (0, 0))],
        out_specs=[pl.BlockSpec((tm, LANES), lambda i: (i, 0))] * 2,
        out_shape=[jax.ShapeDtypeStruct((S, LANES), F32)] * 2,
        compiler_params=_cparams(("arbitrary",)),
        name="rope_tables",
    )(pos_col, invf, sgn)


def _in1_kernel(x_ref, g_ref, w_ref, qg_ref, kvg_ref, cos_ref, sin_ref, cq_ref, ckv_ref, kpe_ref):
    h = _rms(x_ref[...], g_ref[...]).astype(BF16)
    y = _dot_t(h, w_ref[...])
    cq_ref[...] = _rms(y[:, :Q_LORA], qg_ref[...]).astype(BF16)
    ckv_ref[...] = _rms(y[:, Q_LORA:Q_LORA + KV_LORA], kvg_ref[...]).astype(BF16)
    o = Q_LORA + KV_LORA
    kpe = y[:, o:o + LANES] * cos_ref[...] + y[:, o + LANES:o + 2 * LANES] * sin_ref[...]
    kpe_ref[...] = kpe.astype(BF16)


def in1(layer, x, g, w1, qg, kvg, cosT, sinT, tm=512):
    S = x.shape[0]
    n1 = w1.shape[1]
    row = lambda i: (i, 0)
    lyr = lambda i: (layer, 0, 0)
    return pl.pallas_call(
        _in1_kernel,
        grid=(S // tm,),
        in_specs=[pl.BlockSpec((tm, D_MODEL), row), pl.BlockSpec((None, 1, D_MODEL), lyr),
                  pl.BlockSpec((None, n1, D_MODEL), lyr), pl.BlockSpec((None, 1, Q_LORA), lyr),
                  pl.BlockSpec((None, 1, KV_LORA), lyr), pl.BlockSpec((tm, LANES), row),
                  pl.BlockSpec((tm, LANES), row)],
        out_specs=[pl.BlockSpec((tm, Q_LORA), row), pl.BlockSpec((tm, KV_LORA), row),
                   pl.BlockSpec((tm, LANES), row)],
        out_shape=[jax.ShapeDtypeStruct((S, Q_LORA), BF16), jax.ShapeDtypeStruct((S, KV_LORA), BF16),
                   jax.ShapeDtypeStruct((S, LANES), BF16)],
        compiler_params=_cparams(("arbitrary",)),
        name="in1",
    )(x, g, w1, qg, kvg, cosT, sinT)


def _in2_kernel(layer, x_ref, g_ref, wq_ref, wf_ref, wi_ref, wg_ref, wa_ref, wb_ref, lbl_ref,
                oq_ref, of_ref, oi_ref, og_ref, oa_ref, ob_ref, h_ref):
    @pl.when(pl.program_id(1) == 0)
    def _():
        h_ref[...] = _rms(x_ref[...], g_ref[...]).astype(BF16)

    h = h_ref[...]
    y = _dot_t(h, wq_ref[0])
    oq_ref[...] = (y * jax.nn.sigmoid(y)).astype(BF16)
    y = _dot_t(h, wf_ref[0])
    lg = lbl_ref[...]
    e = jnp.exp(lg - jnp.max(lg, axis=0, keepdims=True))
    p = e / jnp.sum(e, axis=0, keepdims=True)
    lb = jnp.sum(p[:layer + 1], axis=0, keepdims=True) - p[0:1]
    of_ref[...] = lb + (1.0 - lb) * jax.nn.sigmoid(y)
    y = _dot_t(h, wi_ref[0])
    oi_ref[...] = y.astype(BF16)
    y = _dot_t(h, wg_ref[0])
    og_ref[...] = (y * jax.nn.sigmoid(y)).astype(BF16)
    y = _dot_t(h, wa_ref[0])
    oa_ref[...] = jax.nn.sigmoid(y).astype(BF16)
    y = _dot_t(h, wb_ref[0])
    ob_ref[...] = jax.nn.sigmoid(y).astype(BF16)


def in2(layer, x, g, wt, off, lb_logits, tm=1024, tn=256):
    S = x.shape[0]
    nseg = (wt.shape[1] - off) // D_MODEL
    bps = D_MODEL // tn

    def seg_rows(s, i, j):
        return layer, pl.multiple_of(off + (s * bps + j) * tn, HALO), 0

    w_block = (pl.Element(1), pl.Element(tn), pl.Element(D_MODEL))
    w_specs = [pl.BlockSpec(w_block, functools.partial(seg_rows, s)) for s in range(nseg)]
    out_dtypes = [BF16, F32, BF16, BF16, BF16, BF16]
    return pl.pallas_call(
        functools.partial(_in2_kernel, layer),
        grid=(S // tm, bps),
        in_specs=[pl.BlockSpec((tm, D_MODEL), lambda i, j: (i, 0)),
                  pl.BlockSpec((None, 1, D_MODEL), lambda i, j: (layer, 0, 0))] + w_specs +
                 [pl.BlockSpec((DEPTH, tn), lambda i, j: (0, j))],
        out_specs=[pl.BlockSpec((tm, tn), lambda i, j: (i, j))] * nseg,
        out_shape=[jax.ShapeDtypeStruct((S, D_MODEL), dt) for dt in out_dtypes],
        scratch_shapes=[pltpu.VMEM((tm, D_MODEL), BF16)],
        compiler_params=_cparams(("arbitrary", "arbitrary")),
        name="in2",
    )(x, g, *([wt] * nseg), lb_logits)


def _qkv_kernel(scale, cq_ref, ckv_ref, kpe_ref, cos_ref, sin_ref, wq_ref, wkv_ref, q_ref, k_ref, v_ref):
    cq = cq_ref[...]
    ckv = ckv_ref[...]
    cos = cos_ref[...]
    sin = sin_ref[...]
    kpe = kpe_ref[...]
    qw = NOPE + 2 * LANES
    kvw = NOPE + VDIM
    grp = 4
    for g in range(HEADS // grp):
        yg = _dot(cq, wq_ref[:, g * grp * qw:(g + 1) * grp * qw])
        zg = _dot(ckv, wkv_ref[:, g * grp * kvw:(g + 1) * grp * kvw])
        for hh in range(grp):
            h = g * grp + hh
            y = yg[:, hh * qw:(hh + 1) * qw]
            pe = y[:, NOPE:NOPE + LANES] * cos + y[:, NOPE + LANES:] * sin
            q_ref[h, :, :NOPE] = (y[:, :NOPE] * scale).astype(BF16)
            q_ref[h, :, NOPE:] = (pe * scale).astype(BF16)
            z = zg[:, hh * kvw:(hh + 1) * kvw]
            k_ref[h, :, :NOPE] = z[:, :NOPE].astype(BF16)
            k_ref[h, :, NOPE:] = kpe
            v_ref[h] = z[:, NOPE:].astype(BF16)


def qkv(layer, cq, ckv, kpe, cosT, sinT, wq, wkv, tm=512):
    S = cq.shape[0]
    scale = float((NOPE + ROPE) ** -0.5 * LOG2E)
    row = lambda i: (i, 0)
    return pl.pallas_call(
        functools.partial(_qkv_kernel, scale),
        grid=(S // tm,),
        in_specs=[pl.BlockSpec((tm, Q_LORA), row), pl.BlockSpec((tm, KV_LORA), row),
                  pl.BlockSpec((tm, LANES), row), pl.BlockSpec((tm, LANES), row),
                  pl.BlockSpec((tm, LANES), row),
                  pl.BlockSpec((None,) + wq.shape[1:], lambda i: (layer, 0, 0)),
                  pl.BlockSpec((None,) + wkv.shape[1:], lambda i: (layer, 0, 0))],
        out_specs=[pl.BlockSpec((HEADS, tm, QK_PAD), lambda i: (0, i, 0)),
                   pl.BlockSpec((HEADS, tm, QK_PAD), lambda i: (0, i, 0)),
                   pl.BlockSpec((HEADS, tm, VDIM), lambda i: (0, i, 0))],
        out_shape=[jax.ShapeDtypeStruct((HEADS, S, QK_PAD), BF16),
                   jax.ShapeDtypeStruct((HEADS, S, QK_PAD), BF16),
                   jax.ShapeDtypeStruct((HEADS, S, VDIM), BF16)],
        compiler_params=_cparams(("arbitrary",)),
        name="qkv",
    )(cq, ckv, kpe, cosT, sinT, wq, wkv)


def _run_phases(gens):
    live = list(gens)
    while live:
        live = [g for g in live if next(g, live) is not live]


def _attn_kernel(t, hpb, nsplit, q_ref, k_ref, v_ref, o_ref, m_ref, l_ref, acc_ref):
    i = pl.program_id(1)
    tk = nsplit * t
    m_ref[...] = jnp.full(m_ref.shape, -jnp.inf, F32)
    l_ref[...] = jnp.zeros(l_ref.shape, F32)
    acc_ref[...] = jnp.zeros(acc_ref.shape, F32)

    def chain(hh, part, j, kw, masked):
        nb = kw // LANES
        start = pl.multiple_of(j * tk, tk)
        rows = slice(part * t, (part + 1) * t)
        k = k_ref[hh, pl.ds(start, kw), :]
        s = lax.dot_general(q_ref[hh, rows, :], k, (((1,), (1,)), ((), ())), preferred_element_type=F32)
        yield
        if masked:
            row = lax.broadcasted_iota(jnp.int32, (t, kw), 0) + part * t
            col = lax.broadcasted_iota(jnp.int32, (t, kw), 1)
            s = jnp.where(col <= row, s, MASK_VALUE)
        lane_blocks = lambda x: [x[:, b * LANES:(b + 1) * LANES] for b in range(nb)]
        m_prev = m_ref[hh, rows, :]
        m_cur = functools.reduce(jnp.maximum, lane_blocks(s))
        m_new = jnp.maximum(m_prev, jnp.max(m_cur, axis=-1, keepdims=True))
        alpha = jnp.exp2(m_prev - m_new)
        p = jnp.exp2(s - jnp.concatenate([m_new] * nb, axis=1))
        l_ref[hh, rows, :] = alpha * l_ref[hh, rows, :] + functools.reduce(jnp.add, lane_blocks(p))
        v = v_ref[hh, pl.ds(start, kw), :]
        acc_ref[hh, rows, :] = alpha * acc_ref[hh, rows, :] + _dot(p.astype(BF16), v)
        m_ref[hh, rows, :] = m_new

    def body(j, c):
        _run_phases([chain(hh, part, j, tk, False) for hh in range(hpb) for part in range(nsplit)])
        return c

    lax.fori_loop(0, i, body, 0)
    _run_phases([chain(hh, part, i, (part + 1) * t, True) for hh in range(hpb) for part in range(nsplit)])
    for hh in range(hpb):
        l = jnp.sum(l_ref[hh], axis=-1, keepdims=True)
        o_ref[:, hh * VDIM:(hh + 1) * VDIM] = (acc_ref[hh] / l).astype(BF16)


def attention(q, k, v, t=512, hpb=2, nsplit=2):
    S = q.shape[1]
    tq = nsplit * t
    return pl.pallas_call(
        functools.partial(_attn_kernel, t, hpb, nsplit),
        grid=(HEADS // hpb, S // tq),
        in_specs=[pl.BlockSpec((hpb, tq, QK_PAD), lambda h, i: (h, i, 0)),
                  pl.BlockSpec((hpb, S, QK_PAD), lambda h, i: (h, 0, 0)),
                  pl.BlockSpec((hpb, S, VDIM), lambda h, i: (h, 0, 0))],
        out_specs=pl.BlockSpec((tq, hpb * VDIM), lambda h, i: (i, h)),
        out_shape=jax.ShapeDtypeStruct((S, HEADS * VDIM), BF16),
        scratch_shapes=[pltpu.VMEM((hpb, tq, LANES), F32), pltpu.VMEM((hpb, tq, LANES), F32),
                        pltpu.VMEM((hpb, tq, VDIM), F32)],
        compiler_params=_cparams(("arbitrary", "arbitrary")),
        name="attention",
    )(q, k, v)


def _hgrn_kernel(nblk, hpb, q_ref, f_ref, v_ref, gate_ref, gn_ref, o_ref, st_ref, bs2_ref):
    @pl.when(pl.program_id(1) == 0)
    def _():
        st_ref[...] = jnp.zeros(st_ref.shape, F32)

    B = LANES
    r = lax.broadcasted_iota(jnp.int32, (B, B), 0)
    c = lax.broadcasted_iota(jnp.int32, (B, B), 1)
    same = (r // SUB) == (c // SUB)
    causal = same & (c <= r)
    incl = jnp.where(causal, 1.0, 0.0).astype(BF16)
    nsub = B // SUB
    row_sub = r // SUB
    col_sub = c // SUB
    lane = lax.broadcasted_iota(jnp.int32, (SUB, B), 1)
    ones = jnp.ones((HG_D, B), BF16)
    gn = gn_ref[...]

    def one_head(hh, rows):
        cols = slice(hh * HG_D, (hh + 1) * HG_D)
        f = f_ref[rows, cols]
        q = q_ref[rows, cols].astype(F32)
        g = jnp.log(jnp.maximum(f, MIN_FORGET))
        g0 = g.astype(BF16)
        r1 = g - g0.astype(F32)
        g1 = r1.astype(BF16)
        g2 = (r1 - g1.astype(F32)).astype(BF16)
        b3 = _dot(incl, jnp.concatenate([g0, g1, g2], axis=1))
        b = b3[:, :HG_D] + b3[:, HG_D:2 * HG_D] + b3[:, 2 * HG_D:]
        b_last = jnp.concatenate(
            [jnp.broadcast_to(b[(u + 1) * SUB - 1:(u + 1) * SUB], (SUB, HG_D)) for u in range(nsub)], axis=0)
        eb = jnp.exp(b)
        kin = jnp.maximum(1.0 - f, 0.0)
        qe = (q * eb).astype(BF16)
        kd = (kin * jnp.exp(b_last - b)).astype(BF16)
        vb = v_ref[rows, cols]
        b2 = b * LOG2E
        bs2_ref[hh] = b2 - jnp.log2(kin)
        yield
        ps = []
        sums = {}
        for u in range(nsub):
            lo = u * SUB
            q16 = q[lo:lo + SUB]
            b16 = b2[lo:lo + SUB]
            for s in range(SUB):
                p = q16 * jnp.exp2(b16 - bs2_ref[hh, pl.ds(lo + s, 1), :])
                if s % 2 == 0:
                    ps.append(p.astype(BF16))
                else:
                    sums[lo + s] = jnp.sum(p, axis=-1, keepdims=True)
        rs = _dot(jnp.concatenate(ps, axis=0), ones)
        yield
        for n, rr in enumerate(range(0, B, 2)):
            sums[rr] = rs[n * SUB:(n + 1) * SUB]
        a_rows = []
        for u in range(nsub):
            a = jnp.zeros((SUB, B), F32)
            for s in range(SUB):
                rr = u * SUB + s
                a = jnp.where(lane == rr, sums[rr], a)
            a_rows.append(a)
        a_blk = jnp.where(causal, jnp.concatenate(a_rows, axis=0), 0.0).astype(BF16)

        vt = jnp.transpose(vb)
        vt_u = jnp.concatenate([jnp.where(col_sub == u, vt, 0.0) for u in range(nsub)], axis=0)
        upd = _dot(vt_u, kd)
        yield
        st = st_ref[hh]
        sts = []
        for u in range(nsub):
            sts.append(st.astype(BF16))
            st = st * eb[(u + 1) * SUB - 1:(u + 1) * SUB] + upd[u * B:(u + 1) * B]
        st_ref[hh] = st
        qe_u = jnp.concatenate([jnp.where(row_sub == u, qe, 0.0) for u in range(nsub)], axis=1)
        o_int = lax.dot_general(qe_u, jnp.concatenate(sts, axis=1), (((1,), (1,)), ((), ())),
                                preferred_element_type=F32)
        o = _dot(a_blk, vb) + o_int
        o = _rms(o, gn) * gate_ref[rows, cols].astype(F32)
        o_ref[rows, cols] = o.astype(BF16)

    def blk_body(bi, carry):
        rows = pl.ds(pl.multiple_of(bi * B, B), B)
        _run_phases([one_head(hh, rows) for hh in range(hpb)])
        return carry

    lax.fori_loop(0, nblk, blk_body, 0)


def hgrn(layer, q, f, v, gate, gn, ts=1024, hpb=4):
    S = q.shape[0]
    w = hpb * HG_D
    blk = pl.BlockSpec((ts, w), lambda h, i: (i, h))
    return pl.pallas_call(
        functools.partial(_hgrn_kernel, ts // LANES, hpb),
        grid=(HEADS // hpb, S // ts),
        in_specs=[blk, blk, blk, blk, pl.BlockSpec((None, 1, HG_D), lambda h, i: (layer, 0, 0))],
        out_specs=blk,
        out_shape=jax.ShapeDtypeStruct((S, D_MODEL), BF16),
        scratch_shapes=[pltpu.VMEM((hpb, HG_D, HG_D), F32), pltpu.VMEM((hpb, LANES, HG_D), F32)],
        compiler_params=_cparams(("arbitrary", "arbitrary")),
        name="hgrn",
    )(q, f, v, gate, gn)


def _merge_kernel(a_ref, r_ref, ga_ref, gb_ref, wa_ref, wb_ref, wo_ref, x_ref, o_ref):
    @pl.when(pl.program_id(1) == 0)
    def _():
        o_ref[...] = x_ref[...]

    ma = _dot(a_ref[...], wa_ref[...])
    mb = _dot(r_ref[...], wb_ref[...])
    mg = (ga_ref[...].astype(F32) * ma + gb_ref[...].astype(F32) * mb).astype(BF16)
    o_ref[...] += _dot(mg, wo_ref[...])


def merge(layer, attn, rec, ga, gb, wa, wb, wo, x, tm=512, tn=512):
    S = x.shape[0]
    row_once = {}
    return pl.pallas_call(
        _merge_kernel,
        grid=(S // tm, D_MODEL // tn),
        in_specs=[pl.BlockSpec((tm, D_MODEL), lambda i, j: (i, 0)),
                  pl.BlockSpec((tm, D_MODEL), lambda i, j: (i, 0)),
                  pl.BlockSpec((tm, tn), lambda i, j: (i, j)),
                  pl.BlockSpec((tm, tn), lambda i, j: (i, j)),
                  pl.BlockSpec((None, D_MODEL, tn), lambda i, j: (layer, 0, j)),
                  pl.BlockSpec((None, D_MODEL, tn), lambda i, j: (layer, 0, j)),
                  pl.BlockSpec((None, tn, D_MODEL), lambda i, j: (layer, j, 0)),
                  pl.BlockSpec((tm, D_MODEL), lambda i, j: (i, 0), **row_once)],
        out_specs=pl.BlockSpec((tm, D_MODEL), lambda i, j: (i, 0), **row_once),
        out_shape=jax.ShapeDtypeStruct((S, D_MODEL), F32),
        compiler_params=_cparams(("arbitrary", "arbitrary")),
        name="merge",
    )(attn, rec, ga, gb, wa, wb, wo, x)


def _ffn_kernel(final, nj, tm, x_ref, xh_ref, g_ref, wg_ref, wu_ref, cwg_ref, cwu_ref, cbg_ref, cbu_ref,
                wd_ref, fg_ref, o_ref, h_ref, ug_ref, uu_ref):
    m = pl.program_id(0)
    j = pl.program_id(1)

    @pl.when(j == 0)
    def _():
        g = g_ref[...]
        x = x_ref[...]
        o_ref[...] = x
        h_ref[HALO:, :] = _rms(x, g).astype(BF16)
        hh = jnp.where(m > 0, _rms(xh_ref[...], g), 0.0)
        h_ref[:HALO, :] = hh.astype(BF16)

    h = h_ref[...]
    ug_ref[...] = _dot(h, wg_ref[...])
    uu_ref[...] = _dot(h, wu_ref[...])

    def conv(u_ref, cw_ref, cb_ref):
        y = cb_ref[...]
        for tap in range(3):
            y = y + cw_ref[tap:tap + 1, :] * u_ref[pl.ds(HALO - 2 + tap, tm), :]
        return y

    yg = conv(ug_ref, cwg_ref, cbg_ref)
    yu = conv(uu_ref, cwu_ref, cbu_ref)
    act = (yg * jax.nn.sigmoid(yg) * yu).astype(BF16)
    o_ref[...] += _dot(act, wd_ref[...])

    if final:
        @pl.when(j == nj - 1)
        def _():
            o_ref[...] = _rms(o_ref[...], fg_ref[...])


def ffn(layer, x, g, w_up, conv_w, conv_b, w_down, fg, tm=1024, tn=512):
    S = x.shape[0]
    nj = D_FF // tn
    hb = tm // HALO
    final = layer == DEPTH - 1
    row_once = dict(pipeline_mode=pl.Buffered(1))
    return pl.pallas_call(
        functools.partial(_ffn_kernel, final, nj, tm),
        grid=(S // tm, nj),
        in_specs=[pl.BlockSpec((tm, D_MODEL), lambda i, j: (i, 0)),
                  pl.BlockSpec((HALO, D_MODEL), lambda i, j: (jnp.maximum(i * hb - 1, 0), 0)),
                  pl.BlockSpec((None, 1, D_MODEL), lambda i, j: (layer, 0, 0)),
                  pl.BlockSpec((None, D_MODEL, tn), lambda i, j: (layer, 0, j)),
                  pl.BlockSpec((None, D_MODEL, tn), lambda i, j: (layer, 0, nj + j)),
                  pl.BlockSpec((None, 3, tn), lambda i, j: (layer, 0, j)),
                  pl.BlockSpec((None, 3, tn), lambda i, j: (layer, 0, nj + j)),
                  pl.BlockSpec((None, 1, tn), lambda i, j: (layer, 0, j)),
                  pl.BlockSpec((None, 1, tn), lambda i, j: (layer, 0, nj + j)),
                  pl.BlockSpec((None, tn, D_MODEL), lambda i, j: (layer, j, 0)),
                  pl.BlockSpec((1, D_MODEL), lambda i, j: (0, 0))],
        out_specs=pl.BlockSpec((tm, D_MODEL), lambda i, j: (i, 0), **row_once),
        out_shape=jax.ShapeDtypeStruct((S, D_MODEL), F32),
        scratch_shapes=[pltpu.VMEM((tm + HALO, D_MODEL), BF16),
                        pltpu.VMEM((tm + HALO, tn), F32),
                        pltpu.VMEM((tm + HALO, tn), F32)],
        compiler_params=_cparams(("arbitrary", "arbitrary")),
        name="ffn",
    )(x, x, g, w_up, w_up, conv_w, conv_w, conv_b, conv_b, w_down, fg)


def _rope_pair_cols(w):
    x1, x2 = w[..., :ROPE // 2], w[..., ROPE // 2:]
    z = jnp.zeros(w.shape[:-1] + (LANES - ROPE,), w.dtype)
    return jnp.concatenate([x1, x2, z], axis=-1), jnp.concatenate([x2, x1, z], axis=-1)


def _rope_pair_rows(wt):
    x1, x2 = wt[..., :ROPE // 2, :], wt[..., ROPE // 2:, :]
    z = jnp.zeros(wt.shape[:-2] + (LANES - ROPE, wt.shape[-1]), wt.dtype)
    return jnp.concatenate([x1, x2, z], axis=-2), jnp.concatenate([x2, x1, z], axis=-2)


def kernel(x, positions, attn_norm_g, w_in, q_norm_g, w_uq, kv_norm_g, w_ukv, hgrn_lb_logits, hgrn_out_norm_g,
           w_branch_a, w_branch_b, w_out, ffn_norm_g, w_up, conv_w, conv_b, w_down, final_norm_g):
    B, S, D = x.shape
    assert B == 1 and D == D_MODEL
    xs = x.reshape(S, D)

    inv_freq = ROPE_THETA ** (-jnp.arange(0, ROPE, 2, dtype=F32) / ROPE)
    zeros = jnp.zeros((LANES - ROPE,), F32)
    invf = jnp.concatenate([inv_freq, inv_freq, zeros]).reshape(1, LANES)
    half = jnp.ones((ROPE // 2,), F32)
    sgn = jnp.concatenate([-half, half, zeros]).reshape(1, LANES)
    cosT, sinT = rope_tables(positions.reshape(S, 1), invf, sgn)

    n_lat = Q_LORA + KV_LORA
    wt = jnp.swapaxes(w_in, 1, 2).astype(BF16)
    ka, kb = _rope_pair_rows(wt[:, n_lat:n_lat + ROPE])
    w1 = jnp.concatenate([wt[:, :n_lat], ka, kb], axis=1)
    wq = w_uq.reshape(DEPTH, Q_LORA, HEADS, NOPE + ROPE)
    qa, qb = _rope_pair_cols(wq[..., NOPE:])
    wq = jnp.concatenate([wq[..., :NOPE], qa, qb], axis=-1).reshape(DEPTH, Q_LORA, -1).astype(BF16)
    wkv = w_ukv.astype(BF16)
    wa, wb, wo = w_branch_a.astype(BF16), w_branch_b.astype(BF16), w_out.astype(BF16)
    wup, wdn = w_up.astype(BF16), w_down.astype(BF16)
    conv_b3 = conv_b.reshape(DEPTH, 1, 2 * D_FF)
    lb_logits = hgrn_lb_logits.astype(F32)
    fg = final_norm_g.reshape(1, D)

    rows3 = lambda a: a.reshape(DEPTH, 1, a.shape[-1])
    g_attn, g_q, g_kv, g_hg, g_ffn = map(rows3, (attn_norm_g, q_norm_g, kv_norm_g, hgrn_out_norm_g, ffn_norm_g))

    for l in range(DEPTH):
        cq, ckv, kpe = in1(l, xs, g_attn, w1, g_q, g_kv, cosT, sinT)
        hq, hf, hi, hg, ga, gb = in2(l, xs, g_attn, wt, n_lat + ROPE, lb_logits)
        q, k, v = qkv(l, cq, ckv, kpe, cosT, sinT, wq, wkv)
        attn = attention(q, k, v)
        rec = hgrn(l, hq, hf, hi, hg, g_hg)
        xs = merge(l, attn, rec, ga, gb, wa, wb, wo, xs)
        xs = ffn(l, xs, g_ffn, wup, conv_w, conv_b3, wdn, fg)
    return xs.reshape(B, S, D)
```

```python
import functools

import jax
import jax.numpy as jnp
from jax import lax
from jax.experimental import pallas as pl
from jax.experimental.pallas import tpu as pltpu

F32 = jnp.float32
BF16 = jnp.bfloat16

D_MODEL = 2048
DEPTH = 4
HEADS = 16
Q_LORA = 512
KV_LORA = 512
NOPE = 128
ROPE = 64
VDIM = 128
ROPE_THETA = 10000.0
MASK_VALUE = -1e30
HG_D = 128
SUB = 16
MIN_FORGET = 1e-30
D_FF = 5632
EPS = 1e-6
LOG2E = 1.4426950408889634
LANES = 128
QK_PAD = 256
HALO = 16

VMEM_LIMIT = 52 * 1024 * 1024


def _cparams(sem):
    return pltpu.CompilerParams(dimension_semantics=sem, vmem_limit_bytes=VMEM_LIMIT)


def _rms(x, g):
    return x * lax.rsqrt(jnp.mean(x * x, axis=-1, keepdims=True) + EPS) * g


def _dot(a, b):
    return jnp.dot(a, b, preferred_element_type=F32)


def _dot_t(a, bt):
    return lax.dot_general(a, bt, (((1,), (1,)), ((), ())), preferred_element_type=F32)


def _rope_kernel(pos_ref, invf_ref, sgn_ref, cos_ref, sin_ref):
    ang = pos_ref[...].astype(F32) * invf_ref[...]
    sgn = sgn_ref[...]
    cos_ref[...] = jnp.cos(ang) * jnp.abs(sgn)
    sin_ref[...] = jnp.sin(ang) * sgn


def rope_tables(pos_col, invf, sgn, tm=1024):
    S = pos_col.shape[0]
    return pl.pallas_call(
        _rope_kernel,
        grid=(S // tm,),
        in_specs=[pl.BlockSpec((tm, 1), lambda i: (i, 0)),
                  pl.BlockSpec((1, LANES), lambda i: (0, 0)),
                  pl.BlockSpec((1, LANES), lambda i: (0, 0))],
        out_specs=[pl.BlockSpec((tm, LANES), lambda i: (i, 0))] * 2,
        out_shape=[jax.ShapeDtypeStruct((S, LANES), F32)] * 2,
        compiler_params=_cparams(("arbitrary",)),
        name="rope_tables",
    )(pos_col, invf, sgn)


def _in1_kernel(x_ref, g_ref, wlat_ref, wrope_ref, qg_ref, kvg_ref, cos_ref, sin_ref,
                cq_ref, ckv_ref, kpe_ref, h_ref, w_ref):
    n_lat = Q_LORA + KV_LORA
    half = ROPE // 2

    @pl.when(pl.program_id(0) == 0)
    def _():
        w_ref[:n_lat, :] = wlat_ref[0].astype(BF16)
        r = wrope_ref[0].astype(BF16)
        x1, x2 = r[:half], r[half:]
        z = jnp.zeros((LANES - ROPE, D_MODEL), BF16)
        w_ref[n_lat:, :] = jnp.concatenate([x1, x2, z, x2, x1, z], axis=0)

    h = _rms(x_ref[...], g_ref[...]).astype(BF16)
    h_ref[...] = h
    y = _dot_t(h, w_ref[...])
    cq_ref[...] = _rms(y[:, :Q_LORA], qg_ref[...]).astype(BF16)
    ckv_ref[...] = _rms(y[:, Q_LORA:n_lat], kvg_ref[...]).astype(BF16)
    kpe = y[:, n_lat:n_lat + LANES] * cos_ref[...] + y[:, n_lat + LANES:] * sin_ref[...]
    kpe_ref[...] = kpe.astype(BF16)


def in1(layer, x, g, wt, qg, kvg, cosT, sinT, tm=512):
    S = x.shape[0]
    n_lat = Q_LORA + KV_LORA
    row = lambda i: (i, 0)
    lyr = lambda i: (layer, 0, 0)
    elem = lambda n: (pl.Element(1), pl.Element(n), pl.Element(D_MODEL))
    return pl.pallas_call(
        _in1_kernel,
        grid=(S // tm,),
        in_specs=[pl.BlockSpec((tm, D_MODEL), row), pl.BlockSpec((None, 1, D_MODEL), lyr),
                  pl.BlockSpec(elem(n_lat), lyr), pl.BlockSpec(elem(ROPE), lambda i: (layer, n_lat, 0)),
                  pl.BlockSpec((None, 1, Q_LORA), lyr),
                  pl.BlockSpec((None, 1, KV_LORA), lyr), pl.BlockSpec((tm, LANES), row),
                  pl.BlockSpec((tm, LANES), row)],
        out_specs=[pl.BlockSpec((tm, Q_LORA), row), pl.BlockSpec((tm, KV_LORA), row),
                   pl.BlockSpec((tm, LANES), row), pl.BlockSpec((tm, D_MODEL), row)],
        out_shape=[jax.ShapeDtypeStruct((S, Q_LORA), BF16), jax.ShapeDtypeStruct((S, KV_LORA), BF16),
                   jax.ShapeDtypeStruct((S, LANES), BF16), jax.ShapeDtypeStruct((S, D_MODEL), BF16)],
        scratch_shapes=[pltpu.VMEM((n_lat + 2 * LANES, D_MODEL), BF16)],
        compiler_params=_cparams(("arbitrary",)),
        name="in1",
    )(x, g, wt, wt, qg, kvg, cosT, sinT)


def _in2_kernel(layer, h_ref, wq_ref, wf_ref, wi_ref, wg_ref, wa_ref, wb_ref, lbl_ref,
                oq_ref, of_ref, oi_ref, og_ref, oa_ref, ob_ref, w_ref):
    @pl.when(pl.program_id(1) == 0)
    def _():
        for s, src in enumerate((wq_ref, wf_ref, wi_ref, wg_ref, wa_ref, wb_ref)):
            w_ref[s] = src[0].astype(BF16)

    h = h_ref[...]
    y = _dot_t(h, w_ref[0])
    oq_ref[...] = (y * jax.nn.sigmoid(y)).astype(BF16)
    y = _dot_t(h, w_ref[1])
    lg = lbl_ref[...]
    e = jnp.exp(lg - jnp.max(lg, axis=0, keepdims=True))
    p = e / jnp.sum(e, axis=0, keepdims=True)
    lb = jnp.sum(p[:layer + 1], axis=0, keepdims=True) - p[0:1]
    of_ref[...] = lb + (1.0 - lb) * jax.nn.sigmoid(y)
    y = _dot_t(h, w_ref[2])
    oi_ref[...] = y.astype(BF16)
    y = _dot_t(h, w_ref[3])
    og_ref[...] = (y * jax.nn.sigmoid(y)).astype(BF16)
    y = _dot_t(h, w_ref[4])
    oa_ref[...] = jax.nn.sigmoid(y).astype(BF16)
    y = _dot_t(h, w_ref[5])
    ob_ref[...] = jax.nn.sigmoid(y).astype(BF16)


def in2(layer, h, wt, off, lb_logits, tm=1024, tn=256):
    S = h.shape[0]
    nseg = (wt.shape[1] - off) // D_MODEL
    bps = D_MODEL // tn

    def seg_rows(s, j, i):
        return layer, pl.multiple_of(off + (s * bps + j) * tn, HALO), 0

    w_block = (pl.Element(1), pl.Element(tn), pl.Element(D_MODEL))
    w_specs = [pl.BlockSpec(w_block, functools.partial(seg_rows, s)) for s in range(nseg)]
    out_dtypes = [BF16, F32, BF16, BF16, BF16, BF16]
    return pl.pallas_call(
        functools.partial(_in2_kernel, layer),
        grid=(bps, S // tm),
        in_specs=[pl.BlockSpec((tm, D_MODEL), lambda j, i: (i, 0))] + w_specs +
                 [pl.BlockSpec((DEPTH, tn), lambda j, i: (0, j))],
        out_specs=[pl.BlockSpec((tm, tn), lambda j, i: (i, j))] * nseg,
        out_shape=[jax.ShapeDtypeStruct((S, D_MODEL), dt) for dt in out_dtypes],
        scratch_shapes=[pltpu.VMEM((nseg, tn, D_MODEL), BF16)],
        compiler_params=_cparams(("arbitrary", "arbitrary")),
        name="in2",
    )(h, *([wt] * nseg), lb_logits)


def _qkv_kernel(scale, cq_ref, ckv_ref, kpe_ref, cos_ref, sin_ref, wq_ref, wkv_ref, q_ref, k_ref, v_ref):
    cq = cq_ref[...]
    ckv = ckv_ref[...]
    cos = cos_ref[...]
    sin = sin_ref[...]
    kpe = kpe_ref[...]
    qw = NOPE + 2 * LANES
    kvw = NOPE + VDIM
    grp = 4
    for g in range(HEADS // grp):
        yg = _dot(cq, wq_ref[:, g * grp * qw:(g + 1) * grp * qw])
        zg = _dot(ckv, wkv_ref[:, g * grp * kvw:(g + 1) * grp * kvw])
        for hh in range(grp):
            h = g * grp + hh
            y = yg[:, hh * qw:(hh + 1) * qw]
            pe = y[:, NOPE:NOPE + LANES] * cos + y[:, NOPE + LANES:] * sin
            q_ref[h, :, :NOPE] = (y[:, :NOPE] * scale).astype(BF16)
            q_ref[h, :, NOPE:] = (pe * scale).astype(BF16)
            z = zg[:, hh * kvw:(hh + 1) * kvw]
            k_ref[h, :, :NOPE] = z[:, :NOPE].astype(BF16)
            k_ref[h, :, NOPE:] = kpe
            v_ref[h] = z[:, NOPE:].astype(BF16)


def qkv(layer, cq, ckv, kpe, cosT, sinT, wq, wkv, tm=512):
    S = cq.shape[0]
    scale = float((NOPE + ROPE) ** -0.5 * LOG2E)
    row = lambda i: (i, 0)
    return pl.pallas_call(
        functools.partial(_qkv_kernel, scale),
        grid=(S // tm,),
        in_specs=[pl.BlockSpec((tm, Q_LORA), row), pl.BlockSpec((tm, KV_LORA), row),
                  pl.BlockSpec((tm, LANES), row), pl.BlockSpec((tm, LANES), row),
                  pl.BlockSpec((tm, LANES), row),
                  pl.BlockSpec((None,) + wq.shape[1:], lambda i: (layer, 0, 0)),
                  pl.BlockSpec((None,) + wkv.shape[1:], lambda i: (layer, 0, 0))],
        out_specs=[pl.BlockSpec((HEADS, tm, QK_PAD), lambda i: (0, i, 0)),
                   pl.BlockSpec((HEADS, tm, QK_PAD), lambda i: (0, i, 0)),
                   pl.BlockSpec((HEADS, tm, VDIM), lambda i: (0, i, 0))],
        out_shape=[jax.ShapeDtypeStruct((HEADS, S, QK_PAD), BF16),
                   jax.ShapeDtypeStruct((HEADS, S, QK_PAD), BF16),
                   jax.ShapeDtypeStruct((HEADS, S, VDIM), BF16)],
        compiler_params=_cparams(("arbitrary",)),
        name="qkv",
    )(cq, ckv, kpe, cosT, sinT, wq, wkv)


def _run_phases(gens):
    live = list(gens)
    while live:
        live = [g for g in live if next(g, live) is not live]


def _attn_kernel(t, hpb, nsplit, q_ref, k_ref, v_ref, o_ref, m_ref, l_ref, acc_ref):
    i = pl.program_id(1)
    tk = nsplit * t
    m_ref[...] = jnp.full(m_ref.shape, -jnp.inf, F32)
    l_ref[...] = jnp.zeros(l_ref.shape, F32)
    acc_ref[...] = jnp.zeros(acc_ref.shape, F32)

    def chain(hh, part, j, kw, masked):
        nb = kw // LANES
        start = pl.multiple_of(j * tk, tk)
        rows = slice(part * t, (part + 1) * t)
        k = k_ref[hh, pl.ds(start, kw), :]
        s = lax.dot_general(q_ref[hh, rows, :], k, (((1,), (1,)), ((), ())), preferred_element_type=F32)
        yield
        if masked:
            row = lax.broadcasted_iota(jnp.int32, (t, kw), 0) + part * t
            col = lax.broadcasted_iota(jnp.int32, (t, kw), 1)
            s = jnp.where(col <= row, s, MASK_VALUE)
        lane_blocks = lambda x: [x[:, b * LANES:(b + 1) * LANES] for b in range(nb)]
        m_prev = m_ref[hh, rows, :]
        m_cur = functools.reduce(jnp.maximum, lane_blocks(s))
        m_new = jnp.maximum(m_prev, jnp.max(m_cur, axis=-1, keepdims=True))
        alpha = jnp.exp2(m_prev - m_new)
        p = jnp.exp2(s - jnp.concatenate([m_new] * nb, axis=1))
        l_ref[hh, rows, :] = alpha * l_ref[hh, rows, :] + functools.reduce(jnp.add, lane_blocks(p))
        v = v_ref[hh, pl.ds(start, kw), :]
        acc_ref[hh, rows, :] = alpha * acc_ref[hh, rows, :] + _dot(p.astype(BF16), v)
        m_ref[hh, rows, :] = m_new

    def body(j, c):
        _run_phases([chain(hh, part, j, tk, False) for hh in range(hpb) for part in range(nsplit)])
        return c

    lax.fori_loop(0, i, body, 0)
    _run_phases([chain(hh, part, i, (part + 1) * t, True) for hh in range(hpb) for part in range(nsplit)])
    for hh in range(hpb):
        l = jnp.sum(l_ref[hh], axis=-1, keepdims=True)
        o_ref[:, hh * VDIM:(hh + 1) * VDIM] = (acc_ref[hh] / l).astype(BF16)


def attention(q, k, v, t=512, hpb=2, nsplit=2):
    S = q.shape[1]
    tq = nsplit * t
    return pl.pallas_call(
        functools.partial(_attn_kernel, t, hpb, nsplit),
        grid=(HEADS // hpb, S // tq),
        in_specs=[pl.BlockSpec((hpb, tq, QK_PAD), lambda h, i: (h, i, 0)),
                  pl.BlockSpec((hpb, S, QK_PAD), lambda h, i: (h, 0, 0)),
                  pl.BlockSpec((hpb, S, VDIM), lambda h, i: (h, 0, 0))],
        out_specs=pl.BlockSpec((tq, hpb * VDIM), lambda h, i: (i, h)),
        out_shape=jax.ShapeDtypeStruct((S, HEADS * VDIM), BF16),
        scratch_shapes=[pltpu.VMEM((hpb, tq, LANES), F32), pltpu.VMEM((hpb, tq, LANES), F32),
                        pltpu.VMEM((hpb, tq, VDIM), F32)],
        compiler_params=_cparams(("arbitrary", "arbitrary")),
        name="attention",
    )(q, k, v)


def _hgrn_kernel(nblk, hpb, q_ref, f_ref, v_ref, gate_ref, gn_ref, o_ref, st_ref, bs2_ref):
    @pl.when(pl.program_id(1) == 0)
    def _():
        st_ref[...] = jnp.zeros(st_ref.shape, F32)

    B = LANES
    r = lax.broadcasted_iota(jnp.int32, (B, B), 0)
    c = lax.broadcasted_iota(jnp.int32, (B, B), 1)
    same = (r // SUB) == (c // SUB)
    causal = same & (c <= r)
    incl = jnp.where(causal, 1.0, 0.0).astype(BF16)
    nsub = B // SUB
    row_sub = r // SUB
    col_sub = c // SUB
    lane = lax.broadcasted_iota(jnp.int32, (SUB, B), 1)
    ones = jnp.ones((HG_D, B), BF16)
    gn = gn_ref[...]

    def one_head(hh, rows):
        cols = slice(hh * HG_D, (hh + 1) * HG_D)
        f = f_ref[rows, cols]
        q = q_ref[rows, cols].astype(F32)
        g = jnp.log(jnp.maximum(f, MIN_FORGET))
        g0 = g.astype(BF16)
        r1 = g - g0.astype(F32)
        g1 = r1.astype(BF16)
        g2 = (r1 - g1.astype(F32)).astype(BF16)
        b3 = _dot(incl, jnp.concatenate([g0, g1, g2], axis=1))
        b = b3[:, :HG_D] + b3[:, HG_D:2 * HG_D] + b3[:, 2 * HG_D:]
        b_last = jnp.concatenate(
            [jnp.broadcast_to(b[(u + 1) * SUB - 1:(u + 1) * SUB], (SUB, HG_D)) for u in range(nsub)], axis=0)
        eb = jnp.exp(b)
        kin = jnp.maximum(1.0 - f, 0.0)
        qe = (q * eb).astype(BF16)
        kd = (kin * jnp.exp(b_last - b)).astype(BF16)
        vb = v_ref[rows, cols]
        b2 = b * LOG2E
        bs2_ref[hh] = b2 - jnp.log2(kin)
        yield
        ps = []
        sums = {}
        for u in range(nsub):
            lo = u * SUB
            q16 = q[lo:lo + SUB]
            b16 = b2[lo:lo + SUB]
            for s in range(SUB):
                p = q16 * jnp.exp2(b16 - bs2_ref[hh, pl.ds(lo + s, 1), :])
                if s % 2 == 0:
                    ps.append(p.astype(BF16))
                else:
                    sums[lo + s] = jnp.sum(p, axis=-1, keepdims=True)
        rs = _dot(jnp.concatenate(ps, axis=0), ones)
        yield
        for n, rr in enumerate(range(0, B, 2)):
            sums[rr] = rs[n * SUB:(n + 1) * SUB]
        a_rows = []
        for u in range(nsub):
            a = jnp.zeros((SUB, B), F32)
            for s in range(SUB):
                rr = u * SUB + s
                a = jnp.where(lane == rr, sums[rr], a)
            a_rows.append(a)
        a_blk = jnp.where(causal, jnp.concatenate(a_rows, axis=0), 0.0).astype(BF16)

        vt = jnp.transpose(vb)
        vt_u = jnp.concatenate([jnp.where(col_sub == u, vt, 0.0) for u in range(nsub)], axis=0)
        upd = _dot(vt_u, kd)
        yield
        st = st_ref[hh]
        sts = []
        for u in range(nsub):
            sts.append(st.astype(BF16))
            st = st * eb[(u + 1) * SUB - 1:(u + 1) * SUB] + upd[u * B:(u + 1) * B]
        st_ref[hh] = st
        qe_u = jnp.concatenate([jnp.where(row_sub == u, qe, 0.0) for u in range(nsub)], axis=1)
        o_int = lax.dot_general(qe_u, jnp.concatenate(sts, axis=1), (((1,), (1,)), ((), ())),
                                preferred_element_type=F32)
        o = _dot(a_blk, vb) + o_int
        o = _rms(o, gn) * gate_ref[rows, cols].astype(F32)
        o_ref[rows, cols] = o.astype(BF16)

    def blk_body(bi, carry):
        rows = pl.ds(pl.multiple_of(bi * B, B), B)
        _run_phases([one_head(hh, rows) for hh in range(hpb)])
        return carry

    lax.fori_loop(0, nblk, blk_body, 0)


def hgrn(layer, q, f, v, gate, gn, ts=1024, hpb=4):
    S = q.shape[0]
    w = hpb * HG_D
    blk = pl.BlockSpec((ts, w), lambda h, i: (i, h))
    return pl.pallas_call(
        functools.partial(_hgrn_kernel, ts // LANES, hpb),
        grid=(HEADS // hpb, S // ts),
        in_specs=[blk, blk, blk, blk, pl.BlockSpec((None, 1, HG_D), lambda h, i: (layer, 0, 0))],
        out_specs=blk,
        out_shape=jax.ShapeDtypeStruct((S, D_MODEL), BF16),
        scratch_shapes=[pltpu.VMEM((hpb, HG_D, HG_D), F32), pltpu.VMEM((hpb, LANES, HG_D), F32)],
        compiler_params=_cparams(("arbitrary", "arbitrary")),
        name="hgrn",
    )(q, f, v, gate, gn)


def _merge_kernel(a_ref, r_ref, ga_ref, gb_ref, wa_ref, wb_ref, wo_ref, x_ref, o_ref):
    @pl.when(pl.program_id(1) == 0)
    def _():
        o_ref[...] = x_ref[...]

    ma = _dot(a_ref[...], wa_ref[...])
    mb = _dot(r_ref[...], wb_ref[...])
    mg = (ga_ref[...].astype(F32) * ma + gb_ref[...].astype(F32) * mb).astype(BF16)
    o_ref[...] += _dot(mg, wo_ref[...])


def merge(layer, attn, rec, ga, gb, wa, wb, wo, x, tm=512, tn=512):
    S = x.shape[0]
    return pl.pallas_call(
        _merge_kernel,
        grid=(S // tm, D_MODEL // tn),
        in_specs=[pl.BlockSpec((tm, D_MODEL), lambda i, j: (i, 0)),
                  pl.BlockSpec((tm, D_MODEL), lambda i, j: (i, 0)),
                  pl.BlockSpec((tm, tn), lambda i, j: (i, j)),
                  pl.BlockSpec((tm, tn), lambda i, j: (i, j)),
                  pl.BlockSpec((None, D_MODEL, tn), lambda i, j: (layer, 0, j)),
                  pl.BlockSpec((None, D_MODEL, tn), lambda i, j: (layer, 0, j)),
                  pl.BlockSpec((None, tn, D_MODEL), lambda i, j: (layer, j, 0)),
                  pl.BlockSpec((tm, D_MODEL), lambda i, j: (i, 0))],
        out_specs=pl.BlockSpec((tm, D_MODEL), lambda i, j: (i, 0)),
        out_shape=jax.ShapeDtypeStruct((S, D_MODEL), F32),
        compiler_params=_cparams(("arbitrary", "arbitrary")),
        name="merge",
    )(attn, rec, ga, gb, wa, wb, wo, x)


def _ffn_kernel(final, nj, tm, x_ref, xh_ref, g_ref, wg_ref, wu_ref, cwg_ref, cwu_ref, cbg_ref, cbu_ref,
                wd_ref, fg_ref, o_ref, h_ref, ug_ref, uu_ref):
    m = pl.program_id(0)
    j = pl.program_id(1)

    @pl.when(j == 0)
    def _():
        g = g_ref[...]
        x = x_ref[...]
        o_ref[...] = x
        h_ref[HALO:, :] = _rms(x, g).astype(BF16)
        hh = jnp.where(m > 0, _rms(xh_ref[...], g), 0.0)
        h_ref[:HALO, :] = hh.astype(BF16)

    h = h_ref[...]
    ug_ref[...] = _dot(h, wg_ref[...])
    uu_ref[...] = _dot(h, wu_ref[...])

    def conv(u_ref, cw_ref, cb_ref):
        y = cb_ref[...]
        for tap in range(3):
            y = y + cw_ref[tap:tap + 1, :] * u_ref[pl.ds(HALO - 2 + tap, tm), :]
        return y

    yg = conv(ug_ref, cwg_ref, cbg_ref)
    yu = conv(uu_ref, cwu_ref, cbu_ref)
    act = (yg * jax.nn.sigmoid(yg) * yu).astype(BF16)
    o_ref[...] += _dot(act, wd_ref[...])

    if final:
        @pl.when(j == nj - 1)
        def _():
            o_ref[...] = _rms(o_ref[...], fg_ref[...])


def ffn(layer, x, g, w_up, conv_w, conv_b, w_down, fg, tm=1024, tn=512):
    S = x.shape[0]
    nj = D_FF // tn
    hb = tm // HALO
    final = layer == DEPTH - 1
    return pl.pallas_call(
        functools.partial(_ffn_kernel, final, nj, tm),
        grid=(S // tm, nj),
        in_specs=[pl.BlockSpec((tm, D_MODEL), lambda i, j: (i, 0)),
                  pl.BlockSpec((HALO, D_MODEL), lambda i, j: (jnp.maximum(i * hb - 1, 0), 0)),
                  pl.BlockSpec((None, 1, D_MODEL), lambda i, j: (layer, 0, 0)),
                  pl.BlockSpec((None, D_MODEL, tn), lambda i, j: (layer, 0, j)),
                  pl.BlockSpec((None, D_MODEL, tn), lambda i, j: (layer, 0, nj + j)),
                  pl.BlockSpec((None, 3, tn), lambda i, j: (layer, 0, j)),
                  pl.BlockSpec((None, 3, tn), lambda i, j: (layer, 0, nj + j)),
                  pl.BlockSpec((None, 1, tn), lambda i, j: (layer, 0, j)),
                  pl.BlockSpec((None, 1, tn), lambda i, j: (layer, 0, nj + j)),
                  pl.BlockSpec((None, tn, D_MODEL), lambda i, j: (layer, j, 0)),
                  pl.BlockSpec((1, D_MODEL), lambda i, j: (0, 0))],
        out_specs=pl.BlockSpec((tm, D_MODEL), lambda i, j: (i, 0), pipeline_mode=pl.Buffered(1)),
        out_shape=jax.ShapeDtypeStruct((S, D_MODEL), F32),
        scratch_shapes=[pltpu.VMEM((tm + HALO, D_MODEL), BF16),
                        pltpu.VMEM((tm + HALO, tn), F32),
                        pltpu.VMEM((tm + HALO, tn), F32)],
        compiler_params=_cparams(("arbitrary", "arbitrary")),
        name="ffn",
    )(x, x, g, w_up, w_up, conv_w, conv_w, conv_b, conv_b, w_down, fg)


def _rope_pair_cols(w):
    x1, x2 = w[..., :ROPE // 2], w[..., ROPE // 2:]
    z = jnp.zeros(w.shape[:-1] + (LANES - ROPE,), w.dtype)
    return jnp.concatenate([x1, x2, z], axis=-1), jnp.concatenate([x2, x1, z], axis=-1)


def _rope_pair_rows(wt):
    x1, x2 = wt[..., :ROPE // 2, :], wt[..., ROPE // 2:, :]
    z = jnp.zeros(wt.shape[:-2] + (LANES - ROPE, wt.shape[-1]), wt.dtype)
    return jnp.concatenate([x1, x2, z], axis=-2), jnp.concatenate([x2, x1, z], axis=-2)


def kernel(x, positions, attn_norm_g, w_in, q_norm_g, w_uq, kv_norm_g, w_ukv, hgrn_lb_logits, hgrn_out_norm_g,
           w_branch_a, w_branch_b, w_out, ffn_norm_g, w_up, conv_w, conv_b, w_down, final_norm_g):
    B, S, D = x.shape
    assert B == 1 and D == D_MODEL
    xs = x.reshape(S, D)

    inv_freq = ROPE_THETA ** (-jnp.arange(0, ROPE, 2, dtype=F32) / ROPE)
    zeros = jnp.zeros((LANES - ROPE,), F32)
    invf = jnp.concatenate([inv_freq, inv_freq, zeros]).reshape(1, LANES)
    half = jnp.ones((ROPE // 2,), F32)
    sgn = jnp.concatenate([-half, half, zeros]).reshape(1, LANES)
    cosT, sinT = rope_tables(positions.reshape(S, 1), invf, sgn)

    n_lat = Q_LORA + KV_LORA
    wt = jnp.swapaxes(w_in, 1, 2)
    wq = w_uq.reshape(DEPTH, Q_LORA, HEADS, NOPE + ROPE)
    qa, qb = _rope_pair_cols(wq[..., NOPE:])
    wq = jnp.concatenate([wq[..., :NOPE], qa, qb], axis=-1).reshape(DEPTH, Q_LORA, -1).astype(BF16)
    wkv = w_ukv.astype(BF16)
    wa, wb, wo = w_branch_a.astype(BF16), w_branch_b.astype(BF16), w_out.astype(BF16)
    wup, wdn = w_up.astype(BF16), w_down.astype(BF16)
    conv_b3 = conv_b.reshape(DEPTH, 1, 2 * D_FF)
    lb_logits = hgrn_lb_logits.astype(F32)
    fg = final_norm_g.reshape(1, D)

    rows3 = lambda a: a.reshape(DEPTH, 1, a.shape[-1])
    g_attn, g_q, g_kv, g_hg, g_ffn = map(rows3, (attn_norm_g, q_norm_g, kv_norm_g, hgrn_out_norm_g, ffn_norm_g))

    for l in range(DEPTH):
        cq, ckv, kpe, h = in1(l, xs, g_attn, wt, g_q, g_kv, cosT, sinT)
        hq, hf, hi, hg, ga, gb = in2(l, h, wt, n_lat + ROPE, lb_logits)
        q, k, v = qkv(l, cq, ckv, kpe, cosT, sinT, wq, wkv)
        attn = attention(q, k, v)
        rec = hgrn(l, hq, hf, hi, hg, g_hg)
        xs = merge(l, attn, rec, ga, gb, wa, wb, wo, xs)
        xs = ffn(l, xs, g_ffn, wup, conv_w, conv_b3, wdn, fg)
    return xs.reshape(B, S, D)
```

```python
import functools

import jax
import jax.numpy as jnp
from jax import lax
from jax.experimental import pallas as pl
from jax.experimental.pallas import tpu as pltpu

F32 = jnp.float32
BF16 = jnp.bfloat16

D_MODEL = 2048
DEPTH = 4
HEADS = 16
Q_LORA = 512
KV_LORA = 512
NOPE = 128
ROPE = 64
VDIM = 128
ROPE_THETA = 10000.0
MASK_VALUE = -1e30
HG_D = 128
SUB = 16
MIN_FORGET = 1e-30
D_FF = 5632
EPS = 1e-6
LOG2E = 1.4426950408889634
LANES = 128
QK_PAD = 256
HALO = 16

VMEM_LIMIT = 52 * 1024 * 1024


def _cparams(sem):
    return pltpu.CompilerParams(dimension_semantics=sem, vmem_limit_bytes=VMEM_LIMIT)


def _rms(x, g):
    return x * lax.rsqrt(jnp.mean(x * x, axis=-1, keepdims=True) + EPS) * g


def _dot(a, b):
    return jnp.dot(a, b, preferred_element_type=F32)


def _dot_t(a, bt):
    return lax.dot_general(a, bt, (((1,), (1,)), ((), ())), preferred_element_type=F32)


def _rope_kernel(pos_ref, invf_ref, sgn_ref, cos_ref, sin_ref):
    ang = pos_ref[...].astype(F32) * invf_ref[...]
    sgn = sgn_ref[...]
    cos_ref[...] = jnp.cos(ang) * jnp.abs(sgn)
    sin_ref[...] = jnp.sin(ang) * sgn


def rope_tables(pos_col, invf, sgn, tm=1024):
    S = pos_col.shape[0]
    return pl.pallas_call(
        _rope_kernel,
        grid=(S // tm,),
        in_specs=[pl.BlockSpec((tm, 1), lambda i: (i, 0)),
                  pl.BlockSpec((1, LANES), lambda i: (0, 0)),
                  pl.BlockSpec((1, LANES), lambda i: (0, 0))],
        out_specs=[pl.BlockSpec((tm, LANES), lambda i: (i, 0))] * 2,
        out_shape=[jax.ShapeDtypeStruct((S, LANES), F32)] * 2,
        compiler_params=_cparams(("arbitrary",)),
        name="rope_tables",
    )(pos_col, invf, sgn)


def _in1_kernel(x_ref, g_ref, wlat_ref, wrope_ref, qg_ref, kvg_ref, cos_ref, sin_ref,
                cq_ref, ckv_ref, kpe_ref, h_ref, w_ref):
    n_lat = Q_LORA + KV_LORA
    half = ROPE // 2

    @pl.when(pl.program_id(0) == 0)
    def _():
        w_ref[:n_lat, :] = wlat_ref[0].astype(BF16)
        r = wrope_ref[0].astype(BF16)
        x1, x2 = r[:half], r[half:]
        z = jnp.zeros((LANES - ROPE, D_MODEL), BF16)
        w_ref[n_lat:, :] = jnp.concatenate([x1, x2, z, x2, x1, z], axis=0)

    h = _rms(x_ref[...], g_ref[...]).astype(BF16)
    h_ref[...] = h
    y = _dot_t(h, w_ref[...])
    cq_ref[...] = _rms(y[:, :Q_LORA], qg_ref[...]).astype(BF16)
    ckv_ref[...] = _rms(y[:, Q_LORA:n_lat], kvg_ref[...]).astype(BF16)
    kpe = y[:, n_lat:n_lat + LANES] * cos_ref[...] + y[:, n_lat + LANES:] * sin_ref[...]
    kpe_ref[...] = kpe.astype(BF16)


def in1(layer, x, g, wt, qg, kvg, cosT, sinT, tm=512):
    S = x.shape[0]
    n_lat = Q_LORA + KV_LORA
    row = lambda i: (i, 0)
    lyr = lambda i: (layer, 0, 0)
    elem = lambda n: (pl.Element(1), pl.Element(n), pl.Element(D_MODEL))
    return pl.pallas_call(
        _in1_kernel,
        grid=(S // tm,),
        in_specs=[pl.BlockSpec((tm, D_MODEL), row), pl.BlockSpec((None, 1, D_MODEL), lyr),
                  pl.BlockSpec(elem(n_lat), lyr), pl.BlockSpec(elem(ROPE), lambda i: (layer, n_lat, 0)),
                  pl.BlockSpec((None, 1, Q_LORA), lyr),
                  pl.BlockSpec((None, 1, KV_LORA), lyr), pl.BlockSpec((tm, LANES), row),
                  pl.BlockSpec((tm, LANES), row)],
        out_specs=[pl.BlockSpec((tm, Q_LORA), row), pl.BlockSpec((tm, KV_LORA), row),
                   pl.BlockSpec((tm, LANES), row), pl.BlockSpec((tm, D_MODEL), row)],
        out_shape=[jax.ShapeDtypeStruct((S, Q_LORA), BF16), jax.ShapeDtypeStruct((S, KV_LORA), BF16),
                   jax.ShapeDtypeStruct((S, LANES), BF16), jax.ShapeDtypeStruct((S, D_MODEL), BF16)],
        scratch_shapes=[pltpu.VMEM((n_lat + 2 * LANES, D_MODEL), BF16)],
        compiler_params=_cparams(("arbitrary",)),
        name="in1",
    )(x, g, wt, wt, qg, kvg, cosT, sinT)


def _in2_kernel(layer, h_ref, wq_ref, wf_ref, wi_ref, wg_ref, wa_ref, wb_ref, lbl_ref,
                oq_ref, of_ref, oi_ref, og_ref, oa_ref, ob_ref, w_ref):
    @pl.when(pl.program_id(1) == 0)
    def _():
        for s, src in enumerate((wq_ref, wf_ref, wi_ref, wg_ref, wa_ref, wb_ref)):
            w_ref[s] = src[0].astype(BF16)

    h = h_ref[...]
    y = _dot_t(h, w_ref[0])
    oq_ref[...] = (y * jax.nn.sigmoid(y)).astype(BF16)
    y = _dot_t(h, w_ref[1])
    lg = lbl_ref[...]
    e = jnp.exp(lg - jnp.max(lg, axis=0, keepdims=True))
    p = e / jnp.sum(e, axis=0, keepdims=True)
    lb = jnp.sum(p[:layer + 1], axis=0, keepdims=True) - p[0:1]
    of_ref[...] = lb + (1.0 - lb) * jax.nn.sigmoid(y)
    y = _dot_t(h, w_ref[2])
    oi_ref[...] = y.astype(BF16)
    y = _dot_t(h, w_ref[3])
    og_ref[...] = (y * jax.nn.sigmoid(y)).astype(BF16)
    y = _dot_t(h, w_ref[4])
    oa_ref[...] = jax.nn.sigmoid(y).astype(BF16)
    y = _dot_t(h, w_ref[5])
    ob_ref[...] = jax.nn.sigmoid(y).astype(BF16)


def in2(layer, h, wt, off, lb_logits, tm=1024, tn=256):
    S = h.shape[0]
    nseg = (wt.shape[1] - off) // D_MODEL
    bps = D_MODEL // tn

    def seg_rows(s, j, i):
        return layer, pl.multiple_of(off + (s * bps + j) * tn, HALO), 0

    w_block = (pl.Element(1), pl.Element(tn), pl.Element(D_MODEL))
    w_specs = [pl.BlockSpec(w_block, functools.partial(seg_rows, s)) for s in range(nseg)]
    out_dtypes = [BF16, F32, BF16, BF16, BF16, BF16]
    return pl.pallas_call(
        functools.partial(_in2_kernel, layer),
        grid=(bps, S // tm),
        in_specs=[pl.BlockSpec((tm, D_MODEL), lambda j, i: (i, 0))] + w_specs +
                 [pl.BlockSpec((DEPTH, tn), lambda j, i: (0, j))],
        out_specs=[pl.BlockSpec((tm, tn), lambda j, i: (i, j))] * nseg,
        out_shape=[jax.ShapeDtypeStruct((S, D_MODEL), dt) for dt in out_dtypes],
        scratch_shapes=[pltpu.VMEM((nseg, tn, D_MODEL), BF16)],
        compiler_params=_cparams(("arbitrary", "arbitrary")),
        name="in2",
    )(h, *([wt] * nseg), lb_logits)


def _qkv_kernel(scale, cq_ref, ckv_ref, kpe_ref, cos_ref, sin_ref, wq_ref, wkv_ref, q_ref, k_ref, v_ref):
    cq = cq_ref[...]
    ckv = ckv_ref[...]
    cos = cos_ref[...]
    sin = sin_ref[...]
    kpe = kpe_ref[...]
    qw = NOPE + 2 * LANES
    kvw = NOPE + VDIM
    grp = 4
    for g in range(HEADS // grp):
        yg = _dot(cq, wq_ref[:, g * grp * qw:(g + 1) * grp * qw])
        zg = _dot(ckv, wkv_ref[:, g * grp * kvw:(g + 1) * grp * kvw])
        for hh in range(grp):
            h = g * grp + hh
            y = yg[:, hh * qw:(hh + 1) * qw]
            pe = y[:, NOPE:NOPE + LANES] * cos + y[:, NOPE + LANES:] * sin
            q_ref[h, :, :NOPE] = (y[:, :NOPE] * scale).astype(BF16)
            q_ref[h, :, NOPE:] = (pe * scale).astype(BF16)
            z = zg[:, hh * kvw:(hh + 1) * kvw]
            k_ref[h, :, :NOPE] = z[:, :NOPE].astype(BF16)
            k_ref[h, :, NOPE:] = kpe
            v_ref[h] = z[:, NOPE:].astype(BF16)


def qkv(layer, cq, ckv, kpe, cosT, sinT, wq, wkv, tm=512):
    S = cq.shape[0]
    scale = float((NOPE + ROPE) ** -0.5 * LOG2E)
    row = lambda i: (i, 0)
    return pl.pallas_call(
        functools.partial(_qkv_kernel, scale),
        grid=(S // tm,),
        in_specs=[pl.BlockSpec((tm, Q_LORA), row), pl.BlockSpec((tm, KV_LORA), row),
                  pl.BlockSpec((tm, LANES), row), pl.BlockSpec((tm, LANES), row),
                  pl.BlockSpec((tm, LANES), row),
                  pl.BlockSpec((None,) + wq.shape[1:], lambda i: (layer, 0, 0)),
                  pl.BlockSpec((None,) + wkv.shape[1:], lambda i: (layer, 0, 0))],
        out_specs=[pl.BlockSpec((HEADS, tm, QK_PAD), lambda i: (0, i, 0)),
                   pl.BlockSpec((HEADS, tm, QK_PAD), lambda i: (0, i, 0)),
                   pl.BlockSpec((HEADS, tm, VDIM), lambda i: (0, i, 0))],
        out_shape=[jax.ShapeDtypeStruct((HEADS, S, QK_PAD), BF16),
                   jax.ShapeDtypeStruct((HEADS, S, QK_PAD), BF16),
                   jax.ShapeDtypeStruct((HEADS, S, VDIM), BF16)],
        compiler_params=_cparams(("arbitrary",)),
        name="qkv",
    )(cq, ckv, kpe, cosT, sinT, wq, wkv)


def _run_phases(gens):
    live = list(gens)
    while live:
        live = [g for g in live if next(g, live) is not live]


def _attn_kernel(t, hpb, nsplit, q_ref, k_ref, v_ref, o_ref, m_ref, l_ref, acc_ref):
    i = pl.program_id(1)
    tk = nsplit * t
    m_ref[...] = jnp.full(m_ref.shape, -jnp.inf, F32)
    l_ref[...] = jnp.zeros(l_ref.shape, F32)
    acc_ref[...] = jnp.zeros(acc_ref.shape, F32)

    def chain(hh, part, j, kw, masked):
        nb = kw // LANES
        start = pl.multiple_of(j * tk, tk)
        rows = slice(part * t, (part + 1) * t)
        k = k_ref[hh, pl.ds(start, kw), :]
        s = lax.dot_general(q_ref[hh, rows, :], k, (((1,), (1,)), ((), ())), preferred_element_type=F32)
        yield
        if masked:
            row = lax.broadcasted_iota(jnp.int32, (t, kw), 0) + part * t
            col = lax.broadcasted_iota(jnp.int32, (t, kw), 1)
            s = jnp.where(col <= row, s, MASK_VALUE)
        lane_blocks = lambda x: [x[:, b * LANES:(b + 1) * LANES] for b in range(nb)]
        m_prev = m_ref[hh, rows, :]
        m_cur = functools.reduce(jnp.maximum, lane_blocks(s))
        m_new = jnp.maximum(m_prev, jnp.max(m_cur, axis=-1, keepdims=True))
        alpha = jnp.exp2(m_prev - m_new)
        p = jnp.exp2(s - jnp.concatenate([m_new] * nb, axis=1))
        l_ref[hh, rows, :] = alpha * l_ref[hh, rows, :] + functools.reduce(jnp.add, lane_blocks(p))
        v = v_ref[hh, pl.ds(start, kw), :]
        acc_ref[hh, rows, :] = alpha * acc_ref[hh, rows, :] + _dot(p.astype(BF16), v)
        m_ref[hh, rows, :] = m_new

    def body(j, c):
        _run_phases([chain(hh, part, j, tk, False) for hh in range(hpb) for part in range(nsplit)])
        return c

    lax.fori_loop(0, i, body, 0)
    _run_phases([chain(hh, part, i, (part + 1) * t, True) for hh in range(hpb) for part in range(nsplit)])
    for hh in range(hpb):
        l = jnp.sum(l_ref[hh], axis=-1, keepdims=True)
        o_ref[:, hh * VDIM:(hh + 1) * VDIM] = (acc_ref[hh] / l).astype(BF16)


def attention(q, k, v, t=512, hpb=2, nsplit=2):
    S = q.shape[1]
    tq = nsplit * t
    return pl.pallas_call(
        functools.partial(_attn_kernel, t, hpb, nsplit),
        grid=(HEADS // hpb, S // tq),
        in_specs=[pl.BlockSpec((hpb, tq, QK_PAD), lambda h, i: (h, i, 0)),
                  pl.BlockSpec((hpb, S, QK_PAD), lambda h, i: (h, 0, 0)),
                  pl.BlockSpec((hpb, S, VDIM), lambda h, i: (h, 0, 0))],
        out_specs=pl.BlockSpec((tq, hpb * VDIM), lambda h, i: (i, h)),
        out_shape=jax.ShapeDtypeStruct((S, HEADS * VDIM), BF16),
        scratch_shapes=[pltpu.VMEM((hpb, tq, LANES), F32), pltpu.VMEM((hpb, tq, LANES), F32),
                        pltpu.VMEM((hpb, tq, VDIM), F32)],
        compiler_params=_cparams(("arbitrary", "arbitrary")),
        name="attention",
    )(q, k, v)


def _hgrn_kernel(nblk, hpb, q_ref, f_ref, v_ref, gate_ref, gn_ref, o_ref, st_ref, bs2_ref):
    @pl.when(pl.program_id(1) == 0)
    def _():
        st_ref[...] = jnp.zeros(st_ref.shape, F32)

    B = LANES
    r = lax.broadcasted_iota(jnp.int32, (B, B), 0)
    c = lax.broadcasted_iota(jnp.int32, (B, B), 1)
    same = (r // SUB) == (c // SUB)
    causal = same & (c <= r)
    incl = jnp.where(causal, 1.0, 0.0).astype(BF16)
    nsub = B // SUB
    row_sub = r // SUB
    col_sub = c // SUB
    lane8 = lax.broadcasted_iota(jnp.int32, (SUB // 2, B), 1)
    ones = jnp.ones((HG_D, B), BF16)
    gn = gn_ref[...]

    def one_head(hh, rows):
        cols = slice(hh * HG_D, (hh + 1) * HG_D)
        f = f_ref[rows, cols]
        q = q_ref[rows, cols].astype(F32)
        g = jnp.log(jnp.maximum(f, MIN_FORGET))
        g0 = g.astype(BF16)
        r1 = g - g0.astype(F32)
        g1 = r1.astype(BF16)
        g2 = (r1 - g1.astype(F32)).astype(BF16)
        b3 = _dot(incl, jnp.concatenate([g0, g1, g2], axis=1))
        b = b3[:, :HG_D] + b3[:, HG_D:2 * HG_D] + b3[:, 2 * HG_D:]
        b_last = jnp.concatenate(
            [jnp.broadcast_to(b[(u + 1) * SUB - 1:(u + 1) * SUB], (SUB, HG_D)) for u in range(nsub)], axis=0)
        eb = jnp.exp(b)
        kin = jnp.maximum(1.0 - f, 0.0)
        qe = (q * eb).astype(BF16)
        kd = (kin * jnp.exp(b_last - b)).astype(BF16)
        vb = v_ref[rows, cols]
        b2 = b * LOG2E
        bs2_ref[hh] = b2 - jnp.log2(kin)
        yield
        hs = SUB // 2
        ps = []
        sums = {}
        for u in range(nsub):
            lo = u * SUB
            q16 = q[lo:lo + SUB]
            b16 = b2[lo:lo + SUB]
            pending = None
            for s in range(SUB):
                upper = s >= hs
                qq, bb = (q16[hs:], b16[hs:]) if upper else (q16, b16)
                p = qq * jnp.exp2(bb - bs2_ref[hh, pl.ds(lo + s, 1), :])
                if s % 2 == 1:
                    sums[lo + s] = (upper, jnp.sum(p, axis=-1, keepdims=True))
                elif not upper:
                    ps.append((p.astype(BF16), [(lo + s, 0, SUB)]))
                elif pending is None:
                    pending = (lo + s, p)
                else:
                    both = jnp.concatenate([pending[1], p], axis=0).astype(BF16)
                    ps.append((both, [(pending[0], 0, hs), (lo + s, hs, hs)]))
                    pending = None
        rs = _dot(jnp.concatenate([piece for piece, _ in ps], axis=0), ones)
        yield
        for n, (_, parts) in enumerate(ps):
            for rr, first, count in parts:
                sums[rr] = (count == hs, rs[n * SUB + first:n * SUB + first + count])
        a_rows = []
        for u in range(nsub):
            a_lo = jnp.zeros((hs, B), F32)
            a_hi = jnp.zeros((hs, B), F32)
            for s in range(SUB):
                rr = u * SUB + s
                upper, val = sums[rr]
                if upper:
                    a_hi = jnp.where(lane8 == rr, val, a_hi)
                else:
                    a_lo = jnp.where(lane8 == rr, val[:hs], a_lo)
                    a_hi = jnp.where(lane8 == rr, val[hs:], a_hi)
            a_rows.append(jnp.concatenate([a_lo, a_hi], axis=0))
        a_blk = jnp.where(causal, jnp.concatenate(a_rows, axis=0), 0.0).astype(BF16)

        vt = jnp.transpose(vb)
        vt_u = jnp.concatenate([jnp.where(col_sub == u, vt, 0.0) for u in range(nsub)], axis=0)
        upd = _dot(vt_u, kd)
        yield
        st = st_ref[hh]
        sts = []
        for u in range(nsub):
            sts.append(st.astype(BF16))
            st = st * eb[(u + 1) * SUB - 1:(u + 1) * SUB] + upd[u * B:(u + 1) * B]
        st_ref[hh] = st
        qe_u = jnp.concatenate([jnp.where(row_sub == u, qe, 0.0) for u in range(nsub)], axis=1)
        o_int = lax.dot_general(qe_u, jnp.concatenate(sts, axis=1), (((1,), (1,)), ((), ())),
                                preferred_element_type=F32)
        o = _dot(a_blk, vb) + o_int
        o = _rms(o, gn) * gate_ref[rows, cols].astype(F32)
        o_ref[rows, cols] = o.astype(BF16)

    def blk_body(bi, carry):
        rows = pl.ds(pl.multiple_of(bi * B, B), B)
        _run_phases([one_head(hh, rows) for hh in range(hpb)])
        return carry

    lax.fori_loop(0, nblk, blk_body, 0)


def hgrn(layer, q, f, v, gate, gn, ts=1024, hpb=8):
    S = q.shape[0]
    w = hpb * HG_D
    blk = pl.BlockSpec((ts, w), lambda h, i: (i, h))
    return pl.pallas_call(
        functools.partial(_hgrn_kernel, ts // LANES, hpb),
        grid=(HEADS // hpb, S // ts),
        in_specs=[blk, blk, blk, blk, pl.BlockSpec((None, 1, HG_D), lambda h, i: (layer, 0, 0))],
        out_specs=blk,
        out_shape=jax.ShapeDtypeStruct((S, D_MODEL), BF16),
        scratch_shapes=[pltpu.VMEM((hpb, HG_D, HG_D), F32), pltpu.VMEM((hpb, LANES, HG_D), F32)],
        compiler_params=_cparams(("arbitrary", "arbitrary")),
        name="hgrn",
    )(q, f, v, gate, gn)


def _merge_kernel(a_ref, r_ref, ga_ref, gb_ref, wa_ref, wb_ref, wo_ref, x_ref, o_ref):
    @pl.when(pl.program_id(1) == 0)
    def _():
        o_ref[...] = x_ref[...]

    ma = _dot(a_ref[...], wa_ref[...])
    mb = _dot(r_ref[...], wb_ref[...])
    mg = (ga_ref[...].astype(F32) * ma + gb_ref[...].astype(F32) * mb).astype(BF16)
    o_ref[...] += _dot(mg, wo_ref[...])


def merge(layer, attn, rec, ga, gb, wa, wb, wo, x, tm=512, tn=512):
    S = x.shape[0]
    return pl.pallas_call(
        _merge_kernel,
        grid=(S // tm, D_MODEL // tn),
        in_specs=[pl.BlockSpec((tm, D_MODEL), lambda i, j: (i, 0)),
                  pl.BlockSpec((tm, D_MODEL), lambda i, j: (i, 0)),
                  pl.BlockSpec((tm, tn), lambda i, j: (i, j)),
                  pl.BlockSpec((tm, tn), lambda i, j: (i, j)),
                  pl.BlockSpec((None, D_MODEL, tn), lambda i, j: (layer, 0, j)),
                  pl.BlockSpec((None, D_MODEL, tn), lambda i, j: (layer, 0, j)),
                  pl.BlockSpec((None, tn, D_MODEL), lambda i, j: (layer, j, 0)),
                  pl.BlockSpec((tm, D_MODEL), lambda i, j: (i, 0))],
        out_specs=pl.BlockSpec((tm, D_MODEL), lambda i, j: (i, 0)),
        out_shape=jax.ShapeDtypeStruct((S, D_MODEL), F32),
        compiler_params=_cparams(("arbitrary", "arbitrary")),
        name="merge",
    )(attn, rec, ga, gb, wa, wb, wo, x)


def _ffn_kernel(final, nj, tm, x_ref, xh_ref, g_ref, wg_ref, wu_ref, cwg_ref, cwu_ref, cbg_ref, cbu_ref,
                wd_ref, fg_ref, o_ref, h_ref, ug_ref, uu_ref):
    m = pl.program_id(0)
    j = pl.program_id(1)

    @pl.when(j == 0)
    def _():
        g = g_ref[...]
        x = x_ref[...]
        o_ref[...] = x
        h_ref[HALO:, :] = _rms(x, g).astype(BF16)
        hh = jnp.where(m > 0, _rms(xh_ref[...], g), 0.0)
        h_ref[:HALO, :] = hh.astype(BF16)

    h = h_ref[...]
    ug_ref[...] = _dot(h, wg_ref[...])
    uu_ref[...] = _dot(h, wu_ref[...])

    def conv(u_ref, cw_ref, cb_ref):
        y = cb_ref[...]
        for tap in range(3):
            y = y + cw_ref[tap:tap + 1, :] * u_ref[pl.ds(HALO - 2 + tap, tm), :]
        return y

    yg = conv(ug_ref, cwg_ref, cbg_ref)
    yu = conv(uu_ref, cwu_ref, cbu_ref)
    act = (yg * jax.nn.sigmoid(yg) * yu).astype(BF16)
    o_ref[...] += _dot(act, wd_ref[...])

    if final:
        @pl.when(j == nj - 1)
        def _():
            o_ref[...] = _rms(o_ref[...], fg_ref[...])


def ffn(layer, x, g, w_up, conv_w, conv_b, w_down, fg, tm=1024, tn=512):
    S = x.shape[0]
    nj = D_FF // tn
    hb = tm // HALO
    final = layer == DEPTH - 1
    return pl.pallas_call(
        functools.partial(_ffn_kernel, final, nj, tm),
        grid=(S // tm, nj),
        in_specs=[pl.BlockSpec((tm, D_MODEL), lambda i, j: (i, 0)),
                  pl.BlockSpec((HALO, D_MODEL), lambda i, j: (jnp.maximum(i * hb - 1, 0), 0)),
                  pl.BlockSpec((None, 1, D_MODEL), lambda i, j: (layer, 0, 0)),
                  pl.BlockSpec((None, D_MODEL, tn), lambda i, j: (layer, 0, j)),
                  pl.BlockSpec((None, D_MODEL, tn), lambda i, j: (layer, 0, nj + j)),
                  pl.BlockSpec((None, 3, tn), lambda i, j: (layer, 0, j)),
                  pl.BlockSpec((None, 3, tn), lambda i, j: (layer, 0, nj + j)),
                  pl.BlockSpec((None, 1, tn), lambda i, j: (layer, 0, j)),
                  pl.BlockSpec((None, 1, tn), lambda i, j: (layer, 0, nj + j)),
                  pl.BlockSpec((None, tn, D_MODEL), lambda i, j: (layer, j, 0)),
                  pl.BlockSpec((1, D_MODEL), lambda i, j: (0, 0))],
        out_specs=pl.BlockSpec((tm, D_MODEL), lambda i, j: (i, 0), pipeline_mode=pl.Buffered(1)),
        out_shape=jax.ShapeDtypeStruct((S, D_MODEL), F32),
        scratch_shapes=[pltpu.VMEM((tm + HALO, D_MODEL), BF16),
                        pltpu.VMEM((tm + HALO, tn), F32),
                        pltpu.VMEM((tm + HALO, tn), F32)],
        compiler_params=_cparams(("arbitrary", "arbitrary")),
        name="ffn",
    )(x, x, g, w_up, w_up, conv_w, conv_w, conv_b, conv_b, w_down, fg)


def _rope_pair_cols(w):
    x1, x2 = w[..., :ROPE // 2], w[..., ROPE // 2:]
    z = jnp.zeros(w.shape[:-1] + (LANES - ROPE,), w.dtype)
    return jnp.concatenate([x1, x2, z], axis=-1), jnp.concatenate([x2, x1, z], axis=-1)


def _rope_pair_rows(wt):
    x1, x2 = wt[..., :ROPE // 2, :], wt[..., ROPE // 2:, :]
    z = jnp.zeros(wt.shape[:-2] + (LANES - ROPE, wt.shape[-1]), wt.dtype)
    return jnp.concatenate([x1, x2, z], axis=-2), jnp.concatenate([x2, x1, z], axis=-2)


def kernel(x, positions, attn_norm_g, w_in, q_norm_g, w_uq, kv_norm_g, w_ukv, hgrn_lb_logits, hgrn_out_norm_g,
           w_branch_a, w_branch_b, w_out, ffn_norm_g, w_up, conv_w, conv_b, w_down, final_norm_g):
    B, S, D = x.shape
    assert B == 1 and D == D_MODEL
    xs = x.reshape(S, D)

    inv_freq = ROPE_THETA ** (-jnp.arange(0, ROPE, 2, dtype=F32) / ROPE)
    zeros = jnp.zeros((LANES - ROPE,), F32)
    invf = jnp.concatenate([inv_freq, inv_freq, zeros]).reshape(1, LANES)
    half = jnp.ones((ROPE // 2,), F32)
    sgn = jnp.concatenate([-half, half, zeros]).reshape(1, LANES)
    cosT, sinT = rope_tables(positions.reshape(S, 1), invf, sgn)

    n_lat = Q_LORA + KV_LORA
    wt = jnp.swapaxes(w_in, 1, 2)
    wq = w_uq.reshape(DEPTH, Q_LORA, HEADS, NOPE + ROPE)
    qa, qb = _rope_pair_cols(wq[..., NOPE:])
    wq = jnp.concatenate([wq[..., :NOPE], qa, qb], axis=-1).reshape(DEPTH, Q_LORA, -1).astype(BF16)
    wkv = w_ukv.astype(BF16)
    wa, wb, wo = w_branch_a.astype(BF16), w_branch_b.astype(BF16), w_out.astype(BF16)
    wup, wdn = w_up.astype(BF16), w_down.astype(BF16)
    conv_b3 = conv_b.reshape(DEPTH, 1, 2 * D_FF)
    lb_logits = hgrn_lb_logits.astype(F32)
    fg = final_norm_g.reshape(1, D)

    rows3 = lambda a: a.reshape(DEPTH, 1, a.shape[-1])
    g_attn, g_q, g_kv, g_hg, g_ffn = map(rows3, (attn_norm_g, q_norm_g, kv_norm_g, hgrn_out_norm_g, ffn_norm_g))

    for l in range(DEPTH):
        cq, ckv, kpe, h = in1(l, xs, g_attn, wt, g_q, g_kv, cosT, sinT)
        hq, hf, hi, hg, ga, gb = in2(l, h, wt, n_lat + ROPE, lb_logits)
        q, k, v = qkv(l, cq, ckv, kpe, cosT, sinT, wq, wkv)
        attn = attention(q, k, v)
        rec = hgrn(l, hq, hf, hi, hg, g_hg)
        xs = merge(l, attn, rec, ga, gb, wa, wb, wo, xs)
        xs = ffn(l, xs, g_ffn, wup, conv_w, conv_b3, wdn, fg)
    return xs.reshape(B, S, D)
```

```python
import functools

import jax
import jax.numpy as jnp
from jax import lax
from jax.experimental import pallas as pl
from jax.experimental.pallas import tpu as pltpu

F32 = jnp.float32
BF16 = jnp.bfloat16

D_MODEL = 2048
DEPTH = 4
HEADS = 16
Q_LORA = 512
KV_LORA = 512
NOPE = 128
ROPE = 64
VDIM = 128
ROPE_THETA = 10000.0
MASK_VALUE = -1e30
HG_D = 128
SUB = 16
MIN_FORGET = 1e-30
D_FF = 5632
EPS = 1e-6
LOG2E = 1.4426950408889634
LANES = 128
QK_PAD = 256
HALO = 16

VMEM_LIMIT = 52 * 1024 * 1024


def _cparams(sem):
    return pltpu.CompilerParams(dimension_semantics=sem, vmem_limit_bytes=VMEM_LIMIT)


def _rms(x, g):
    return x * lax.rsqrt(jnp.mean(x * x, axis=-1, keepdims=True) + EPS) * g


def _dot(a, b):
    return jnp.dot(a, b, preferred_element_type=F32)


def _dot_t(a, bt):
    return lax.dot_general(a, bt, (((1,), (1,)), ((), ())), preferred_element_type=F32)


def _rope_kernel(pos_ref, invf_ref, sgn_ref, cos_ref, sin_ref):
    ang = pos_ref[...].astype(F32) * invf_ref[...]
    sgn = sgn_ref[...]
    cos_ref[...] = jnp.cos(ang) * jnp.abs(sgn)
    sin_ref[...] = jnp.sin(ang) * sgn


def rope_tables(pos_col, invf, sgn, tm=1024):
    S = pos_col.shape[0]
    return pl.pallas_call(
        _rope_kernel,
        grid=(S // tm,),
        in_specs=[pl.BlockSpec((tm, 1), lambda i: (i, 0)),
                  pl.BlockSpec((1, LANES), lambda i: (0, 0)),
                  pl.BlockSpec((1, LANES), lambda i: (0, 0))],
        out_specs=[pl.BlockSpec((tm, LANES), lambda i: (i, 0))] * 2,
        out_shape=[jax.ShapeDtypeStruct((S, LANES), F32)] * 2,
        compiler_params=_cparams(("arbitrary",)),
        name="rope_tables",
    )(pos_col, invf, sgn)


def _in1_kernel(x_ref, g_ref, wlat_ref, wrope_ref, qg_ref, kvg_ref, cos_ref, sin_ref,
                cq_ref, ckv_ref, kpe_ref, h_ref, w_ref):
    n_lat = Q_LORA + KV_LORA
    half = ROPE // 2

    @pl.when(pl.program_id(0) == 0)
    def _():
        w_ref[:n_lat, :] = wlat_ref[0].astype(BF16)
        r = wrope_ref[0].astype(BF16)
        x1, x2 = r[:half], r[half:]
        z = jnp.zeros((LANES - ROPE, D_MODEL), BF16)
        w_ref[n_lat:, :] = jnp.concatenate([x1, x2, z, x2, x1, z], axis=0)

    h = _rms(x_ref[...], g_ref[...]).astype(BF16)
    h_ref[...] = h
    y = _dot_t(h, w_ref[...])
    cq_ref[...] = _rms(y[:, :Q_LORA], qg_ref[...]).astype(BF16)
    ckv_ref[...] = _rms(y[:, Q_LORA:n_lat], kvg_ref[...]).astype(BF16)
    kpe = y[:, n_lat:n_lat + LANES] * cos_ref[...] + y[:, n_lat + LANES:] * sin_ref[...]
    kpe_ref[...] = kpe.astype(BF16)


def in1(layer, x, g, wt, qg, kvg, cosT, sinT, tm=512):
    S = x.shape[0]
    n_lat = Q_LORA + KV_LORA
    row = lambda i: (i, 0)
    lyr = lambda i: (layer, 0, 0)
    elem = lambda n: (pl.Element(1), pl.Element(n), pl.Element(D_MODEL))
    return pl.pallas_call(
        _in1_kernel,
        grid=(S // tm,),
        in_specs=[pl.BlockSpec((tm, D_MODEL), row), pl.BlockSpec((None, 1, D_MODEL), lyr),
                  pl.BlockSpec(elem(n_lat), lyr), pl.BlockSpec(elem(ROPE), lambda i: (layer, n_lat, 0)),
                  pl.BlockSpec((None, 1, Q_LORA), lyr),
                  pl.BlockSpec((None, 1, KV_LORA), lyr), pl.BlockSpec((tm, LANES), row),
                  pl.BlockSpec((tm, LANES), row)],
        out_specs=[pl.BlockSpec((tm, Q_LORA), row), pl.BlockSpec((tm, KV_LORA), row),
                   pl.BlockSpec((tm, LANES), row), pl.BlockSpec((tm, D_MODEL), row)],
        out_shape=[jax.ShapeDtypeStruct((S, Q_LORA), BF16), jax.ShapeDtypeStruct((S, KV_LORA), BF16),
                   jax.ShapeDtypeStruct((S, LANES), BF16), jax.ShapeDtypeStruct((S, D_MODEL), BF16)],
        scratch_shapes=[pltpu.VMEM((n_lat + 2 * LANES, D_MODEL), BF16)],
        compiler_params=_cparams(("arbitrary",)),
        name="in1",
    )(x, g, wt, wt, qg, kvg, cosT, sinT)


def _in2_kernel(layer, h_ref, wq_ref, wf_ref, wi_ref, wg_ref, wa_ref, wb_ref, lbl_ref,
                oq_ref, of_ref, oi_ref, og_ref, oa_ref, ob_ref, w_ref):
    @pl.when(pl.program_id(1) == 0)
    def _():
        for s, src in enumerate((wq_ref, wf_ref, wi_ref, wg_ref, wa_ref, wb_ref)):
            w_ref[s] = src[0].astype(BF16)

    h = h_ref[...]
    y = _dot_t(h, w_ref[0])
    oq_ref[...] = (y * jax.nn.sigmoid(y)).astype(BF16)
    y = _dot_t(h, w_ref[1])
    lg = lbl_ref[...]
    e = jnp.exp(lg - jnp.max(lg, axis=0, keepdims=True))
    p = e / jnp.sum(e, axis=0, keepdims=True)
    lb = jnp.sum(p[:layer + 1], axis=0, keepdims=True) - p[0:1]
    of_ref[...] = lb + (1.0 - lb) * jax.nn.sigmoid(y)
    y = _dot_t(h, w_ref[2])
    oi_ref[...] = y.astype(BF16)
    y = _dot_t(h, w_ref[3])
    og_ref[...] = (y * jax.nn.sigmoid(y)).astype(BF16)
    y = _dot_t(h, w_ref[4])
    oa_ref[...] = jax.nn.sigmoid(y).astype(BF16)
    y = _dot_t(h, w_ref[5])
    ob_ref[...] = jax.nn.sigmoid(y).astype(BF16)


def in2(layer, h, wt, off, lb_logits, tm=1024, tn=256):
    S = h.shape[0]
    nseg = (wt.shape[1] - off) // D_MODEL
    bps = D_MODEL // tn

    def seg_rows(s, j, i):
        return layer, pl.multiple_of(off + (s * bps + j) * tn, HALO), 0

    w_block = (pl.Element(1), pl.Element(tn), pl.Element(D_MODEL))
    w_specs = [pl.BlockSpec(w_block, functools.partial(seg_rows, s)) for s in range(nseg)]
    out_dtypes = [BF16, F32, BF16, BF16, BF16, BF16]
    return pl.pallas_call(
        functools.partial(_in2_kernel, layer),
        grid=(bps, S // tm),
        in_specs=[pl.BlockSpec((tm, D_MODEL), lambda j, i: (i, 0))] + w_specs +
                 [pl.BlockSpec((DEPTH, tn), lambda j, i: (0, j))],
        out_specs=[pl.BlockSpec((tm, tn), lambda j, i: (i, j))] * nseg,
        out_shape=[jax.ShapeDtypeStruct((S, D_MODEL), dt) for dt in out_dtypes],
        scratch_shapes=[pltpu.VMEM((nseg, tn, D_MODEL), BF16)],
        compiler_params=_cparams(("arbitrary", "arbitrary")),
        name="in2",
    )(h, *([wt] * nseg), lb_logits)


def _qkv_kernel(scale, cq_ref, ckv_ref, kpe_ref, cos_ref, sin_ref, wq_ref, wkv_ref, q_ref, k_ref, v_ref):
    cq = cq_ref[...]
    ckv = ckv_ref[...]
    cos = cos_ref[...]
    sin = sin_ref[...]
    kpe = kpe_ref[...]
    qw = NOPE + 2 * LANES
    kvw = NOPE + VDIM
    grp = 4
    for g in range(HEADS // grp):
        yg = _dot(cq, wq_ref[:, g * grp * qw:(g + 1) * grp * qw])
        zg = _dot(ckv, wkv_ref[:, g * grp * kvw:(g + 1) * grp * kvw])
        for hh in range(grp):
            h = g * grp + hh
            y = yg[:, hh * qw:(hh + 1) * qw]
            pe = y[:, NOPE:NOPE + LANES] * cos + y[:, NOPE + LANES:] * sin
            q_ref[h, :, :NOPE] = (y[:, :NOPE] * scale).astype(BF16)
            q_ref[h, :, NOPE:] = (pe * scale).astype(BF16)
            z = zg[:, hh * kvw:(hh + 1) * kvw]
            k_ref[h, :, :NOPE] = z[:, :NOPE].astype(BF16)
            k_ref[h, :, NOPE:] = kpe
            v_ref[h] = z[:, NOPE:].astype(BF16)


def qkv(layer, cq, ckv, kpe, cosT, sinT, wq, wkv, tm=512):
    S = cq.shape[0]
    scale = float((NOPE + ROPE) ** -0.5 * LOG2E)
    row = lambda i: (i, 0)
    return pl.pallas_call(
        functools.partial(_qkv_kernel, scale),
        grid=(S // tm,),
        in_specs=[pl.BlockSpec((tm, Q_LORA), row), pl.BlockSpec((tm, KV_LORA), row),
                  pl.BlockSpec((tm, LANES), row), pl.BlockSpec((tm, LANES), row),
                  pl.BlockSpec((tm, LANES), row),
                  pl.BlockSpec((None,) + wq.shape[1:], lambda i: (layer, 0, 0)),
                  pl.BlockSpec((None,) + wkv.shape[1:], lambda i: (layer, 0, 0))],
        out_specs=[pl.BlockSpec((HEADS, tm, QK_PAD), lambda i: (0, i, 0)),
                   pl.BlockSpec((HEADS, tm, QK_PAD), lambda i: (0, i, 0)),
                   pl.BlockSpec((HEADS, tm, VDIM), lambda i: (0, i, 0))],
        out_shape=[jax.ShapeDtypeStruct((HEADS, S, QK_PAD), BF16),
                   jax.ShapeDtypeStruct((HEADS, S, QK_PAD), BF16),
                   jax.ShapeDtypeStruct((HEADS, S, VDIM), BF16)],
        compiler_params=_cparams(("arbitrary",)),
        name="qkv",
    )(cq, ckv, kpe, cosT, sinT, wq, wkv)


def _run_phases(gens):
    live = list(gens)
    while live:
        live = [g for g in live if next(g, live) is not live]


def _attn_kernel(t, hpb, nsplit, q_ref, k_ref, v_ref, o_ref, m_ref, l_ref, acc_ref):
    i = pl.program_id(1)
    tk = nsplit * t
    m_ref[...] = jnp.full(m_ref.shape, -jnp.inf, F32)
    l_ref[...] = jnp.zeros(l_ref.shape, F32)
    acc_ref[...] = jnp.zeros(acc_ref.shape, F32)

    def chain(hh, part, j, kw, masked):
        nb = kw // LANES
        start = pl.multiple_of(j * tk, tk)
        rows = slice(part * t, (part + 1) * t)
        k = k_ref[hh, pl.ds(start, kw), :]
        s = lax.dot_general(q_ref[hh, rows, :], k, (((1,), (1,)), ((), ())), preferred_element_type=F32)
        yield
        if masked:
            row = lax.broadcasted_iota(jnp.int32, (t, kw), 0) + part * t
            col = lax.broadcasted_iota(jnp.int32, (t, kw), 1)
            s = jnp.where(col <= row, s, MASK_VALUE)
        lane_blocks = lambda x: [x[:, b * LANES:(b + 1) * LANES] for b in range(nb)]
        m_prev = m_ref[hh, rows, :]
        m_cur = functools.reduce(jnp.maximum, lane_blocks(s))
        m_new = jnp.maximum(m_prev, jnp.max(m_cur, axis=-1, keepdims=True))
        alpha = jnp.exp2(m_prev - m_new)
        p = jnp.exp2(s - jnp.concatenate([m_new] * nb, axis=1))
        l_ref[hh, rows, :] = alpha * l_ref[hh, rows, :] + functools.reduce(jnp.add, lane_blocks(p))
        v = v_ref[hh, pl.ds(start, kw), :]
        acc_ref[hh, rows, :] = alpha * acc_ref[hh, rows, :] + _dot(p.astype(BF16), v)
        m_ref[hh, rows, :] = m_new

    def body(j, c):
        _run_phases([chain(hh, part, j, tk, False) for hh in range(hpb) for part in range(nsplit)])
        return c

    lax.fori_loop(0, i, body, 0)
    _run_phases([chain(hh, part, i, (part + 1) * t, True) for hh in range(hpb) for part in range(nsplit)])
    for hh in range(hpb):
        l = jnp.sum(l_ref[hh], axis=-1, keepdims=True)
        o_ref[:, hh * VDIM:(hh + 1) * VDIM] = (acc_ref[hh] / l).astype(BF16)


def attention(q, k, v, t=512, hpb=2, nsplit=2):
    S = q.shape[1]
    tq = nsplit * t
    return pl.pallas_call(
        functools.partial(_attn_kernel, t, hpb, nsplit),
        grid=(HEADS // hpb, S // tq),
        in_specs=[pl.BlockSpec((hpb, tq, QK_PAD), lambda h, i: (h, i, 0)),
                  pl.BlockSpec((hpb, S, QK_PAD), lambda h, i: (h, 0, 0)),
                  pl.BlockSpec((hpb, S, VDIM), lambda h, i: (h, 0, 0))],
        out_specs=pl.BlockSpec((tq, hpb * VDIM), lambda h, i: (i, h)),
        out_shape=jax.ShapeDtypeStruct((S, HEADS * VDIM), BF16),
        scratch_shapes=[pltpu.VMEM((hpb, tq, LANES), F32), pltpu.VMEM((hpb, tq, LANES), F32),
                        pltpu.VMEM((hpb, tq, VDIM), F32)],
        compiler_params=_cparams(("arbitrary", "arbitrary")),
        name="attention",
    )(q, k, v)


def _hgrn_kernel(nblk, hpb, q_ref, f_ref, v_ref, gate_ref, gn_ref, o_ref, st_ref, bs2_ref):
    @pl.when(pl.program_id(1) == 0)
    def _():
        st_ref[...] = jnp.zeros(st_ref.shape, F32)

    B = LANES
    r = lax.broadcasted_iota(jnp.int32, (B, B), 0)
    c = lax.broadcasted_iota(jnp.int32, (B, B), 1)
    same = (r // SUB) == (c // SUB)
    causal = same & (c <= r)
    incl = jnp.where(causal, 1.0, 0.0).astype(BF16)
    nsub = B // SUB
    row_sub = r // SUB
    col_sub = c // SUB
    lane8 = lax.broadcasted_iota(jnp.int32, (SUB // 2, B), 1)
    ones = jnp.ones((HG_D, B), BF16)
    gn = gn_ref[...]

    def one_head(hh, rows):
        cols = slice(hh * HG_D, (hh + 1) * HG_D)
        f = f_ref[rows, cols]
        q = q_ref[rows, cols].astype(F32)
        g = jnp.log(jnp.maximum(f, MIN_FORGET))
        g0 = g.astype(BF16)
        r1 = g - g0.astype(F32)
        g1 = r1.astype(BF16)
        g2 = (r1 - g1.astype(F32)).astype(BF16)
        b3 = _dot(incl, jnp.concatenate([g0, g1, g2], axis=1))
        b = b3[:, :HG_D] + b3[:, HG_D:2 * HG_D] + b3[:, 2 * HG_D:]
        b_last = jnp.concatenate(
            [jnp.broadcast_to(b[(u + 1) * SUB - 1:(u + 1) * SUB], (SUB, HG_D)) for u in range(nsub)], axis=0)
        eb = jnp.exp(b)
        kin = jnp.maximum(1.0 - f, 0.0)
        qe = (q * eb).astype(BF16)
        kd = (kin * jnp.exp(b_last - b)).astype(BF16)
        vb = v_ref[rows, cols]
        b2 = b * LOG2E
        bs2_ref[hh] = b2 - jnp.log2(kin)
        yield
        hs = SUB // 2
        ps = []
        sums = {}
        for u in range(nsub):
            lo = u * SUB
            q16 = q[lo:lo + SUB]
            b16 = b2[lo:lo + SUB]
            pending = None
            for s in range(SUB):
                upper = s >= hs
                qq, bb = (q16[hs:], b16[hs:]) if upper else (q16, b16)
                p = qq * jnp.exp2(bb - bs2_ref[hh, pl.ds(lo + s, 1), :])
                if s % 2 == 1:
                    sums[lo + s] = (upper, jnp.sum(p, axis=-1, keepdims=True))
                elif not upper:
                    ps.append((p.astype(BF16), [(lo + s, 0, SUB)]))
                elif pending is None:
                    pending = (lo + s, p)
                else:
                    both = jnp.concatenate([pending[1], p], axis=0).astype(BF16)
                    ps.append((both, [(pending[0], 0, hs), (lo + s, hs, hs)]))
                    pending = None
        rs = _dot(jnp.concatenate([piece for piece, _ in ps], axis=0), ones)
        yield
        for n, (_, parts) in enumerate(ps):
            for rr, first, count in parts:
                sums[rr] = (count == hs, rs[n * SUB + first:n * SUB + first + count])
        a_rows = []
        for u in range(nsub):
            a_lo = jnp.zeros((hs, B), F32)
            a_hi = jnp.zeros((hs, B), F32)
            for s in range(SUB):
                rr = u * SUB + s
                upper, val = sums[rr]
                if upper:
                    a_hi = jnp.where(lane8 == rr, val, a_hi)
                else:
                    a_lo = jnp.where(lane8 == rr, val[:hs], a_lo)
                    a_hi = jnp.where(lane8 == rr, val[hs:], a_hi)
            a_rows.append(jnp.concatenate([a_lo, a_hi], axis=0))
        a_blk = jnp.where(causal, jnp.concatenate(a_rows, axis=0), 0.0).astype(BF16)

        vt = jnp.transpose(vb)
        vt_u = jnp.concatenate([jnp.where(col_sub == u, vt, 0.0) for u in range(nsub)], axis=0)
        upd = _dot(vt_u, kd)
        yield
        st = st_ref[hh]
        sts = []
        for u in range(nsub):
            sts.append(st.astype(BF16))
            st = st * eb[(u + 1) * SUB - 1:(u + 1) * SUB] + upd[u * B:(u + 1) * B]
        st_ref[hh] = st
        qe_u = jnp.concatenate([jnp.where(row_sub == u, qe, 0.0) for u in range(nsub)], axis=1)
        o_int = lax.dot_general(qe_u, jnp.concatenate(sts, axis=1), (((1,), (1,)), ((), ())),
                                preferred_element_type=F32)
        o = _dot(a_blk, vb) + o_int
        o = _rms(o, gn) * gate_ref[rows, cols].astype(F32)
        o_ref[rows, cols] = o.astype(BF16)

    def blk_body(bi, carry):
        rows = pl.ds(pl.multiple_of(bi * B, B), B)
        _run_phases([one_head(hh, rows) for hh in range(hpb)])
        return carry

    lax.fori_loop(0, nblk, blk_body, 0)


def hgrn(layer, q, f, v, gate, gn, ts=1024, hpb=8):
    S = q.shape[0]
    w = hpb * HG_D
    blk = pl.BlockSpec((ts, w), lambda h, i: (i, h))
    return pl.pallas_call(
        functools.partial(_hgrn_kernel, ts // LANES, hpb),
        grid=(HEADS // hpb, S // ts),
        in_specs=[blk, blk, blk, blk, pl.BlockSpec((None, 1, HG_D), lambda h, i: (layer, 0, 0))],
        out_specs=blk,
        out_shape=jax.ShapeDtypeStruct((S, D_MODEL), BF16),
        scratch_shapes=[pltpu.VMEM((hpb, HG_D, HG_D), F32), pltpu.VMEM((hpb, LANES, HG_D), F32)],
        compiler_params=_cparams(("arbitrary", "arbitrary")),
        name="hgrn",
    )(q, f, v, gate, gn)


def _merge_kernel(a_ref, r_ref, ga_ref, gb_ref, wa_ref, wb_ref, wo_ref, x_ref, o_ref):
    @pl.when(pl.program_id(1) == 0)
    def _():
        o_ref[...] = x_ref[...]

    ma = _dot(a_ref[...], wa_ref[...])
    mb = _dot(r_ref[...], wb_ref[...])
    mg = (ga_ref[...].astype(F32) * ma + gb_ref[...].astype(F32) * mb).astype(BF16)
    o_ref[...] += _dot(mg, wo_ref[...])


def merge(layer, attn, rec, ga, gb, wa, wb, wo, x, tm=512, tn=512):
    S = x.shape[0]
    return pl.pallas_call(
        _merge_kernel,
        grid=(S // tm, D_MODEL // tn),
        in_specs=[pl.BlockSpec((tm, D_MODEL), lambda i, j: (i, 0)),
                  pl.BlockSpec((tm, D_MODEL), lambda i, j: (i, 0)),
                  pl.BlockSpec((tm, tn), lambda i, j: (i, j)),
                  pl.BlockSpec((tm, tn), lambda i, j: (i, j)),
                  pl.BlockSpec((None, D_MODEL, tn), lambda i, j: (layer, 0, j)),
                  pl.BlockSpec((None, D_MODEL, tn), lambda i, j: (layer, 0, j)),
                  pl.BlockSpec((None, tn, D_MODEL), lambda i, j: (layer, j, 0)),
                  pl.BlockSpec((tm, D_MODEL), lambda i, j: (i, 0))],
        out_specs=pl.BlockSpec((tm, D_MODEL), lambda i, j: (i, 0)),
        out_shape=jax.ShapeDtypeStruct((S, D_MODEL), F32),
        compiler_params=_cparams(("arbitrary", "arbitrary")),
        name="merge",
    )(attn, rec, ga, gb, wa, wb, wo, x)


def _ffn_kernel(final, nj, tm, x_ref, xh_ref, g_ref, wg_ref, wu_ref, cwg_ref, cwu_ref, cbg_ref, cbu_ref,
                wd_ref, fg_ref, o_ref, h_ref, ug_ref, uu_ref):
    m = pl.program_id(0)
    j = pl.program_id(1)

    @pl.when(j == 0)
    def _():
        g = g_ref[...]
        x = x_ref[...]
        o_ref[...] = x
        h_ref[HALO:, :] = _rms(x, g).astype(BF16)
        hh = jnp.where(m > 0, _rms(xh_ref[...], g), 0.0)
        h_ref[:HALO, :] = hh.astype(BF16)

    h = h_ref[...]
    ug_ref[...] = _dot(h, wg_ref[...])
    uu_ref[...] = _dot(h, wu_ref[...])

    def conv(u_ref, cw_ref, cb_ref):
        y = cb_ref[...]
        for tap in range(3):
            y = y + cw_ref[tap:tap + 1, :] * u_ref[pl.ds(HALO - 2 + tap, tm), :]
        return y

    yg = conv(ug_ref, cwg_ref, cbg_ref)
    yu = conv(uu_ref, cwu_ref, cbu_ref)
    act = (yg * jax.nn.sigmoid(yg) * yu).astype(BF16)
    o_ref[...] += _dot(act, wd_ref[...])

    if final:
        @pl.when(j == nj - 1)
        def _():
            o_ref[...] = _rms(o_ref[...], fg_ref[...])


def ffn(layer, x, g, w_up, conv_w, conv_b, w_down, fg, tm=1024, tn=512):
    S = x.shape[0]
    nj = D_FF // tn
    hb = tm // HALO
    final = layer == DEPTH - 1
    return pl.pallas_call(
        functools.partial(_ffn_kernel, final, nj, tm),
        grid=(S // tm, nj),
        in_specs=[pl.BlockSpec((tm, D_MODEL), lambda i, j: (i, 0)),
                  pl.BlockSpec((HALO, D_MODEL), lambda i, j: (jnp.maximum(i * hb - 1, 0), 0)),
                  pl.BlockSpec((None, 1, D_MODEL), lambda i, j: (layer, 0, 0)),
                  pl.BlockSpec((None, D_MODEL, tn), lambda i, j: (layer, 0, j)),
                  pl.BlockSpec((None, D_MODEL, tn), lambda i, j: (layer, 0, nj + j)),
                  pl.BlockSpec((None, 3, tn), lambda i, j: (layer, 0, j)),
                  pl.BlockSpec((None, 3, tn), lambda i, j: (layer, 0, nj + j)),
                  pl.BlockSpec((None, 1, tn), lambda i, j: (layer, 0, j)),
                  pl.BlockSpec((None, 1, tn), lambda i, j: (layer, 0, nj + j)),
                  pl.BlockSpec((None, tn, D_MODEL), lambda i, j: (layer, j, 0)),
                  pl.BlockSpec((1, D_MODEL), lambda i, j: (0, 0))],
        out_specs=pl.BlockSpec((tm, D_MODEL), lambda i, j: (i, 0), pipeline_mode=pl.Buffered(1)),
        out_shape=jax.ShapeDtypeStruct((S, D_MODEL), F32),
        scratch_shapes=[pltpu.VMEM((tm + HALO, D_MODEL), BF16),
                        pltpu.VMEM((tm + HALO, tn), F32),
                        pltpu.VMEM((tm + HALO, tn), F32)],
        compiler_params=_cparams(("arbitrary", "arbitrary")),
        name="ffn",
    )(x, x, g, w_up, w_up, conv_w, conv_w, conv_b, conv_b, w_down, fg)


def _rope_pair_cols(w):
    x1, x2 = w[..., :ROPE // 2], w[..., ROPE // 2:]
    z = jnp.zeros(w.shape[:-1] + (LANES - ROPE,), w.dtype)
    return jnp.concatenate([x1, x2, z], axis=-1), jnp.concatenate([x2, x1, z], axis=-1)


def kernel(x, positions, attn_norm_g, w_in, q_norm_g, w_uq, kv_norm_g, w_ukv, hgrn_lb_logits, hgrn_out_norm_g,
           w_branch_a, w_branch_b, w_out, ffn_norm_g, w_up, conv_w, conv_b, w_down, final_norm_g):
    B, S, D = x.shape
    assert B == 1 and D == D_MODEL
    xs = x.reshape(S, D)

    inv_freq = ROPE_THETA ** (-jnp.arange(0, ROPE, 2, dtype=F32) / ROPE)
    zeros = jnp.zeros((LANES - ROPE,), F32)
    invf = jnp.concatenate([inv_freq, inv_freq, zeros]).reshape(1, LANES)
    half = jnp.ones((ROPE // 2,), F32)
    sgn = jnp.concatenate([-half, half, zeros]).reshape(1, LANES)
    cosT, sinT = rope_tables(positions.reshape(S, 1), invf, sgn)

    n_lat = Q_LORA + KV_LORA
    wt = jnp.swapaxes(w_in, 1, 2)
    wq = w_uq.reshape(DEPTH, Q_LORA, HEADS, NOPE + ROPE)
    qa, qb = _rope_pair_cols(wq[..., NOPE:])
    wq = jnp.concatenate([wq[..., :NOPE], qa, qb], axis=-1).reshape(DEPTH, Q_LORA, -1).astype(BF16)
    wkv = w_ukv.astype(BF16)
    wa, wb, wo = w_branch_a.astype(BF16), w_branch_b.astype(BF16), w_out.astype(BF16)
    wup, wdn = w_up.astype(BF16), w_down.astype(BF16)
    conv_b3 = conv_b.reshape(DEPTH, 1, 2 * D_FF)
    lb_logits = hgrn_lb_logits.astype(F32)
    fg = final_norm_g.reshape(1, D)

    rows3 = lambda a: a.reshape(DEPTH, 1, a.shape[-1])
    g_attn, g_q, g_kv, g_hg, g_ffn = map(rows3, (attn_norm_g, q_norm_g, kv_norm_g, hgrn_out_norm_g, ffn_norm_g))

    for l in range(DEPTH):
        cq, ckv, kpe, h = in1(l, xs, g_attn, wt, g_q, g_kv, cosT, sinT)
        hq, hf, hi, hg, ga, gb = in2(l, h, wt, n_lat + ROPE, lb_logits)
        q, k, v = qkv(l, cq, ckv, kpe, cosT, sinT, wq, wkv)
        attn = attention(q, k, v)
        rec = hgrn(l, hq, hf, hi, hg, g_hg)
        xs = merge(l, attn, rec, ga, gb, wa, wb, wo, xs)
        xs = ffn(l, xs, g_ffn, wup, conv_w, conv_b3, wdn, fg)
    return xs.reshape(B, S, D)
```

```python
import functools

import jax
import jax.numpy as jnp
from jax import lax
from jax.experimental import pallas as pl
from jax.experimental.pallas import tpu as pltpu

F32 = jnp.float32
BF16 = jnp.bfloat16

D_MODEL = 2048
DEPTH = 4
HEADS = 16
Q_LORA = 512
KV_LORA = 512
NOPE = 128
ROPE = 64
VDIM = 128
ROPE_THETA = 10000.0
MASK_VALUE = -1e30
HG_D = 128
SUB = 16
MIN_FORGET = 1e-30
D_FF = 5632
EPS = 1e-6
LOG2E = 1.4426950408889634
LANES = 128
QK_PAD = 256
HALO = 16

VMEM_LIMIT = 52 * 1024 * 1024


def _cparams(sem):
    return pltpu.CompilerParams(dimension_semantics=sem, vmem_limit_bytes=VMEM_LIMIT)


def _rms(x, g):
    return x * lax.rsqrt(jnp.mean(x * x, axis=-1, keepdims=True) + EPS) * g


def _dot(a, b):
    return jnp.dot(a, b, preferred_element_type=F32)


def _dot_t(a, bt):
    return lax.dot_general(a, bt, (((1,), (1,)), ((), ())), preferred_element_type=F32)


def _rope_kernel(pos_ref, invf_ref, sgn_ref, cos_ref, sin_ref):
    ang = pos_ref[...].astype(F32) * invf_ref[...]
    sgn = sgn_ref[...]
    cos_ref[...] = jnp.cos(ang) * jnp.abs(sgn)
    sin_ref[...] = jnp.sin(ang) * sgn


def rope_tables(pos_col, invf, sgn, tm=1024):
    S = pos_col.shape[0]
    return pl.pallas_call(
        _rope_kernel,
        grid=(S // tm,),
        in_specs=[pl.BlockSpec((tm, 1), lambda i: (i, 0)),
                  pl.BlockSpec((1, LANES), lambda i: (0, 0)),
                  pl.BlockSpec((1, LANES), lambda i: (0, 0))],
        out_specs=[pl.BlockSpec((tm, LANES), lambda i: (i, 0))] * 2,
        out_shape=[jax.ShapeDtypeStruct((S, LANES), F32)] * 2,
        compiler_params=_cparams(("arbitrary",)),
        name="rope_tables",
    )(pos_col, invf, sgn)


def _in1_kernel(x_ref, g_ref, wlat_ref, wrope_ref, qg_ref, kvg_ref, cos_ref, sin_ref,
                cq_ref, ckv_ref, kpe_ref, h_ref, w_ref):
    n_lat = Q_LORA + KV_LORA
    half = ROPE // 2

    @pl.when(pl.program_id(0) == 0)
    def _():
        w_ref[:n_lat, :] = wlat_ref[0].astype(BF16)
        r = wrope_ref[0].astype(BF16)
        x1, x2 = r[:half], r[half:]
        z = jnp.zeros((LANES - ROPE, D_MODEL), BF16)
        w_ref[n_lat:, :] = jnp.concatenate([x1, x2, z, x2, x1, z], axis=0)

    h = _rms(x_ref[...], g_ref[...]).astype(BF16)
    h_ref[...] = h
    y = _dot_t(h, w_ref[...])
    cq_ref[...] = _rms(y[:, :Q_LORA], qg_ref[...]).astype(BF16)
    ckv_ref[...] = _rms(y[:, Q_LORA:n_lat], kvg_ref[...]).astype(BF16)
    kpe = y[:, n_lat:n_lat + LANES] * cos_ref[...] + y[:, n_lat + LANES:] * sin_ref[...]
    kpe_ref[...] = kpe.astype(BF16)


def in1(layer, x, g, wt, qg, kvg, cosT, sinT, tm=512):
    S = x.shape[0]
    n_lat = Q_LORA + KV_LORA
    row = lambda i: (i, 0)
    lyr = lambda i: (layer, 0, 0)
    elem = lambda n: (pl.Element(1), pl.Element(n), pl.Element(D_MODEL))
    return pl.pallas_call(
        _in1_kernel,
        grid=(S // tm,),
        in_specs=[pl.BlockSpec((tm, D_MODEL), row), pl.BlockSpec((None, 1, D_MODEL), lyr),
                  pl.BlockSpec(elem(n_lat), lyr), pl.BlockSpec(elem(ROPE), lambda i: (layer, n_lat, 0)),
                  pl.BlockSpec((None, 1, Q_LORA), lyr),
                  pl.BlockSpec((None, 1, KV_LORA), lyr), pl.BlockSpec((tm, LANES), row),
                  pl.BlockSpec((tm, LANES), row)],
        out_specs=[pl.BlockSpec((tm, Q_LORA), row), pl.BlockSpec((tm, KV_LORA), row),
                   pl.BlockSpec((tm, LANES), row), pl.BlockSpec((tm, D_MODEL), row)],
        out_shape=[jax.ShapeDtypeStruct((S, Q_LORA), BF16), jax.ShapeDtypeStruct((S, KV_LORA), BF16),
                   jax.ShapeDtypeStruct((S, LANES), BF16), jax.ShapeDtypeStruct((S, D_MODEL), BF16)],
        scratch_shapes=[pltpu.VMEM((n_lat + 2 * LANES, D_MODEL), BF16)],
        compiler_params=_cparams(("arbitrary",)),
        name="in1",
    )(x, g, wt, wt, qg, kvg, cosT, sinT)


def _in2_kernel(layer, h_ref, wq_ref, wf_ref, wi_ref, wg_ref, wa_ref, wb_ref, lbl_ref,
                oq_ref, of_ref, oi_ref, og_ref, oa_ref, ob_ref, w_ref):
    @pl.when(pl.program_id(1) == 0)
    def _():
        for s, src in enumerate((wq_ref, wf_ref, wi_ref, wg_ref, wa_ref, wb_ref)):
            w_ref[s] = src[0].astype(BF16)

    h = h_ref[...]
    y = _dot_t(h, w_ref[0])
    oq_ref[...] = (y * jax.nn.sigmoid(y)).astype(BF16)
    y = _dot_t(h, w_ref[1])
    lg = lbl_ref[...]
    e = jnp.exp(lg - jnp.max(lg, axis=0, keepdims=True))
    p = e / jnp.sum(e, axis=0, keepdims=True)
    lb = jnp.sum(p[:layer + 1], axis=0, keepdims=True) - p[0:1]
    of_ref[...] = lb + (1.0 - lb) * jax.nn.sigmoid(y)
    y = _dot_t(h, w_ref[2])
    oi_ref[...] = y.astype(BF16)
    y = _dot_t(h, w_ref[3])
    og_ref[...] = (y * jax.nn.sigmoid(y)).astype(BF16)
    y = _dot_t(h, w_ref[4])
    oa_ref[...] = jax.nn.sigmoid(y).astype(BF16)
    y = _dot_t(h, w_ref[5])
    ob_ref[...] = jax.nn.sigmoid(y).astype(BF16)


def in2(layer, h, wt, off, lb_logits, tm=1024, tn=256):
    S = h.shape[0]
    nseg = (wt.shape[1] - off) // D_MODEL
    bps = D_MODEL // tn

    def seg_rows(s, j, i):
        return layer, pl.multiple_of(off + (s * bps + j) * tn, HALO), 0

    w_block = (pl.Element(1), pl.Element(tn), pl.Element(D_MODEL))
    w_specs = [pl.BlockSpec(w_block, functools.partial(seg_rows, s)) for s in range(nseg)]
    out_dtypes = [BF16, F32, BF16, BF16, BF16, BF16]
    return pl.pallas_call(
        functools.partial(_in2_kernel, layer),
        grid=(bps, S // tm),
        in_specs=[pl.BlockSpec((tm, D_MODEL), lambda j, i: (i, 0))] + w_specs +
                 [pl.BlockSpec((DEPTH, tn), lambda j, i: (0, j))],
        out_specs=[pl.BlockSpec((tm, tn), lambda j, i: (i, j))] * nseg,
        out_shape=[jax.ShapeDtypeStruct((S, D_MODEL), dt) for dt in out_dtypes],
        scratch_shapes=[pltpu.VMEM((nseg, tn, D_MODEL), BF16)],
        compiler_params=_cparams(("arbitrary", "arbitrary")),
        name="in2",
    )(h, *([wt] * nseg), lb_logits)


def _qkv_kernel(scale, cq_ref, ckv_ref, kpe_ref, cos_ref, sin_ref, wq_ref, wkv_ref, q_ref, k_ref, v_ref):
    cq = cq_ref[...]
    ckv = ckv_ref[...]
    cos = cos_ref[...]
    sin = sin_ref[...]
    kpe = kpe_ref[...]
    qw = NOPE + 2 * LANES
    kvw = NOPE + VDIM
    grp = 4
    for g in range(HEADS // grp):
        yg = _dot(cq, wq_ref[:, g * grp * qw:(g + 1) * grp * qw])
        zg = _dot(ckv, wkv_ref[:, g * grp * kvw:(g + 1) * grp * kvw])
        for hh in range(grp):
            h = g * grp + hh
            y = yg[:, hh * qw:(hh + 1) * qw]
            pe = y[:, NOPE:NOPE + LANES] * cos + y[:, NOPE + LANES:] * sin
            q_ref[h, :, :NOPE] = (y[:, :NOPE] * scale).astype(BF16)
            q_ref[h, :, NOPE:] = (pe * scale).astype(BF16)
            z = zg[:, hh * kvw:(hh + 1) * kvw]
            k_ref[h, :, :NOPE] = z[:, :NOPE].astype(BF16)
            k_ref[h, :, NOPE:] = kpe
            v_ref[h] = z[:, NOPE:].astype(BF16)


def qkv(layer, cq, ckv, kpe, cosT, sinT, wq, wkv, tm=512):
    S = cq.shape[0]
    scale = float((NOPE + ROPE) ** -0.5 * LOG2E)
    row = lambda i: (i, 0)
    return pl.pallas_call(
        functools.partial(_qkv_kernel, scale),
        grid=(S // tm,),
        in_specs=[pl.BlockSpec((tm, Q_LORA), row), pl.BlockSpec((tm, KV_LORA), row),
                  pl.BlockSpec((tm, LANES), row), pl.BlockSpec((tm, LANES), row),
                  pl.BlockSpec((tm, LANES), row),
                  pl.BlockSpec((None,) + wq.shape[1:], lambda i: (layer, 0, 0)),
                  pl.BlockSpec((None,) + wkv.shape[1:], lambda i: (layer, 0, 0))],
        out_specs=[pl.BlockSpec((HEADS, tm, QK_PAD), lambda i: (0, i, 0)),
                   pl.BlockSpec((HEADS, tm, QK_PAD), lambda i: (0, i, 0)),
                   pl.BlockSpec((HEADS, tm, VDIM), lambda i: (0, i, 0))],
        out_shape=[jax.ShapeDtypeStruct((HEADS, S, QK_PAD), BF16),
                   jax.ShapeDtypeStruct((HEADS, S, QK_PAD), BF16),
                   jax.ShapeDtypeStruct((HEADS, S, VDIM), BF16)],
        compiler_params=_cparams(("arbitrary",)),
        name="qkv",
    )(cq, ckv, kpe, cosT, sinT, wq, wkv)


def _run_phases(gens):
    live = list(gens)
    while live:
        live = [g for g in live if next(g, live) is not live]


def _attn_kernel(t, hpb, nsplit, q_ref, k_ref, v_ref, o_ref, m_ref, l_ref, acc_ref):
    i = pl.program_id(1)
    tk = nsplit * t
    m_ref[...] = jnp.full(m_ref.shape, -jnp.inf, F32)
    l_ref[...] = jnp.zeros(l_ref.shape, F32)
    acc_ref[...] = jnp.zeros(acc_ref.shape, F32)

    def chain(hh, part, j, kw, masked, s_head=None):
        nb = kw // LANES
        start = pl.multiple_of(j * tk, tk)
        rows = slice(part * t, (part + 1) * t)
        if s_head is None:
            k = k_ref[hh, pl.ds(start, kw), :]
            s = lax.dot_general(q_ref[hh, rows, :], k, (((1,), (1,)), ((), ())), preferred_element_type=F32)
        else:
            s = s_head[rows]
        yield
        if masked:
            row = lax.broadcasted_iota(jnp.int32, (t, kw), 0) + part * t
            col = lax.broadcasted_iota(jnp.int32, (t, kw), 1)
            s = jnp.where(col <= row, s, MASK_VALUE)
        lane_blocks = lambda x: [x[:, b * LANES:(b + 1) * LANES] for b in range(nb)]
        m_prev = m_ref[hh, rows, :]
        m_cur = functools.reduce(jnp.maximum, lane_blocks(s))
        m_new = jnp.maximum(m_prev, jnp.max(m_cur, axis=-1, keepdims=True))
        alpha = jnp.exp2(m_prev - m_new)
        p = jnp.exp2(s - jnp.concatenate([m_new] * nb, axis=1))
        l_ref[hh, rows, :] = alpha * l_ref[hh, rows, :] + functools.reduce(jnp.add, lane_blocks(p))
        v = v_ref[hh, pl.ds(start, kw), :]
        acc_ref[hh, rows, :] = alpha * acc_ref[hh, rows, :] + _dot(p.astype(BF16), v)
        m_ref[hh, rows, :] = m_new

    def body(j, c):
        start = pl.multiple_of(j * tk, tk)
        s_heads = [lax.dot_general(q_ref[hh], k_ref[hh, pl.ds(start, tk), :], (((1,), (1,)), ((), ())),
                                   preferred_element_type=F32) for hh in range(hpb)]
        _run_phases([chain(hh, part, j, tk, False, s_heads[hh]) for hh in range(hpb) for part in range(nsplit)])
        return c

    lax.fori_loop(0, i, body, 0)
    _run_phases([chain(hh, part, i, (part + 1) * t, True) for hh in range(hpb) for part in range(nsplit)])
    for hh in range(hpb):
        l = jnp.sum(l_ref[hh], axis=-1, keepdims=True)
        o_ref[:, hh * VDIM:(hh + 1) * VDIM] = (acc_ref[hh] / l).astype(BF16)


def attention(q, k, v, t=512, hpb=2, nsplit=2):
    S = q.shape[1]
    tq = nsplit * t
    return pl.pallas_call(
        functools.partial(_attn_kernel, t, hpb, nsplit),
        grid=(HEADS // hpb, S // tq),
        in_specs=[pl.BlockSpec((hpb, tq, QK_PAD), lambda h, i: (h, i, 0)),
                  pl.BlockSpec((hpb, S, QK_PAD), lambda h, i: (h, 0, 0)),
                  pl.BlockSpec((hpb, S, VDIM), lambda h, i: (h, 0, 0))],
        out_specs=pl.BlockSpec((tq, hpb * VDIM), lambda h, i: (i, h)),
        out_shape=jax.ShapeDtypeStruct((S, HEADS * VDIM), BF16),
        scratch_shapes=[pltpu.VMEM((hpb, tq, LANES), F32), pltpu.VMEM((hpb, tq, LANES), F32),
                        pltpu.VMEM((hpb, tq, VDIM), F32)],
        compiler_params=_cparams(("arbitrary", "arbitrary")),
        name="attention",
    )(q, k, v)


def _hgrn_kernel(nblk, hpb, q_ref, f_ref, v_ref, gate_ref, gn_ref, o_ref, st_ref, bs2_ref):
    @pl.when(pl.program_id(1) == 0)
    def _():
        st_ref[...] = jnp.zeros(st_ref.shape, F32)

    B = LANES
    r = lax.broadcasted_iota(jnp.int32, (B, B), 0)
    c = lax.broadcasted_iota(jnp.int32, (B, B), 1)
    same = (r // SUB) == (c // SUB)
    causal = same & (c <= r)
    incl = jnp.where(causal, 1.0, 0.0).astype(BF16)
    nsub = B // SUB
    row_sub = r // SUB
    col_sub = c // SUB
    lane8 = lax.broadcasted_iota(jnp.int32, (SUB // 2, B), 1)
    ones = jnp.ones((HG_D, B), BF16)
    gn = gn_ref[...]

    def one_head(hh, rows):
        cols = slice(hh * HG_D, (hh + 1) * HG_D)
        f = f_ref[rows, cols]
        q = q_ref[rows, cols].astype(F32)
        g = jnp.log(jnp.maximum(f, MIN_FORGET))
        g0 = g.astype(BF16)
        r1 = g - g0.astype(F32)
        g1 = r1.astype(BF16)
        g2 = (r1 - g1.astype(F32)).astype(BF16)
        b3 = _dot(incl, jnp.concatenate([g0, g1, g2], axis=1))
        b = b3[:, :HG_D] + b3[:, HG_D:2 * HG_D] + b3[:, 2 * HG_D:]
        b_last = jnp.concatenate(
            [jnp.broadcast_to(b[(u + 1) * SUB - 1:(u + 1) * SUB], (SUB, HG_D)) for u in range(nsub)], axis=0)
        eb = jnp.exp(b)
        kin = jnp.maximum(1.0 - f, 0.0)
        qe = (q * eb).astype(BF16)
        kd = (kin * jnp.exp(b_last - b)).astype(BF16)
        vb = v_ref[rows, cols]
        b2 = b * LOG2E
        bs2_ref[hh] = b2 - jnp.log2(kin)
        yield
        hs = SUB // 2
        ps = []
        sums = {}
        for u in range(nsub):
            lo = u * SUB
            q16 = q[lo:lo + SUB]
            b16 = b2[lo:lo + SUB]
            pending = None
            for s in range(SUB):
                upper = s >= hs
                qq, bb = (q16[hs:], b16[hs:]) if upper else (q16, b16)
                p = qq * jnp.exp2(bb - bs2_ref[hh, pl.ds(lo + s, 1), :])
                if s % 2 == 1:
                    sums[lo + s] = (upper, jnp.sum(p, axis=-1, keepdims=True))
                elif not upper:
                    ps.append((p.astype(BF16), [(lo + s, 0, SUB)]))
                elif pending is None:
                    pending = (lo + s, p)
                else:
                    both = jnp.concatenate([pending[1], p], axis=0).astype(BF16)
                    ps.append((both, [(pending[0], 0, hs), (lo + s, hs, hs)]))
                    pending = None
        rs = _dot(jnp.concatenate([piece for piece, _ in ps], axis=0), ones)
        yield
        for n, (_, parts) in enumerate(ps):
            for rr, first, count in parts:
                sums[rr] = (count == hs, rs[n * SUB + first:n * SUB + first + count])
        a_rows = []
        for u in range(nsub):
            a_lo = jnp.zeros((hs, B), F32)
            a_hi = jnp.zeros((hs, B), F32)
            for s in range(SUB):
                rr = u * SUB + s
                upper, val = sums[rr]
                if upper:
                    a_hi = jnp.where(lane8 == rr, val, a_hi)
                else:
                    a_lo = jnp.where(lane8 == rr, val[:hs], a_lo)
                    a_hi = jnp.where(lane8 == rr, val[hs:], a_hi)
            a_rows.append(jnp.concatenate([a_lo, a_hi], axis=0))
        a_blk = jnp.where(causal, jnp.concatenate(a_rows, axis=0), 0.0).astype(BF16)

        vt = jnp.transpose(vb)
        vt_u = jnp.concatenate([jnp.where(col_sub == u, vt, 0.0) for u in range(nsub)], axis=0)
        upd = _dot(vt_u, kd)
        yield
        st = st_ref[hh]
        sts = []
        for u in range(nsub):
            sts.append(st.astype(BF16))
            st = st * eb[(u + 1) * SUB - 1:(u + 1) * SUB] + upd[u * B:(u + 1) * B]
        st_ref[hh] = st
        qe_u = jnp.concatenate([jnp.where(row_sub == u, qe, 0.0) for u in range(nsub)], axis=1)
        o_int = lax.dot_general(qe_u, jnp.concatenate(sts, axis=1), (((1,), (1,)), ((), ())),
                                preferred_element_type=F32)
        o = _dot(a_blk, vb) + o_int
        o = _rms(o, gn) * gate_ref[rows, cols].astype(F32)
        o_ref[rows, cols] = o.astype(BF16)

    def blk_body(bi, carry):
        rows = pl.ds(pl.multiple_of(bi * B, B), B)
        _run_phases([one_head(hh, rows) for hh in range(hpb)])
        return carry

    lax.fori_loop(0, nblk, blk_body, 0)


def hgrn(layer, q, f, v, gate, gn, ts=1024, hpb=8):
    S = q.shape[0]
    w = hpb * HG_D
    blk = pl.BlockSpec((ts, w), lambda h, i: (i, h))
    return pl.pallas_call(
        functools.partial(_hgrn_kernel, ts // LANES, hpb),
        grid=(HEADS // hpb, S // ts),
        in_specs=[blk, blk, blk, blk, pl.BlockSpec((None, 1, HG_D), lambda h, i: (layer, 0, 0))],
        out_specs=blk,
        out_shape=jax.ShapeDtypeStruct((S, D_MODEL), BF16),
        scratch_shapes=[pltpu.VMEM((hpb, HG_D, HG_D), F32), pltpu.VMEM((hpb, LANES, HG_D), F32)],
        compiler_params=_cparams(("arbitrary", "arbitrary")),
        name="hgrn",
    )(q, f, v, gate, gn)


def _merge_kernel(a_ref, r_ref, ga_ref, gb_ref, wa_ref, wb_ref, wo_ref, x_ref, o_ref):
    @pl.when(pl.program_id(1) == 0)
    def _():
        o_ref[...] = x_ref[...]

    ma = _dot(a_ref[...], wa_ref[...])
    mb = _dot(r_ref[...], wb_ref[...])
    mg = (ga_ref[...].astype(F32) * ma + gb_ref[...].astype(F32) * mb).astype(BF16)
    o_ref[...] += _dot(mg, wo_ref[...])


def merge(layer, attn, rec, ga, gb, wa, wb, wo, x, tm=512, tn=512):
    S = x.shape[0]
    return pl.pallas_call(
        _merge_kernel,
        grid=(S // tm, D_MODEL // tn),
        in_specs=[pl.BlockSpec((tm, D_MODEL), lambda i, j: (i, 0)),
                  pl.BlockSpec((tm, D_MODEL), lambda i, j: (i, 0)),
                  pl.BlockSpec((tm, tn), lambda i, j: (i, j)),
                  pl.BlockSpec((tm, tn), lambda i, j: (i, j)),
                  pl.BlockSpec((None, D_MODEL, tn), lambda i, j: (layer, 0, j)),
                  pl.BlockSpec((None, D_MODEL, tn), lambda i, j: (layer, 0, j)),
                  pl.BlockSpec((None, tn, D_MODEL), lambda i, j: (layer, j, 0)),
                  pl.BlockSpec((tm, D_MODEL), lambda i, j: (i, 0))],
        out_specs=pl.BlockSpec((tm, D_MODEL), lambda i, j: (i, 0)),
        out_shape=jax.ShapeDtypeStruct((S, D_MODEL), F32),
        compiler_params=_cparams(("arbitrary", "arbitrary")),
        name="merge",
    )(attn, rec, ga, gb, wa, wb, wo, x)


def _ffn_kernel(final, nj, tm, x_ref, xh_ref, g_ref, wg_ref, wu_ref, cwg_ref, cwu_ref, cbg_ref, cbu_ref,
                wd_ref, fg_ref, o_ref, h_ref, ug_ref, uu_ref):
    m = pl.program_id(0)
    j = pl.program_id(1)

    @pl.when(j == 0)
    def _():
        g = g_ref[...]
        x = x_ref[...]
        o_ref[...] = x
        h_ref[HALO:, :] = _rms(x, g).astype(BF16)
        hh = jnp.where(m > 0, _rms(xh_ref[...], g), 0.0)
        h_ref[:HALO, :] = hh.astype(BF16)

    h = h_ref[...]
    ug_ref[...] = _dot(h, wg_ref[...])
    uu_ref[...] = _dot(h, wu_ref[...])

    def conv(u_ref, cw_ref, cb_ref):
        y = cb_ref[...]
        for tap in range(3):
            y = y + cw_ref[tap:tap + 1, :] * u_ref[pl.ds(HALO - 2 + tap, tm), :]
        return y

    yg = conv(ug_ref, cwg_ref, cbg_ref)
    yu = conv(uu_ref, cwu_ref, cbu_ref)
    act = (yg * jax.nn.sigmoid(yg) * yu).astype(BF16)
    o_ref[...] += _dot(act, wd_ref[...])

    if final:
        @pl.when(j == nj - 1)
        def _():
            o_ref[...] = _rms(o_ref[...], fg_ref[...])


def ffn(layer, x, g, w_up, conv_w, conv_b, w_down, fg, tm=1024, tn=512):
    S = x.shape[0]
    nj = D_FF // tn
    hb = tm // HALO
    final = layer == DEPTH - 1
    return pl.pallas_call(
        functools.partial(_ffn_kernel, final, nj, tm),
        grid=(S // tm, nj),
        in_specs=[pl.BlockSpec((tm, D_MODEL), lambda i, j: (i, 0)),
                  pl.BlockSpec((HALO, D_MODEL), lambda i, j: (jnp.maximum(i * hb - 1, 0), 0)),
                  pl.BlockSpec((None, 1, D_MODEL), lambda i, j: (layer, 0, 0)),
                  pl.BlockSpec((None, D_MODEL, tn), lambda i, j: (layer, 0, j)),
                  pl.BlockSpec((None, D_MODEL, tn), lambda i, j: (layer, 0, nj + j)),
                  pl.BlockSpec((None, 3, tn), lambda i, j: (layer, 0, j)),
                  pl.BlockSpec((None, 3, tn), lambda i, j: (layer, 0, nj + j)),
                  pl.BlockSpec((None, 1, tn), lambda i, j: (layer, 0, j)),
                  pl.BlockSpec((None, 1, tn), lambda i, j: (layer, 0, nj + j)),
                  pl.BlockSpec((None, tn, D_MODEL), lambda i, j: (layer, j, 0)),
                  pl.BlockSpec((1, D_MODEL), lambda i, j: (0, 0))],
        out_specs=pl.BlockSpec((tm, D_MODEL), lambda i, j: (i, 0), pipeline_mode=pl.Buffered(1)),
        out_shape=jax.ShapeDtypeStruct((S, D_MODEL), F32),
        scratch_shapes=[pltpu.VMEM((tm + HALO, D_MODEL), BF16),
                        pltpu.VMEM((tm + HALO, tn), F32),
                        pltpu.VMEM((tm + HALO, tn), F32)],
        compiler_params=_cparams(("arbitrary", "arbitrary")),
        name="ffn",
    )(x, x, g, w_up, w_up, conv_w, conv_w, conv_b, conv_b, w_down, fg)


def _rope_pair_cols(w):
    x1, x2 = w[..., :ROPE // 2], w[..., ROPE // 2:]
    z = jnp.zeros(w.shape[:-1] + (LANES - ROPE,), w.dtype)
    return jnp.concatenate([x1, x2, z], axis=-1), jnp.concatenate([x2, x1, z], axis=-1)


def kernel(x, positions, attn_norm_g, w_in, q_norm_g, w_uq, kv_norm_g, w_ukv, hgrn_lb_logits, hgrn_out_norm_g,
           w_branch_a, w_branch_b, w_out, ffn_norm_g, w_up, conv_w, conv_b, w_down, final_norm_g):
    B, S, D = x.shape
    assert B == 1 and D == D_MODEL
    xs = x.reshape(S, D)

    inv_freq = ROPE_THETA ** (-jnp.arange(0, ROPE, 2, dtype=F32) / ROPE)
    zeros = jnp.zeros((LANES - ROPE,), F32)
    invf = jnp.concatenate([inv_freq, inv_freq, zeros]).reshape(1, LANES)
    half = jnp.ones((ROPE // 2,), F32)
    sgn = jnp.concatenate([-half, half, zeros]).reshape(1, LANES)
    cosT, sinT = rope_tables(positions.reshape(S, 1), invf, sgn)

    n_lat = Q_LORA + KV_LORA
    wt = jnp.swapaxes(w_in, 1, 2)
    wq = w_uq.reshape(DEPTH, Q_LORA, HEADS, NOPE + ROPE)
    qa, qb = _rope_pair_cols(wq[..., NOPE:])
    wq = jnp.concatenate([wq[..., :NOPE], qa, qb], axis=-1).reshape(DEPTH, Q_LORA, -1).astype(BF16)
    wkv = w_ukv.astype(BF16)
    wa, wb, wo = w_branch_a.astype(BF16), w_branch_b.astype(BF16), w_out.astype(BF16)
    wup, wdn = w_up.astype(BF16), w_down.astype(BF16)
    conv_b3 = conv_b.reshape(DEPTH, 1, 2 * D_FF)
    lb_logits = hgrn_lb_logits.astype(F32)
    fg = final_norm_g.reshape(1, D)

    rows3 = lambda a: a.reshape(DEPTH, 1, a.shape[-1])
    g_attn, g_q, g_kv, g_hg, g_ffn = map(rows3, (attn_norm_g, q_norm_g, kv_norm_g, hgrn_out_norm_g, ffn_norm_g))

    for l in range(DEPTH):
        cq, ckv, kpe, h = in1(l, xs, g_attn, wt, g_q, g_kv, cosT, sinT)
        hq, hf, hi, hg, ga, gb = in2(l, h, wt, n_lat + ROPE, lb_logits)
        q, k, v = qkv(l, cq, ckv, kpe, cosT, sinT, wq, wkv)
        attn = attention(q, k, v)
        rec = hgrn(l, hq, hf, hi, hg, g_hg)
        xs = merge(l, attn, rec, ga, gb, wa, wb, wo, xs)
        xs = ffn(l, xs, g_ffn, wup, conv_w, conv_b3, wdn, fg)
    return xs.reshape(B, S, D)
```

```python
import functools

import jax
import jax.numpy as jnp
from jax import lax
from jax.experimental import pallas as pl
from jax.experimental.pallas import tpu as pltpu

F32 = jnp.float32
BF16 = jnp.bfloat16

D_MODEL = 2048
DEPTH = 4
HEADS = 16
Q_LORA = 512
KV_LORA = 512
NOPE = 128
ROPE = 64
VDIM = 128
ROPE_THETA = 10000.0
MASK_VALUE = -1e30
HG_D = 128
SUB = 16
MIN_FORGET = 1e-30
D_FF = 5632
EPS = 1e-6
LOG2E = 1.4426950408889634
LANES = 128
QK_PAD = 256
HALO = 16

VMEM_LIMIT = 52 * 1024 * 1024


def _cparams(sem):
    return pltpu.CompilerParams(dimension_semantics=sem, vmem_limit_bytes=VMEM_LIMIT)


def _rms(x, g):
    return x * lax.rsqrt(jnp.mean(x * x, axis=-1, keepdims=True) + EPS) * g


def _dot(a, b):
    return jnp.dot(a, b, preferred_element_type=F32)


def _dot_t(a, bt):
    return lax.dot_general(a, bt, (((1,), (1,)), ((), ())), preferred_element_type=F32)


def _rope_kernel(pos_ref, invf_ref, sgn_ref, cos_ref, sin_ref):
    ang = pos_ref[...].astype(F32) * invf_ref[...]
    sgn = sgn_ref[...]
    cos_ref[...] = jnp.cos(ang) * jnp.abs(sgn)
    sin_ref[...] = jnp.sin(ang) * sgn


def rope_tables(pos_col, invf, sgn, tm=1024):
    S = pos_col.shape[0]
    return pl.pallas_call(
        _rope_kernel,
        grid=(S // tm,),
        in_specs=[pl.BlockSpec((tm, 1), lambda i: (i, 0)),
                  pl.BlockSpec((1, LANES), lambda i: (0, 0)),
                  pl.BlockSpec((1, LANES), lambda i: (0, 0))],
        out_specs=[pl.BlockSpec((tm, LANES), lambda i: (i, 0))] * 2,
        out_shape=[jax.ShapeDtypeStruct((S, LANES), F32)] * 2,
        compiler_params=_cparams(("arbitrary",)),
        name="rope_tables",
    )(pos_col, invf, sgn)


def _in1_kernel(x_ref, g_ref, wlat_ref, wrope_ref, qg_ref, kvg_ref, cos_ref, sin_ref,
                cq_ref, ckv_ref, kpe_ref, h_ref, w_ref):
    n_lat = Q_LORA + KV_LORA
    half = ROPE // 2

    @pl.when(pl.program_id(0) == 0)
    def _():
        w_ref[:n_lat, :] = wlat_ref[0].astype(BF16)
        r = wrope_ref[0].astype(BF16)
        x1, x2 = r[:half], r[half:]
        z = jnp.zeros((LANES - ROPE, D_MODEL), BF16)
        w_ref[n_lat:, :] = jnp.concatenate([x1, x2, z, x2, x1, z], axis=0)

    h = _rms(x_ref[...], g_ref[...]).astype(BF16)
    h_ref[...] = h
    y = _dot_t(h, w_ref[...])
    cq_ref[...] = _rms(y[:, :Q_LORA], qg_ref[...]).astype(BF16)
    ckv_ref[...] = _rms(y[:, Q_LORA:n_lat], kvg_ref[...]).astype(BF16)
    kpe = y[:, n_lat:n_lat + LANES] * cos_ref[...] + y[:, n_lat + LANES:] * sin_ref[...]
    kpe_ref[...] = kpe.astype(BF16)


def in1(layer, x, g, wt, qg, kvg, cosT, sinT, tm=512):
    S = x.shape[0]
    n_lat = Q_LORA + KV_LORA
    row = lambda i: (i, 0)
    lyr = lambda i: (layer, 0, 0)
    elem = lambda n: (pl.Element(1), pl.Element(n), pl.Element(D_MODEL))
    return pl.pallas_call(
        _in1_kernel,
        grid=(S // tm,),
        in_specs=[pl.BlockSpec((tm, D_MODEL), row), pl.BlockSpec((None, 1, D_MODEL), lyr),
                  pl.BlockSpec(elem(n_lat), lyr), pl.BlockSpec(elem(ROPE), lambda i: (layer, n_lat, 0)),
                  pl.BlockSpec((None, 1, Q_LORA), lyr),
                  pl.BlockSpec((None, 1, KV_LORA), lyr), pl.BlockSpec((tm, LANES), row),
                  pl.BlockSpec((tm, LANES), row)],
        out_specs=[pl.BlockSpec((tm, Q_LORA), row), pl.BlockSpec((tm, KV_LORA), row),
                   pl.BlockSpec((tm, LANES), row), pl.BlockSpec((tm, D_MODEL), row)],
        out_shape=[jax.ShapeDtypeStruct((S, Q_LORA), BF16), jax.ShapeDtypeStruct((S, KV_LORA), BF16),
                   jax.ShapeDtypeStruct((S, LANES), BF16), jax.ShapeDtypeStruct((S, D_MODEL), BF16)],
        scratch_shapes=[pltpu.VMEM((n_lat + 2 * LANES, D_MODEL), BF16)],
        compiler_params=_cparams(("arbitrary",)),
        name="in1",
    )(x, g, wt, wt, qg, kvg, cosT, sinT)


def _in2_kernel(layer, h_ref, wq_ref, wf_ref, wi_ref, wg_ref, wa_ref, wb_ref, lbl_ref,
                oq_ref, of_ref, oi_ref, og_ref, oa_ref, ob_ref, w_ref):
    @pl.when(pl.program_id(1) == 0)
    def _():
        for s, src in enumerate((wq_ref, wf_ref, wi_ref, wg_ref, wa_ref, wb_ref)):
            w_ref[s] = src[0].astype(BF16)

    h = h_ref[...]
    y = _dot_t(h, w_ref[0])
    oq_ref[...] = (y * jax.nn.sigmoid(y)).astype(BF16)
    y = _dot_t(h, w_ref[1])
    lg = lbl_ref[...]
    e = jnp.exp(lg - jnp.max(lg, axis=0, keepdims=True))
    p = e / jnp.sum(e, axis=0, keepdims=True)
    lb = jnp.sum(p[:layer + 1], axis=0, keepdims=True) - p[0:1]
    of_ref[...] = lb + (1.0 - lb) * jax.nn.sigmoid(y)
    y = _dot_t(h, w_ref[2])
    oi_ref[...] = y.astype(BF16)
    y = _dot_t(h, w_ref[3])
    og_ref[...] = (y * jax.nn.sigmoid(y)).astype(BF16)
    y = _dot_t(h, w_ref[4])
    oa_ref[...] = jax.nn.sigmoid(y).astype(BF16)
    y = _dot_t(h, w_ref[5])
    ob_ref[...] = jax.nn.sigmoid(y).astype(BF16)


def in2(layer, h, wt, off, lb_logits, tm=1024, tn=256):
    S = h.shape[0]
    nseg = (wt.shape[1] - off) // D_MODEL
    bps = D_MODEL // tn

    def seg_rows(s, j, i):
        return layer, pl.multiple_of(off + (s * bps + j) * tn, HALO), 0

    w_block = (pl.Element(1), pl.Element(tn), pl.Element(D_MODEL))
    w_specs = [pl.BlockSpec(w_block, functools.partial(seg_rows, s)) for s in range(nseg)]
    out_dtypes = [BF16, F32, BF16, BF16, BF16, BF16]
    return pl.pallas_call(
        functools.partial(_in2_kernel, layer),
        grid=(bps, S // tm),
        in_specs=[pl.BlockSpec((tm, D_MODEL), lambda j, i: (i, 0))] + w_specs +
                 [pl.BlockSpec((DEPTH, tn), lambda j, i: (0, j))],
        out_specs=[pl.BlockSpec((tm, tn), lambda j, i: (i, j))] * nseg,
        out_shape=[jax.ShapeDtypeStruct((S, D_MODEL), dt) for dt in out_dtypes],
        scratch_shapes=[pltpu.VMEM((nseg, tn, D_MODEL), BF16)],
        compiler_params=_cparams(("arbitrary", "arbitrary")),
        name="in2",
    )(h, *([wt] * nseg), lb_logits)


def _qkv_kernel(scale, cq_ref, ckv_ref, kpe_ref, cos_ref, sin_ref, wq_ref, wkv_ref, q_ref, k_ref, v_ref):
    cq = cq_ref[...]
    ckv = ckv_ref[...]
    cos = cos_ref[...]
    sin = sin_ref[...]
    kpe = kpe_ref[...]
    qw = NOPE + 2 * LANES
    kvw = NOPE + VDIM
    grp = 4
    for g in range(HEADS // grp):
        yg = _dot(cq, wq_ref[:, g * grp * qw:(g + 1) * grp * qw])
        zg = _dot(ckv, wkv_ref[:, g * grp * kvw:(g + 1) * grp * kvw])
        for hh in range(grp):
            h = g * grp + hh
            y = yg[:, hh * qw:(hh + 1) * qw]
            pe = y[:, NOPE:NOPE + LANES] * cos + y[:, NOPE + LANES:] * sin
            q_ref[h, :, :NOPE] = (y[:, :NOPE] * scale).astype(BF16)
            q_ref[h, :, NOPE:] = (pe * scale).astype(BF16)
            z = zg[:, hh * kvw:(hh + 1) * kvw]
            k_ref[h, :, :NOPE] = z[:, :NOPE].astype(BF16)
            k_ref[h, :, NOPE:] = kpe
            v_ref[h] = z[:, NOPE:].astype(BF16)


def qkv(layer, cq, ckv, kpe, cosT, sinT, wq, wkv, tm=512):
    S = cq.shape[0]
    scale = float((NOPE + ROPE) ** -0.5 * LOG2E)
    row = lambda i: (i, 0)
    return pl.pallas_call(
        functools.partial(_qkv_kernel, scale),
        grid=(S // tm,),
        in_specs=[pl.BlockSpec((tm, Q_LORA), row), pl.BlockSpec((tm, KV_LORA), row),
                  pl.BlockSpec((tm, LANES), row), pl.BlockSpec((tm, LANES), row),
                  pl.BlockSpec((tm, LANES), row),
                  pl.BlockSpec((None,) + wq.shape[1:], lambda i: (layer, 0, 0)),
                  pl.BlockSpec((None,) + wkv.shape[1:], lambda i: (layer, 0, 0))],
        out_specs=[pl.BlockSpec((HEADS, tm, QK_PAD), lambda i: (0, i, 0)),
                   pl.BlockSpec((HEADS, tm, QK_PAD), lambda i: (0, i, 0)),
                   pl.BlockSpec((HEADS, tm, VDIM), lambda i: (0, i, 0))],
        out_shape=[jax.ShapeDtypeStruct((HEADS, S, QK_PAD), BF16),
                   jax.ShapeDtypeStruct((HEADS, S, QK_PAD), BF16),
                   jax.ShapeDtypeStruct((HEADS, S, VDIM), BF16)],
        compiler_params=_cparams(("arbitrary",)),
        name="qkv",
    )(cq, ckv, kpe, cosT, sinT, wq, wkv)


def _run_phases(gens):
    live = list(gens)
    while live:
        live = [g for g in live if next(g, live) is not live]


def _attn_kernel(t, hpb, nsplit, q_ref, k_ref, v_ref, o_ref, m_ref, l_ref, acc_ref):
    i = pl.program_id(1)
    tk = nsplit * t
    m_ref[...] = jnp.full(m_ref.shape, -jnp.inf, F32)
    l_ref[...] = jnp.zeros(l_ref.shape, F32)
    acc_ref[...] = jnp.zeros(acc_ref.shape, F32)

    def chain(hh, part, j, kw, masked, s_head=None):
        nb = kw // LANES
        start = pl.multiple_of(j * tk, tk)
        rows = slice(part * t, (part + 1) * t)
        if s_head is None:
            k = k_ref[hh, pl.ds(start, kw), :]
            s = lax.dot_general(q_ref[hh, rows, :], k, (((1,), (1,)), ((), ())), preferred_element_type=F32)
        else:
            s = s_head[rows]
        yield
        if masked:
            row = lax.broadcasted_iota(jnp.int32, (t, kw), 0) + part * t
            col = lax.broadcasted_iota(jnp.int32, (t, kw), 1)
            s = jnp.where(col <= row, s, MASK_VALUE)
        lane_blocks = lambda x: [x[:, b * LANES:(b + 1) * LANES] for b in range(nb)]
        m_prev = m_ref[hh, rows, :]
        m_cur = functools.reduce(jnp.maximum, lane_blocks(s))
        m_new = jnp.maximum(m_prev, jnp.max(m_cur, axis=-1, keepdims=True))
        alpha = jnp.exp2(m_prev - m_new)
        p = jnp.exp2(s - jnp.concatenate([m_new] * nb, axis=1))
        l_ref[hh, rows, :] = alpha * l_ref[hh, rows, :] + functools.reduce(jnp.add, lane_blocks(p))
        v = v_ref[hh, pl.ds(start, kw), :]
        acc_ref[hh, rows, :] = alpha * acc_ref[hh, rows, :] + _dot(p.astype(BF16), v)
        m_ref[hh, rows, :] = m_new

    def body(j, c):
        start = pl.multiple_of(j * tk, tk)
        s_heads = [lax.dot_general(q_ref[hh], k_ref[hh, pl.ds(start, tk), :], (((1,), (1,)), ((), ())),
                                   preferred_element_type=F32) for hh in range(hpb)]
        _run_phases([chain(hh, part, j, tk, False, s_heads[hh]) for hh in range(hpb) for part in range(nsplit)])
        return c

    lax.fori_loop(0, i, body, 0)
    _run_phases([chain(hh, part, i, (part + 1) * t, True) for hh in range(hpb) for part in range(nsplit)])
    for hh in range(hpb):
        l = jnp.sum(l_ref[hh], axis=-1, keepdims=True)
        o_ref[:, hh * VDIM:(hh + 1) * VDIM] = (acc_ref[hh] / l).astype(BF16)


def attention(q, k, v, t=512, hpb=2, nsplit=2):
    S = q.shape[1]
    tq = nsplit * t
    return pl.pallas_call(
        functools.partial(_attn_kernel, t, hpb, nsplit),
        grid=(HEADS // hpb, S // tq),
        in_specs=[pl.BlockSpec((hpb, tq, QK_PAD), lambda h, i: (h, i, 0)),
                  pl.BlockSpec((hpb, S, QK_PAD), lambda h, i: (h, 0, 0)),
                  pl.BlockSpec((hpb, S, VDIM), lambda h, i: (h, 0, 0))],
        out_specs=pl.BlockSpec((tq, hpb * VDIM), lambda h, i: (i, h)),
        out_shape=jax.ShapeDtypeStruct((S, HEADS * VDIM), BF16),
        scratch_shapes=[pltpu.VMEM((hpb, tq, LANES), F32), pltpu.VMEM((hpb, tq, LANES), F32),
                        pltpu.VMEM((hpb, tq, VDIM), F32)],
        compiler_params=_cparams(("arbitrary", "arbitrary")),
        name="attention",
    )(q, k, v)


def _hgrn_kernel(nblk, hpb, q_ref, f_ref, v_ref, gate_ref, gn_ref, o_ref, st_ref, bs2_ref):
    @pl.when(pl.program_id(1) == 0)
    def _():
        st_ref[...] = jnp.zeros(st_ref.shape, F32)

    B = LANES
    r = lax.broadcasted_iota(jnp.int32, (B, B), 0)
    c = lax.broadcasted_iota(jnp.int32, (B, B), 1)
    same = (r // SUB) == (c // SUB)
    causal = same & (c <= r)
    incl = jnp.where(causal, 1.0, 0.0).astype(BF16)
    nsub = B // SUB
    row_sub = r // SUB
    col_sub = c // SUB
    lane8 = lax.broadcasted_iota(jnp.int32, (SUB // 2, B), 1)
    ones = jnp.ones((HG_D, B), BF16)
    gn = gn_ref[...]

    def one_head(hh, rows):
        cols = slice(hh * HG_D, (hh + 1) * HG_D)
        f = f_ref[rows, cols]
        q = q_ref[rows, cols].astype(F32)
        g = jnp.log(jnp.maximum(f, MIN_FORGET))
        g0 = g.astype(BF16)
        r1 = g - g0.astype(F32)
        g1 = r1.astype(BF16)
        g2 = (r1 - g1.astype(F32)).astype(BF16)
        b3 = _dot(incl, jnp.concatenate([g0, g1, g2], axis=1))
        b = b3[:, :HG_D] + b3[:, HG_D:2 * HG_D] + b3[:, 2 * HG_D:]
        b_last = jnp.concatenate(
            [jnp.broadcast_to(b[(u + 1) * SUB - 1:(u + 1) * SUB], (SUB, HG_D)) for u in range(nsub)], axis=0)
        eb = jnp.exp(b)
        kin = jnp.maximum(1.0 - f, 0.0)
        qe = (q * eb).astype(BF16)
        kd = (kin * jnp.exp(b_last - b)).astype(BF16)
        vb = v_ref[rows, cols]
        b2 = b * LOG2E
        bs2_ref[hh] = b2 - jnp.log2(kin)
        yield
        hs = SUB // 2
        ps = []
        sums = {}
        for u in range(nsub):
            lo = u * SUB
            q16 = q[lo:lo + SUB]
            b16 = b2[lo:lo + SUB]
            pending = None
            for s in range(SUB):
                upper = s >= hs
                qq, bb = (q16[hs:], b16[hs:]) if upper else (q16, b16)
                p = qq * jnp.exp2(bb - bs2_ref[hh, pl.ds(lo + s, 1), :])
                if s % 2 == 1:
                    sums[lo + s] = (upper, jnp.sum(p, axis=-1, keepdims=True))
                elif not upper:
                    ps.append((p.astype(BF16), [(lo + s, 0, SUB)]))
                elif pending is None:
                    pending = (lo + s, p)
                else:
                    both = jnp.concatenate([pending[1], p], axis=0).astype(BF16)
                    ps.append((both, [(pending[0], 0, hs), (lo + s, hs, hs)]))
                    pending = None
        rs = _dot(jnp.concatenate([piece for piece, _ in ps], axis=0), ones)
        yield
        for n, (_, parts) in enumerate(ps):
            for rr, first, count in parts:
                sums[rr] = (count == hs, rs[n * SUB + first:n * SUB + first + count])
        a_rows = []
        for u in range(nsub):
            a_lo = jnp.zeros((hs, B), F32)
            a_hi = jnp.zeros((hs, B), F32)
            for s in range(SUB):
                rr = u * SUB + s
                upper, val = sums[rr]
                if upper:
                    a_hi = jnp.where(lane8 == rr, val, a_hi)
                else:
                    a_lo = jnp.where(lane8 == rr, val[:hs], a_lo)
                    a_hi = jnp.where(lane8 == rr, val[hs:], a_hi)
            a_rows.append(jnp.concatenate([a_lo, a_hi], axis=0))
        a_blk = jnp.where(causal, jnp.concatenate(a_rows, axis=0), 0.0).astype(BF16)

        vt = jnp.transpose(vb)
        vt_u = jnp.concatenate([jnp.where(col_sub == u, vt, 0.0) for u in range(nsub)], axis=0)
        upd = _dot(vt_u, kd)
        yield
        st = st_ref[hh]
        sts = []
        for u in range(nsub):
            sts.append(st.astype(BF16))
            st = st * eb[(u + 1) * SUB - 1:(u + 1) * SUB] + upd[u * B:(u + 1) * B]
        st_ref[hh] = st
        qe_u = jnp.concatenate([jnp.where(row_sub == u, qe, 0.0) for u in range(nsub)], axis=1)
        o_int = lax.dot_general(qe_u, jnp.concatenate(sts, axis=1), (((1,), (1,)), ((), ())),
                                preferred_element_type=F32)
        o = _dot(a_blk, vb) + o_int
        o = _rms(o, gn) * gate_ref[rows, cols].astype(F32)
        o_ref[rows, cols] = o.astype(BF16)

    def blk_body(bi, carry):
        rows = pl.ds(pl.multiple_of(bi * B, B), B)
        _run_phases([one_head(hh, rows) for hh in range(hpb)])
        return carry

    lax.fori_loop(0, nblk, blk_body, 0)


def hgrn(layer, q, f, v, gate, gn, ts=1024, hpb=8):
    S = q.shape[0]
    w = hpb * HG_D
    blk = pl.BlockSpec((ts, w), lambda h, i: (i, h))
    return pl.pallas_call(
        functools.partial(_hgrn_kernel, ts // LANES, hpb),
        grid=(HEADS // hpb, S // ts),
        in_specs=[blk, blk, blk, blk, pl.BlockSpec((None, 1, HG_D), lambda h, i: (layer, 0, 0))],
        out_specs=blk,
        out_shape=jax.ShapeDtypeStruct((S, D_MODEL), BF16),
        scratch_shapes=[pltpu.VMEM((hpb, HG_D, HG_D), F32), pltpu.VMEM((hpb, LANES, HG_D), F32)],
        compiler_params=_cparams(("arbitrary", "arbitrary")),
        name="hgrn",
    )(q, f, v, gate, gn)


def _merge_kernel(a_ref, r_ref, ga_ref, gb_ref, wa_ref, wb_ref, wo_ref, x_ref, o_ref):
    @pl.when(pl.program_id(1) == 0)
    def _():
        o_ref[...] = x_ref[...]

    ma = _dot(a_ref[...], wa_ref[...])
    mb = _dot(r_ref[...], wb_ref[...])
    mg = (ga_ref[...].astype(F32) * ma + gb_ref[...].astype(F32) * mb).astype(BF16)
    o_ref[...] += _dot(mg, wo_ref[...])


def merge(layer, attn, rec, ga, gb, wa, wb, wo, x, tm=512, tn=512):
    S = x.shape[0]
    return pl.pallas_call(
        _merge_kernel,
        grid=(S // tm, D_MODEL // tn),
        in_specs=[pl.BlockSpec((tm, D_MODEL), lambda i, j: (i, 0)),
                  pl.BlockSpec((tm, D_MODEL), lambda i, j: (i, 0)),
                  pl.BlockSpec((tm, tn), lambda i, j: (i, j)),
                  pl.BlockSpec((tm, tn), lambda i, j: (i, j)),
                  pl.BlockSpec((None, D_MODEL, tn), lambda i, j: (layer, 0, j)),
                  pl.BlockSpec((None, D_MODEL, tn), lambda i, j: (layer, 0, j)),
                  pl.BlockSpec((None, tn, D_MODEL), lambda i, j: (layer, j, 0)),
                  pl.BlockSpec((tm, D_MODEL), lambda i, j: (i, 0))],
        out_specs=pl.BlockSpec((tm, D_MODEL), lambda i, j: (i, 0)),
        out_shape=jax.ShapeDtypeStruct((S, D_MODEL), F32),
        compiler_params=_cparams(("arbitrary", "arbitrary")),
        name="merge",
    )(attn, rec, ga, gb, wa, wb, wo, x)


def _ffn_kernel(final, nj, tm, x_ref, xh_ref, g_ref, wg_ref, wu_ref, cwg_ref, cwu_ref, cbg_ref, cbu_ref,
                wd_ref, fg_ref, o_ref, h_ref, ug_ref, uu_ref):
    m = pl.program_id(0)
    j = pl.program_id(1)

    @pl.when(j == 0)
    def _():
        g = g_ref[...]
        x = x_ref[...]
        o_ref[...] = x
        h_ref[HALO:, :] = _rms(x, g).astype(BF16)
        hh = jnp.where(m > 0, _rms(xh_ref[...], g), 0.0)
        h_ref[:HALO, :] = hh.astype(BF16)

    h = h_ref[...]
    ug_ref[...] = _dot(h, wg_ref[...])
    uu_ref[...] = _dot(h, wu_ref[...])

    def conv(u_ref, cw_ref, cb_ref):
        y = cb_ref[...]
        for tap in range(3):
            y = y + cw_ref[tap:tap + 1, :] * u_ref[pl.ds(HALO - 2 + tap, tm), :]
        return y

    yg = conv(ug_ref, cwg_ref, cbg_ref)
    yu = conv(uu_ref, cwu_ref, cbu_ref)
    act = (yg * jax.nn.sigmoid(yg) * yu).astype(BF16)
    o_ref[...] += _dot(act, wd_ref[...])

    if final:
        @pl.when(j == nj - 1)
        def _():
            o_ref[...] = _rms(o_ref[...], fg_ref[...])


def ffn(layer, x, g, w_up, conv_w, conv_b, w_down, fg, tm=512, tn=512):
    S = x.shape[0]
    nj = D_FF // tn
    hb = tm // HALO
    final = layer == DEPTH - 1
    return pl.pallas_call(
        functools.partial(_ffn_kernel, final, nj, tm),
        grid=(S // tm, nj),
        in_specs=[pl.BlockSpec((tm, D_MODEL), lambda i, j: (i, 0)),
                  pl.BlockSpec((HALO, D_MODEL), lambda i, j: (jnp.maximum(i * hb - 1, 0), 0)),
                  pl.BlockSpec((None, 1, D_MODEL), lambda i, j: (layer, 0, 0)),
                  pl.BlockSpec((None, D_MODEL, tn), lambda i, j: (layer, 0, j)),
                  pl.BlockSpec((None, D_MODEL, tn), lambda i, j: (layer, 0, nj + j)),
                  pl.BlockSpec((None, 3, tn), lambda i, j: (layer, 0, j)),
                  pl.BlockSpec((None, 3, tn), lambda i, j: (layer, 0, nj + j)),
                  pl.BlockSpec((None, 1, tn), lambda i, j: (layer, 0, j)),
                  pl.BlockSpec((None, 1, tn), lambda i, j: (layer, 0, nj + j)),
                  pl.BlockSpec((None, tn, D_MODEL), lambda i, j: (layer, j, 0)),
                  pl.BlockSpec((1, D_MODEL), lambda i, j: (0, 0))],
        out_specs=pl.BlockSpec((tm, D_MODEL), lambda i, j: (i, 0)),
        out_shape=jax.ShapeDtypeStruct((S, D_MODEL), F32),
        scratch_shapes=[pltpu.VMEM((tm + HALO, D_MODEL), BF16),
                        pltpu.VMEM((tm + HALO, tn), F32),
                        pltpu.VMEM((tm + HALO, tn), F32)],
        compiler_params=_cparams(("arbitrary", "arbitrary")),
        name="ffn",
    )(x, x, g, w_up, w_up, conv_w, conv_w, conv_b, conv_b, w_down, fg)


def _rope_pair_cols(w):
    x1, x2 = w[..., :ROPE // 2], w[..., ROPE // 2:]
    z = jnp.zeros(w.shape[:-1] + (LANES - ROPE,), w.dtype)
    return jnp.concatenate([x1, x2, z], axis=-1), jnp.concatenate([x2, x1, z], axis=-1)


def kernel(x, positions, attn_norm_g, w_in, q_norm_g, w_uq, kv_norm_g, w_ukv, hgrn_lb_logits, hgrn_out_norm_g,
           w_branch_a, w_branch_b, w_out, ffn_norm_g, w_up, conv_w, conv_b, w_down, final_norm_g):
    B, S, D = x.shape
    assert B == 1 and D == D_MODEL
    xs = x.reshape(S, D)

    inv_freq = ROPE_THETA ** (-jnp.arange(0, ROPE, 2, dtype=F32) / ROPE)
    zeros = jnp.zeros((LANES - ROPE,), F32)
    invf = jnp.concatenate([inv_freq, inv_freq, zeros]).reshape(1, LANES)
    half = jnp.ones((ROPE // 2,), F32)
    sgn = jnp.concatenate([-half, half, zeros]).reshape(1, LANES)
    cosT, sinT = rope_tables(positions.reshape(S, 1), invf, sgn)

    n_lat = Q_LORA + KV_LORA
    wt = jnp.swapaxes(w_in, 1, 2)
    wq = w_uq.reshape(DEPTH, Q_LORA, HEADS, NOPE + ROPE)
    qa, qb = _rope_pair_cols(wq[..., NOPE:])
    wq = jnp.concatenate([wq[..., :NOPE], qa, qb], axis=-1).reshape(DEPTH, Q_LORA, -1).astype(BF16)
    wkv = w_ukv.astype(BF16)
    wa, wb, wo = w_branch_a.astype(BF16), w_branch_b.astype(BF16), w_out.astype(BF16)
    wup, wdn = w_up.astype(BF16), w_down.astype(BF16)
    conv_b3 = conv_b.reshape(DEPTH, 1, 2 * D_FF)
    lb_logits = hgrn_lb_logits.astype(F32)
    fg = final_norm_g.reshape(1, D)

    rows3 = lambda a: a.reshape(DEPTH, 1, a.shape[-1])
    g_attn, g_q, g_kv, g_hg, g_ffn = map(rows3, (attn_norm_g, q_norm_g, kv_norm_g, hgrn_out_norm_g, ffn_norm_g))

    for l in range(DEPTH):
        cq, ckv, kpe, h = in1(l, xs, g_attn, wt, g_q, g_kv, cosT, sinT)
        hq, hf, hi, hg, ga, gb = in2(l, h, wt, n_lat + ROPE, lb_logits)
        q, k, v = qkv(l, cq, ckv, kpe, cosT, sinT, wq, wkv)
        attn = attention(q, k, v)
        rec = hgrn(l, hq, hf, hi, hg, g_hg)
        xs = merge(l, attn, rec, ga, gb, wa, wb, wo, xs)
        xs = ffn(l, xs, g_ffn, wup, conv_w, conv_b3, wdn, fg)
    return xs.reshape(B, S, D)
```
